```python
import math
import jax, jax.numpy as jnp
from jax import lax
import numpy as np

D_MODEL = 1024
BATCH = 16
SEQ = 4096
DEPTH = 2
DEC_BATCH = 8
DEC_SEQ = 64
PAST_LEN = 1024

CHUNK = 64
Q_BLOCK = 128
ROPE_THETA = 10000.0
NEG_INF = -1e30
LN_EPS = 1e-5
NORM_EPS = 1e-6

N_EVEN = (DEPTH + 1) // 2
N_ODD = DEPTH // 2
DN_ALPHA = (2.0 * DEPTH) ** 0.25
DN_BETA = (8.0 * DEPTH) ** -0.25

RET_HEADS = 4
RET_DK = 128
RET_DV = 128
RET_LOG_GAMMA = tuple(math.log(1.0 - 2.0 ** (-5 - h)) for h in range(RET_HEADS))
RET_QK_W = RET_HEADS * RET_DK
RET_V_W = RET_HEADS * RET_DV

MLA_HEADS = 8
MLA_Q_RANK = 384
MLA_KV_RANK = 256
MLA_NOPE = 64
MLA_ROPE = 32
MLA_V = 64

DIFF_HEADS = 8
DIFF_HD = 64
DIFF_V = 2 * DIFF_HD
DIFF_QK_W = 2 * DIFF_HEADS * DIFF_HD

EVEN_IN = 2 * RET_QK_W + 2 * RET_V_W + MLA_Q_RANK + MLA_KV_RANK + MLA_ROPE
EVEN_MIX = RET_V_W + MLA_HEADS * MLA_V
ODD_MIX = DIFF_HEADS * DIFF_V
ODD_IN = 2 * DIFF_QK_W + ODD_MIX

N_EXPERTS = 16
N_GROUPS = 4
EXPERTS_PER_GROUP = N_EXPERTS // N_GROUPS
TOP_K = 2
D_EXPERT = 512

kernel_name = 'streaming_hybrid_retention_mla_diffattn_grouped_moe'


def layer_norm(x, g, b):
    xf = x.astype(jnp.float32)
    mu = jnp.mean(xf, -1, keepdims=True)
    var = jnp.mean(jnp.square(xf - mu), -1, keepdims=True)
    y = (xf - mu) * lax.rsqrt(var + LN_EPS) * g.astype(jnp.float32) + b.astype(jnp.float32)
    return y.astype(x.dtype)


def head_norm(x):
    xf = x.astype(jnp.float32)
    mu = jnp.mean(xf, -1, keepdims=True)
    var = jnp.mean(jnp.square(xf - mu), -1, keepdims=True)
    return ((xf - mu) * lax.rsqrt(var + LN_EPS)).astype(x.dtype)


def rms_norm(x, g):
    xf = x.astype(jnp.float32)
    y = xf * lax.rsqrt(jnp.mean(xf * xf, -1, keepdims=True) + NORM_EPS) * g.astype(jnp.float32)
    return y.astype(x.dtype)


def rope(x, pos):
    d = x.shape[-1]
    inv = 1.0 / (ROPE_THETA ** (jnp.arange(0, d, 2, dtype=jnp.float32) / d))
    ang = pos.astype(jnp.float32)[:, None] * inv[None, :]
    ang = jnp.concatenate([ang, ang], -1)[:, None, :]
    xf = x.astype(jnp.float32)
    rot = jnp.concatenate([-xf[..., d // 2:], xf[..., :d // 2]], -1)
    return (xf * jnp.cos(ang) + rot * jnp.sin(ang)).astype(x.dtype)


def chunk_causal_mask(q_pos, k_pos):
    return (k_pos[None, :] // CHUNK) <= (q_pos[:, None] // CHUNK)


def split_cols(h, sizes):
    out, start = [], 0
    for s in sizes:
        out.append(h[..., start:start + s])
        start += s
    return out


def sweep_query_blocks(fn, q_pos, *q_args):
    s = q_pos.shape[0]
    if s > Q_BLOCK and s % Q_BLOCK == 0:
        nb = s // Q_BLOCK
        def split(a):
            return jnp.moveaxis(a.reshape(a.shape[0], nb, Q_BLOCK, *a.shape[2:]), 1, 0)
        xs = (q_pos.reshape(nb, Q_BLOCK),) + tuple(split(a) for a in q_args)
        out = lax.map(lambda t: fn(t[0], *t[1:]), xs)
        out = jnp.moveaxis(out, 0, 1)
        return out.reshape(out.shape[0], s, *out.shape[3:])
    return fn(q_pos, *q_args)


def retention_chunk(state, q, k, v):
    L = q.shape[1]
    lg = jnp.asarray(RET_LOG_GAMMA, dtype=jnp.float32)
    idx = jnp.arange(L, dtype=jnp.float32)
    diff = idx[:, None] - idx[None, :]
    dmask = jnp.where(diff[None] >= 0.0, jnp.exp(jnp.maximum(diff, 0.0)[None] * lg[:, None, None]), 0.0).astype(q.dtype)
    inner = jnp.einsum('bnhd,bmhd->bhnm', q, k) * dmask[None]
    o = jnp.einsum('bhnm,bmhe->bnhe', inner, v)
    q_decay = jnp.exp((idx[:, None] + 1.0) * lg[None, :]).astype(q.dtype)
    o = o + jnp.einsum('bnhd,bhde->bnhe', q, state) * q_decay[None, :, :, None]
    k_decay = jnp.exp((L - 1.0 - idx)[:, None] * lg[None, :]).astype(q.dtype)
    new_state = (state * jnp.exp(L * lg).astype(state.dtype)[None, :, None, None]
                 + jnp.einsum('bmhd,bmhe->bhde', k * k_decay[None, :, :, None], v))
    return new_state, o


def retention_mixer(q, k, v, g, pos, state0):
    b, s = q.shape[0], q.shape[1]
    q = rope(q, pos)
    k = rope(k, pos) * (RET_DK ** -0.5)
    if state0 is None:
        state0 = jnp.zeros((b, RET_HEADS, RET_DK, RET_DV), q.dtype)
    state0 = state0.astype(q.dtype)
    if s > CHUNK and s % CHUNK == 0:
        nc = s // CHUNK
        def split(a):
            return jnp.moveaxis(a.reshape(b, nc, CHUNK, *a.shape[2:]), 1, 0)
        state, o = lax.scan(lambda st, xs: retention_chunk(st, *xs), state0, (split(q), split(k), split(v)))
        o = jnp.moveaxis(o, 0, 1).reshape(b, s, RET_HEADS, RET_DV)
    else:
        state, o = retention_chunk(state0, q, k, v)
    o = head_norm(o).reshape(b, s, RET_V_W)
    return jax.nn.silu(g) * o, state


def mla_mixer(cq, ckv, kr, pos, w_uq, w_ukv, g_qn, g_kvn, past_lat, past_kr):
    b, s = cq.shape[0], cq.shape[1]
    q = (rms_norm(cq, g_qn) @ w_uq).reshape(b, s, MLA_HEADS, MLA_NOPE + MLA_ROPE)
    q_nope = q[..., :MLA_NOPE]
    q_rope = rope(q[..., MLA_NOPE:], pos)
    latent = rms_norm(ckv, g_kvn)
    krope = rope(kr[:, :, None, :], pos)[:, :, 0, :]
    if past_lat is None:
        lat_all, kr_all, k_pos = latent, krope, pos
    else:
        lat_all = jnp.concatenate([past_lat.astype(latent.dtype), latent], 1)
        kr_all = jnp.concatenate([past_kr.astype(krope.dtype), krope], 1)
        k_pos = jnp.arange(lat_all.shape[1], dtype=jnp.int32)
    kv = (lat_all @ w_ukv).reshape(b, lat_all.shape[1], MLA_HEADS, MLA_NOPE + MLA_V)
    k_nope, v = kv[..., :MLA_NOPE], kv[..., MLA_NOPE:]
    scale = (MLA_NOPE + MLA_ROPE) ** -0.5

    def block(qp, qn, qr):
        sc = jnp.einsum('bqhd,bkhd->bhqk', qn, k_nope) + jnp.einsum('bqhr,bkr->bhqk', qr, kr_all)
        sc = jnp.where(chunk_causal_mask(qp, k_pos)[None, None], sc.astype(jnp.float32) * scale, NEG_INF)
        p = jax.nn.softmax(sc, axis=-1).astype(v.dtype)
        return jnp.einsum('bhqk,bkhd->bqhd', p, v)

    o = sweep_query_blocks(block, pos, q_nope, q_rope)
    return o.reshape(b, s, MLA_HEADS * MLA_V), latent, krope


def even_layer(x, pos, i, past, w_in_even, w_uq, w_ukv, g_qnorm, g_kvnorm, w_out_even):
    b, s, _ = x.shape
    h = x @ w_in_even[i]
    rq, rk, rv, rg, cq, ckv, kr = split_cols(h, (RET_QK_W, RET_QK_W, RET_V_W, RET_V_W, MLA_Q_RANK, MLA_KV_RANK, MLA_ROPE))
    state0, past_lat, past_kr = (None, None, None) if past is None else past
    ret_out, ret_state = retention_mixer(rq.reshape(b, s, RET_HEADS, RET_DK), rk.reshape(b, s, RET_HEADS, RET_DK),
                                         rv.reshape(b, s, RET_HEADS, RET_DV), rg, pos, state0)
    mla_out, lat, krope = mla_mixer(cq, ckv, kr, pos, w_uq[i], w_ukv[i], g_qnorm[i], g_kvnorm[i], past_lat, past_kr)
    y = jnp.concatenate([ret_out, mla_out], -1) @ w_out_even[i]
    return y, ret_state, lat, krope


def odd_layer(x, pos, i, layer_idx, past, w_in_odd, lambda_q1, lambda_k1, lambda_q2, lambda_k2, g_diff_norm, w_out_odd):
    b, s, _ = x.shape
    h = x @ w_in_odd[i]
    q, k, v = split_cols(h, (DIFF_QK_W, DIFF_QK_W, ODD_MIX))
    q = rope(q.reshape(b, s, 2 * DIFF_HEADS, DIFF_HD), pos)
    k = rope(k.reshape(b, s, 2 * DIFF_HEADS, DIFF_HD), pos)
    v = v.reshape(b, s, DIFF_HEADS, DIFF_V)
    lam_init = 0.8 - 0.6 * math.exp(-0.3 * layer_idx)
    lam = (jnp.exp(jnp.sum(lambda_q1[i].astype(jnp.float32) * lambda_k1[i].astype(jnp.float32)))
           - jnp.exp(jnp.sum(lambda_q2[i].astype(jnp.float32) * lambda_k2[i].astype(jnp.float32))) + lam_init)
    if past is None:
        k_all, v_all, k_pos = k, v, pos
    else:
        k_all = jnp.concatenate([past[0].astype(k.dtype), k], 1)
        v_all = jnp.concatenate([past[1].astype(v.dtype), v], 1)
        k_pos = jnp.arange(k_all.shape[1], dtype=jnp.int32)
    scale = DIFF_HD ** -0.5

    def block(qp, qb):
        sc = jnp.einsum('bqhd,bkhd->bhqk', qb, k_all).astype(jnp.float32) * scale
        sc = jnp.where(chunk_causal_mask(qp, k_pos)[None, None], sc, NEG_INF)
        p = jax.nn.softmax(sc, axis=-1)
        p = p.reshape(p.shape[0], DIFF_HEADS, 2, p.shape[2], p.shape[3])
        a = (p[:, :, 0] - lam * p[:, :, 1]).astype(v_all.dtype)
        return jnp.einsum('bhqk,bkhe->bqhe', a, v_all)

    o = sweep_query_blocks(block, pos, q)
    o = rms_norm(o, g_diff_norm[i]) * (1.0 - lam_init)
    y = o.reshape(b, s, ODD_MIX) @ w_out_odd[i]
    return y, k, v


def moe(x, w_router, router_bias, w_gate, w_up, w_down):
    b, s, d = x.shape
    t = x.reshape(-1, d)
    scores = jax.nn.sigmoid((t @ w_router).astype(jnp.float32))
    sel = scores + router_bias.astype(jnp.float32)
    grp_score = jnp.sum(lax.top_k(sel.reshape(-1, N_GROUPS, EXPERTS_PER_GROUP), TOP_K)[0], -1)
    best = jnp.argmax(grp_score, -1)
    in_group = (jnp.arange(N_EXPERTS) // EXPERTS_PER_GROUP)[None, :] == best[:, None]
    _, idx = lax.top_k(jnp.where(in_group, sel, -jnp.inf), TOP_K)
    w = jnp.take_along_axis(scores, idx, -1)
    w = w / jnp.sum(w, -1, keepdims=True)
    gates = jnp.sum(jax.nn.one_hot(idx, N_EXPERTS, dtype=jnp.float32) * w[..., None], 1).astype(x.dtype)
    y = jnp.zeros_like(t)
    for e in range(N_EXPERTS):
        h = jax.nn.silu(t @ w_gate[e]) * (t @ w_up[e])
        y = y + gates[:, e:e + 1] * (h @ w_down[e])
    return y.reshape(b, s, d)


def run_trunk(x, pos, past, w_in_even, w_uq, w_ukv, g_qnorm, g_kvnorm, w_out_even,
              w_in_odd, lambda_q1, lambda_k1, lambda_q2, lambda_k2, g_diff_norm, w_out_odd,
              ln_g, ln_b, w_router, router_bias, w_expert_gate, w_expert_up, w_expert_down):
    ret_states, lats, krs, dks, dvs = [], [], [], [], []
    for l in range(DEPTH):
        if l % 2 == 0:
            e = l // 2
            p = None if past is None else (past[0][e], past[1][e], past[2][e])
            mix, st, lat, kr = even_layer(x, pos, e, p, w_in_even, w_uq, w_ukv, g_qnorm, g_kvnorm, w_out_even)
            ret_states.append(st)
            lats.append(lat)
            krs.append(kr)
        else:
            o = l // 2
            p = None if past is None else (past[3][o], past[4][o])
            mix, kn, vn = odd_layer(x, pos, o, l, p, w_in_odd, lambda_q1, lambda_k1, lambda_q2, lambda_k2,
                                    g_diff_norm, w_out_odd)
            dks.append(kn)
            dvs.append(vn)
        x = layer_norm(DN_ALPHA * x + mix, ln_g[l, 0], ln_b[l, 0])
        ffn = moe(x, w_router, router_bias, w_expert_gate[l], w_expert_up[l], w_expert_down[l])
        x = layer_norm(DN_ALPHA * x + ffn, ln_g[l, 1], ln_b[l, 1])
    return x, jnp.stack(ret_states), jnp.stack(lats), jnp.stack(krs), jnp.stack(dks), jnp.stack(dvs)


def setup_inputs(seed: int = 0) -> dict:
    key = jax.random.key(seed)
    ks = jax.random.split(key, 32)
    f32 = jnp.float32
    nrm = lambda k, shape, sc: jax.random.normal(k, shape, f32) * sc
    return {
        'x_prompt': nrm(ks[0], (BATCH, SEQ, D_MODEL), 1.0),
        'x_sample': nrm(ks[1], (DEC_BATCH, DEC_SEQ, D_MODEL), 1.0),
        'state_ret': nrm(ks[2], (N_EVEN, DEC_BATCH, RET_HEADS, RET_DK, RET_DV), 0.1),
        'cache_mla_latent': nrm(ks[3], (N_EVEN, DEC_BATCH, PAST_LEN, MLA_KV_RANK), 1.0),
        'cache_mla_krope': nrm(ks[4], (N_EVEN, DEC_BATCH, PAST_LEN, MLA_ROPE), 1.0),
        'cache_diff_k': nrm(ks[5], (N_ODD, DEC_BATCH, PAST_LEN, 2 * DIFF_HEADS, DIFF_HD), 1.0),
        'cache_diff_v': nrm(ks[6], (N_ODD, DEC_BATCH, PAST_LEN, DIFF_HEADS, DIFF_V), 1.0),
        'w_in_even': nrm(ks[7], (N_EVEN, D_MODEL, EVEN_IN), D_MODEL ** -0.5),
        'w_uq': nrm(ks[8], (N_EVEN, MLA_Q_RANK, MLA_HEADS * (MLA_NOPE + MLA_ROPE)), MLA_Q_RANK ** -0.5),
        'w_ukv': nrm(ks[9], (N_EVEN, MLA_KV_RANK, MLA_HEADS * (MLA_NOPE + MLA_V)), MLA_KV_RANK ** -0.5),
        'g_qnorm': 1.0 + nrm(ks[10], (N_EVEN, MLA_Q_RANK), 0.02),
        'g_kvnorm': 1.0 + nrm(ks[11], (N_EVEN, MLA_KV_RANK), 0.02),
        'w_out_even': nrm(ks[12], (N_EVEN, EVEN_MIX, D_MODEL), DN_BETA * EVEN_MIX ** -0.5),
        'w_in_odd': nrm(ks[13], (N_ODD, D_MODEL, ODD_IN), D_MODEL ** -0.5),
        'lambda_q1': nrm(ks[14], (N_ODD, DIFF_HD), 0.1),
        'lambda_k1': nrm(ks[15], (N_ODD, DIFF_HD), 0.1),
        'lambda_q2': nrm(ks[16], (N_ODD, DIFF_HD), 0.1),
        'lambda_k2': nrm(ks[17], (N_ODD, DIFF_HD), 0.1),
        'g_diff_norm': 1.0 + nrm(ks[18], (N_ODD, DIFF_V), 0.02),
        'w_out_odd': nrm(ks[19], (N_ODD, ODD_MIX, D_MODEL), DN_BETA * ODD_MIX ** -0.5),
        'ln_g': 1.0 + nrm(ks[20], (DEPTH, 2, D_MODEL), 0.02),
        'ln_b': nrm(ks[21], (DEPTH, 2, D_MODEL), 0.02),
        'w_router': nrm(ks[22], (D_MODEL, N_EXPERTS), D_MODEL ** -0.5),
        'router_bias': nrm(ks[23], (N_EXPERTS,), 0.01),
        'w_expert_gate': nrm(ks[24], (DEPTH, N_EXPERTS, D_MODEL, D_EXPERT), D_MODEL ** -0.5),
        'w_expert_up': nrm(ks[25], (DEPTH, N_EXPERTS, D_MODEL, D_EXPERT), D_MODEL ** -0.5),
        'w_expert_down': nrm(ks[26], (DEPTH, N_EXPERTS, D_EXPERT, D_MODEL), DN_BETA * D_EXPERT ** -0.5),
    }


def reference(x_prompt, x_sample, state_ret, cache_mla_latent, cache_mla_krope, cache_diff_k, cache_diff_v,
              w_in_even, w_uq, w_ukv, g_qnorm, g_kvnorm, w_out_even,
              w_in_odd, lambda_q1, lambda_k1, lambda_q2, lambda_k2, g_diff_norm, w_out_odd,
              ln_g, ln_b, w_router, router_bias, w_expert_gate, w_expert_up, w_expert_down):
    pos_p = jnp.arange(x_prompt.shape[1], dtype=jnp.int32)
    past_len = cache_mla_latent.shape[2]
    pos_s = past_len + jnp.arange(x_sample.shape[1], dtype=jnp.int32)
    y_prompt, p_ret, p_lat, p_kr, p_dk, p_dv = run_trunk(
        x_prompt, pos_p, None, w_in_even, w_uq, w_ukv, g_qnorm, g_kvnorm, w_out_even,
        w_in_odd, lambda_q1, lambda_k1, lambda_q2, lambda_k2, g_diff_norm, w_out_odd,
        ln_g, ln_b, w_router, router_bias, w_expert_gate, w_expert_up, w_expert_down)
    past = (state_ret, cache_mla_latent, cache_mla_krope, cache_diff_k, cache_diff_v)
    y_sample, s_ret, s_lat, s_kr, s_dk, s_dv = run_trunk(
        x_sample, pos_s, past, w_in_even, w_uq, w_ukv, g_qnorm, g_kvnorm, w_out_even,
        w_in_odd, lambda_q1, lambda_k1, lambda_q2, lambda_k2, g_diff_norm, w_out_odd,
        ln_g, ln_b, w_router, router_bias, w_expert_gate, w_expert_up, w_expert_down)
    return (y_prompt, y_sample, p_ret, p_lat, p_kr, p_dk, p_dv, s_ret, s_lat, s_kr, s_dk, s_dv)
```

```python
import functools
import math

import numpy as np
import jax
import jax.numpy as jnp
from jax import lax
from jax.experimental import pallas as pl
from jax.experimental.pallas import tpu as pltpu

F32 = jnp.float32
BF16 = jnp.bfloat16

CHUNK = 64
ROPE_THETA = 10000.0
NEG_INF = -1e30
LN_EPS = 1e-5
NORM_EPS = 1e-6
DEPTH = 2
DN_ALPHA = (2.0 * DEPTH) ** 0.25
RET_HEADS = 4
RET_DK = 128
RET_DV = 128
RET_LOG_GAMMA = tuple(math.log(1.0 - 2.0 ** (-5 - h)) for h in range(RET_HEADS))
MLA_HEADS = 8
MLA_Q_RANK = 384
MLA_KV_RANK = 256
MLA_NOPE = 64
MLA_ROPE = 32
MLA_V = 64
DIFF_HEADS = 8
DIFF_HD = 64
DIFF_V = 128
N_EXPERTS = 16
N_GROUPS = 4
EXPERTS_PER_GROUP = 4
LOG2E = math.log2(math.e)

LANES = 128
RET_W = RET_HEADS * RET_DK
MLA_PAD = MLA_HEADS * LANES
VMEM_LIMIT = 56 * 1024 * 1024


def _cparams(sem):
    return pltpu.CompilerParams(dimension_semantics=sem, vmem_limit_bytes=VMEM_LIMIT)


def _rope_tables(pos, d, group, offset, scale):
    pos = np.asarray(pos, np.float64)
    half = d // 2
    inv = 1.0 / (ROPE_THETA ** (np.arange(0, d, 2, dtype=np.float64) / d))
    ang = pos[:, None] * inv[None, :]
    cos = np.full((pos.shape[0], LANES), scale, np.float64)
    s_lo = np.zeros((pos.shape[0], LANES), np.float64)
    s_hi = np.zeros((pos.shape[0], LANES), np.float64)
    start = offset
    while start + d <= LANES:
        cos[:, start:start + half] = np.cos(ang) * scale
        cos[:, start + half:start + d] = np.cos(ang) * scale
        s_lo[:, start:start + half] = -np.sin(ang) * scale
        s_hi[:, start + half:start + d] = np.sin(ang) * scale
        start += group
    return cos, s_lo, s_hi


def _even_tables(pos):
    rq = _rope_tables(pos, RET_DK, LANES, 0, 1.0)
    rk = _rope_tables(pos, RET_DK, LANES, 0, RET_DK ** -0.5)
    c = (MLA_NOPE + MLA_ROPE) ** -0.5 * LOG2E
    mq = _rope_tables(pos, MLA_ROPE, LANES, MLA_NOPE, c)
    mk = _rope_tables(pos, MLA_ROPE, LANES, MLA_NOPE, 1.0)
    tabs = [rq[0], rq[1] + rq[2], rk[0], rk[1] + rk[2], mq[0], mq[1], mq[2], mk[0], mk[1], mk[2]]
    return jnp.asarray(np.stack(tabs).astype(np.float32))


def _odd_tables(pos):
    c = DIFF_HD ** -0.5 * LOG2E
    dq = _rope_tables(pos, DIFF_HD, DIFF_HD, 0, c)
    dk = _rope_tables(pos, DIFF_HD, DIFF_HD, 0, 1.0)
    return jnp.asarray(np.stack(list(dq) + list(dk)).astype(np.float32))


def _retention_tables(L):
    lg = np.asarray(RET_LOG_GAMMA, np.float64)
    idx = np.arange(L, dtype=np.float64)
    diff = idx[:, None] - idx[None, :]
    dmask = np.where(diff[None] >= 0, np.exp(np.maximum(diff, 0.0)[None] * lg[:, None, None]), 0.0)
    qd = np.exp((idx[None, :] + 1.0) * lg[:, None])
    kd = np.exp((L - 1.0 - idx)[None, :] * lg[:, None])
    gl = np.exp(L * lg)
    qd = np.broadcast_to(qd[:, :, None], (RET_HEADS, L, LANES))
    kd = np.broadcast_to(kd[:, :, None], (RET_HEADS, L, LANES))
    gl = np.broadcast_to(gl[:, None, None], (RET_HEADS, RET_DK, RET_DV))
    f = lambda a: jnp.asarray(np.ascontiguousarray(a).astype(np.float32))
    return f(dmask), f(qd), f(kd), f(gl)


def _dot(a, b):
    return jnp.dot(a, b, preferred_element_type=F32)


def _dot_nt(a, b):
    return lax.dot_general(a, b, (((1,), (1,)), ((), ())), preferred_element_type=F32)


def _dot_tn(a, b):
    return lax.dot_general(a, b, (((0,), (0,)), ((), ())), preferred_element_type=F32)


def _rope(x, cos, s_lo, s_hi, half):
    return x * cos + pltpu.roll(x, LANES - half, 1) * s_lo + pltpu.roll(x, half, 1) * s_hi


def _layer_norm(x, g, b):
    mu = jnp.mean(x, -1, keepdims=True)
    xc = x - mu
    var = jnp.mean(xc * xc, -1, keepdims=True)
    return xc * lax.rsqrt(var + LN_EPS) * g + b


def _rms_norm(x, g):
    return x * lax.rsqrt(jnp.mean(x * x, -1, keepdims=True) + NORM_EPS) * g


def _even_in_kernel(x_ref, w_ref, wq_ref, gq_ref, gkv_ref, tab_ref,
                    rq_ref, rk_ref, rv_ref, rg_ref, q_ref, lat_ref, kr_ref):
    xb = x_ref[...].astype(BF16)
    c_rq, s_rq, c_rk, s_rk = tab_ref[0], tab_ref[1], tab_ref[2], tab_ref[3]
    hq = _dot(xb, w_ref[:, 0:RET_W])
    hk = _dot(xb, w_ref[:, RET_W:2 * RET_W])
    for h in range(RET_HEADS):
        sl = slice(h * LANES, (h + 1) * LANES)
        xq = hq[:, sl]
        rq_ref[:, sl] = (xq * c_rq + pltpu.roll(xq, RET_DK // 2, 1) * s_rq).astype(BF16)
        xk = hk[:, sl]
        rk_ref[:, sl] = (xk * c_rk + pltpu.roll(xk, RET_DK // 2, 1) * s_rk).astype(BF16)
    rv_ref[...] = _dot(xb, w_ref[:, 2 * RET_W:3 * RET_W]).astype(BF16)
    rg_ref[...] = _dot(xb, w_ref[:, 3 * RET_W:4 * RET_W]).astype(BF16)
    o = 4 * RET_W
    cq = _dot(xb, w_ref[:, o:o + MLA_Q_RANK])
    qn = _rms_norm(cq, gq_ref[...]).astype(BF16)
    qf = _dot(qn, wq_ref[...])
    c_q, lo_q, hi_q = tab_ref[4], tab_ref[5], tab_ref[6]
    for h in range(MLA_HEADS):
        sl = slice(h * LANES, (h + 1) * LANES)
        q_ref[:, sl] = _rope(qf[:, sl], c_q, lo_q, hi_q, MLA_ROPE // 2).astype(BF16)
    o += MLA_Q_RANK
    ckv = _dot(xb, w_ref[:, o:o + MLA_KV_RANK])
    lat_ref[...] = _rms_norm(ckv, gkv_ref[...])
    o += MLA_KV_RANK
    krp = _dot(xb, w_ref[:, o:o + LANES])
    krp = _rope(krp, tab_ref[7], tab_ref[8], tab_ref[9], MLA_ROPE // 2)
    kr_ref[...] = krp[:, MLA_NOPE:MLA_NOPE + MLA_ROPE]


def _even_in(x, w_a, wq, gq, gkv, tabs, tm):
    T, D = x.shape
    P = tabs.shape[1]
    nt = P // tm
    row = lambda i: (i, 0)
    const = lambda i: (0, 0)
    outs = [
        jax.ShapeDtypeStruct((T, RET_W), BF16), jax.ShapeDtypeStruct((T, RET_W), BF16),
        jax.ShapeDtypeStruct((T, RET_W), BF16), jax.ShapeDtypeStruct((T, RET_W), BF16),
        jax.ShapeDtypeStruct((T, MLA_PAD), BF16),
        jax.ShapeDtypeStruct((T, MLA_KV_RANK), F32), jax.ShapeDtypeStruct((T, MLA_ROPE), F32),
    ]
    return pl.pallas_call(
        _even_in_kernel,
        grid=(T // tm,),
        in_specs=[
            pl.BlockSpec((tm, D), row),
            pl.BlockSpec(w_a.shape, const),
            pl.BlockSpec(wq.shape, const),
            pl.BlockSpec(gq.shape, const),
            pl.BlockSpec(gkv.shape, const),
            pl.BlockSpec((tabs.shape[0], tm, LANES), lambda i: (0, i % nt, 0)),
        ],
        out_specs=[
            pl.BlockSpec((tm, RET_W), row), pl.BlockSpec((tm, RET_W), row),
            pl.BlockSpec((tm, RET_W), row), pl.BlockSpec((tm, RET_W), row),
            pl.BlockSpec((tm, MLA_PAD), row),
            pl.BlockSpec((tm, MLA_KV_RANK), row), pl.BlockSpec((tm, MLA_ROPE), row),
        ],
        out_shape=outs,
        compiler_params=_cparams(("parallel",)),
        name="even_in",
    )(x, w_a, wq, gq, gkv, tabs)


def _kv_up_kernel(lat_ref, kr_ref, wk_ref, wv_ref, e_ref, k_ref, v_ref):
    lb = lat_ref[...].astype(BF16)
    krb = kr_ref[...].astype(BF16)
    k_ref[...] = (_dot(lb, wk_ref[...]) + _dot(krb, e_ref[...])).astype(BF16)
    v_ref[...] = _dot(lb, wv_ref[...]).astype(BF16)


def _kv_up(lat, kr, wk, wv, e_mat, tm):
    T = lat.shape[0]
    row = lambda i: (i, 0)
    const = lambda i: (0, 0)
    return pl.pallas_call(
        _kv_up_kernel,
        grid=(T // tm,),
        in_specs=[
            pl.BlockSpec((tm, MLA_KV_RANK), row), pl.BlockSpec((tm, MLA_ROPE), row),
            pl.BlockSpec(wk.shape, const), pl.BlockSpec(wv.shape, const), pl.BlockSpec(e_mat.shape, const),
        ],
        out_specs=[pl.BlockSpec((tm, MLA_PAD), row), pl.BlockSpec((tm, MLA_HEADS * MLA_V), row)],
        out_shape=[jax.ShapeDtypeStruct((T, MLA_PAD), BF16), jax.ShapeDtypeStruct((T, MLA_HEADS * MLA_V), BF16)],
        compiler_params=_cparams(("parallel",)),
        name="kv_up",
    )(lat, kr, wk, wv, e_mat)


def _retention_kernel(q_ref, k_ref, v_ref, g_ref, s0_ref, dm_ref, qd_ref, kd_ref, gl_ref,
                      o_ref, st_ref, *, L, nchunk):
    @pl.when(pl.program_id(1) == 0)
    def _():
        st_ref[...] = s0_ref[...]

    for c in range(nchunk):
        rows = slice(c * L, (c + 1) * L)
        for h in range(RET_HEADS):
            sl = slice(h * LANES, (h + 1) * LANES)
            q = q_ref[rows, sl]
            k = k_ref[rows, sl]
            v = v_ref[rows, sl]
            st = st_ref[0, h]
            a = (_dot_nt(q, k) * dm_ref[h]).astype(BF16)
            o = _dot(a, v) + _dot(q, st.astype(BF16)) * qd_ref[h]
            kdec = (k.astype(F32) * kd_ref[h]).astype(BF16)
            st_ref[0, h] = st * gl_ref[h] + _dot_tn(kdec, v)
            mu = jnp.mean(o, -1, keepdims=True)
            oc = o - mu
            var = jnp.mean(oc * oc, -1, keepdims=True)
            on = oc * lax.rsqrt(var + LN_EPS)
            g = g_ref[rows, sl].astype(F32)
            o_ref[rows, sl] = (g * jax.nn.sigmoid(g) * on).astype(BF16)


def _retention(rq, rk, rv, rg, state0, B, S):
    L = min(S, 256)
    lt = min(S, 512)
    nj = S // lt
    dm, qd, kd, gl = _retention_tables(L)
    row = lambda b, j: (b * nj + j, 0)
    c3 = lambda b, j: (0, 0, 0)
    st_spec = pl.BlockSpec((1, RET_HEADS, RET_DK, RET_DV), lambda b, j: (b, 0, 0, 0))
    return pl.pallas_call(
        functools.partial(_retention_kernel, L=L, nchunk=lt // L),
        grid=(B, nj),
        in_specs=[pl.BlockSpec((lt, RET_W), row)] * 4 + [
            st_spec,
            pl.BlockSpec(dm.shape, c3), pl.BlockSpec(qd.shape, c3),
            pl.BlockSpec(kd.shape, c3), pl.BlockSpec(gl.shape, c3),
        ],
        out_specs=[pl.BlockSpec((lt, RET_W), row), st_spec],
        out_shape=[jax.ShapeDtypeStruct((B * S, RET_W), BF16),
                   jax.ShapeDtypeStruct((B, RET_HEADS, RET_DK, RET_DV), F32)],
        compiler_params=_cparams(("parallel", "arbitrary")),
        name="retention",
    )(rq, rk, rv, rg, state0, dm, qd, kd, gl)


def _flash(q, k_at, v_at, n_full, n_diag, diag_off, tq, tk, dv):
    def step(j, carry, rel):
        m, l, acc = carry
        s = _dot_nt(q, k_at(j))
        if rel is not None:
            r = lax.broadcasted_iota(jnp.int32, (tq, tk), 0) // CHUNK
            c = (lax.broadcasted_iota(jnp.int32, (tq, tk), 1) + rel) // CHUNK
            s = jnp.where(c <= r, s, NEG_INF)
        m_new = jnp.maximum(m, jnp.max(s, -1, keepdims=True))
        p = jnp.exp2(s - m_new)
        alpha = jnp.exp2(m - m_new)
        l = alpha * l + jnp.sum(p, -1, keepdims=True)
        acc = alpha * acc + _dot(p.astype(BF16), v_at(j))
        return m_new, l, acc

    carry = (jnp.full((tq, 1), NEG_INF, F32), jnp.zeros((tq, 1), F32), jnp.zeros((tq, dv), F32))
    carry = lax.fori_loop(0, n_full, lambda j, c: step(j, c, None), carry)
    for d in range(n_diag):
        carry = step(n_full + d, carry, diag_off + d * tk)
    _, l, acc = carry
    return acc / l


def _attn_tiles(i, tq, tk, sk, causal):
    if causal:
        return i * (tq // tk), tq // tk
    return sk // tk, 0


def _mla_attn_kernel(q_ref, k_ref, v_ref, o_ref, *, tq, tk, sk, causal):
    n_full, n_diag = _attn_tiles(pl.program_id(2), tq, tk, sk, causal)
    v_at = lambda j: v_ref[pl.ds(pl.multiple_of(j * tk, tk), tk), :]
    outs = []
    for hh in range(2):
        sl = slice(hh * LANES, (hh + 1) * LANES)
        k_at = lambda j, sl=sl: k_ref[pl.ds(pl.multiple_of(j * tk, tk), tk), sl]
        outs.append(_flash(q_ref[:, sl], k_at, v_at, n_full, n_diag, 0, tq, tk, LANES))
    lane = lax.broadcasted_iota(jnp.int32, (tq, LANES), 1)
    o_ref[...] = jnp.where(lane < MLA_V, outs[0], outs[1]).astype(BF16)


def _mla_attn(q, k, v, B, sq, sk, tq, tk, causal):
    nq = sq // tq
    npair = MLA_HEADS // 2
    return pl.pallas_call(
        functools.partial(_mla_attn_kernel, tq=tq, tk=tk, sk=sk, causal=causal),
        grid=(B, npair, nq),
        in_specs=[
            pl.BlockSpec((tq, 2 * LANES), lambda b, p, i: (b * nq + i, p)),
            pl.BlockSpec((sk, 2 * LANES), lambda b, p, i: (b, p)),
            pl.BlockSpec((sk, LANES), lambda b, p, i: (b, p)),
        ],
        out_specs=pl.BlockSpec((tq, LANES), lambda b, p, i: (b * nq + i, p)),
        out_shape=jax.ShapeDtypeStruct((B * sq, MLA_HEADS * MLA_V), BF16),
        compiler_params=_cparams(("parallel", "parallel", "arbitrary")),
        name="mla_attn",
    )(q, k, v)


def _diff_attn_kernel(lam_ref, gn_ref, q_ref, k_ref, v_ref, o_ref, *, tq, tk, sk, causal, lam_init):
    n_full, n_diag = _attn_tiles(pl.program_id(2), tq, tk, sk, causal)
    k_at = lambda j: k_ref[pl.ds(pl.multiple_of(j * tk, tk), tk), :]
    v_at = lambda j: v_ref[pl.ds(pl.multiple_of(j * tk, tk), tk), :]
    lane = lax.broadcasted_iota(jnp.int32, (tq, LANES), 1)
    q = q_ref[...]
    zero = jnp.zeros_like(q)
    o1 = _flash(jnp.where(lane < DIFF_HD, q, zero), k_at, v_at, n_full, n_diag, 0, tq, tk, LANES)
    o2 = _flash(jnp.where(lane < DIFF_HD, zero, q), k_at, v_at, n_full, n_diag, 0, tq, tk, LANES)
    lv = lam_ref[...]
    lam = (jnp.exp(jnp.sum(lv[0:1] * lv[1:2], -1, keepdims=True))
           - jnp.exp(jnp.sum(lv[2:3] * lv[3:4], -1, keepdims=True)) + lam_init)
    o = o1 - lam * o2
    o_ref[...] = (_rms_norm(o, gn_ref[...]) * (1.0 - lam_init)).astype(BF16)


def _diff_attn(lam_vecs, gn, q, k, v, B, sq, sk, tq, tk, causal, lam_init):
    nq = sq // tq
    const = lambda b, h, i: (0, 0)
    return pl.pallas_call(
        functools.partial(_diff_attn_kernel, tq=tq, tk=tk, sk=sk, causal=causal, lam_init=lam_init),
        grid=(B, DIFF_HEADS, nq),
        in_specs=[
            pl.BlockSpec(lam_vecs.shape, const), pl.BlockSpec(gn.shape, const),
            pl.BlockSpec((tq, LANES), lambda b, h, i: (b * nq + i, h)),
            pl.BlockSpec((sk, LANES), lambda b, h, i: (b, h)),
            pl.BlockSpec((sk, LANES), lambda b, h, i: (b, h)),
        ],
        out_specs=pl.BlockSpec((tq, LANES), lambda b, h, i: (b * nq + i, h)),
        out_shape=jax.ShapeDtypeStruct((B * sq, DIFF_HEADS * DIFF_V), BF16),
        compiler_params=_cparams(("parallel", "parallel", "arbitrary")),
        name="diff_attn",
    )(lam_vecs, gn, q, k, v)


def _out_proj_kernel(*refs, n_in):
    a_refs = refs[:n_in]
    w_ref, x_ref, g_ref, b_ref, o_ref = refs[n_in:]
    y = None
    off = 0
    for a_ref in a_refs:
        width = a_ref.shape[1]
        part = _dot(a_ref[...], w_ref[off:off + width, :])
        y = part if y is None else y + part
        off += width
    o_ref[...] = _layer_norm(DN_ALPHA * x_ref[...] + y, g_ref[...], b_ref[...])


def _out_proj(acts, w, x, g, b, tm):
    T, D = x.shape
    row = lambda i: (i, 0)
    const = lambda i: (0, 0)
    return pl.pallas_call(
        functools.partial(_out_proj_kernel, n_in=len(acts)),
        grid=(T // tm,),
        in_specs=[pl.BlockSpec((tm, a.shape[1]), row) for a in acts] + [
            pl.BlockSpec(w.shape, const), pl.BlockSpec((tm, D), row),
            pl.BlockSpec(g.shape, const), pl.BlockSpec(b.shape, const),
        ],
        out_specs=pl.BlockSpec((tm, D), row),
        out_shape=jax.ShapeDtypeStruct((T, D), F32),
        compiler_params=_cparams(("parallel",)),
        name="out_proj",
    )(*acts, w, x, g, b)


def _odd_in_kernel(x_ref, w_ref, tab_ref, q_ref, kf_ref, kb_ref, vf_ref, vb_ref):
    xb = x_ref[...].astype(BF16)
    W = DIFF_HEADS * 2 * DIFF_HD
    hq = _dot(xb, w_ref[:, 0:W])
    hk = _dot(xb, w_ref[:, W:2 * W])
    for h in range(W // LANES):
        sl = slice(h * LANES, (h + 1) * LANES)
        q_ref[:, sl] = _rope(hq[:, sl], tab_ref[0], tab_ref[1], tab_ref[2], DIFF_HD // 2).astype(BF16)
        kk = _rope(hk[:, sl], tab_ref[3], tab_ref[4], tab_ref[5], DIFF_HD // 2)
        kf_ref[:, sl] = kk
        kb_ref[:, sl] = kk.astype(BF16)
    hv = _dot(xb, w_ref[:, 2 * W:])
    vf_ref[...] = hv
    vb_ref[...] = hv.astype(BF16)


def _odd_in(x, w, tabs, tm):
    T, D = x.shape
    W = DIFF_HEADS * 2 * DIFF_HD
    nt = tabs.shape[1] // tm
    row = lambda i: (i, 0)
    blk = pl.BlockSpec((tm, W), row)
    return pl.pallas_call(
        _odd_in_kernel,
        grid=(T // tm,),
        in_specs=[pl.BlockSpec((tm, D), row), pl.BlockSpec(w.shape, lambda i: (0, 0)),
                  pl.BlockSpec((tabs.shape[0], tm, LANES), lambda i: (0, i % nt, 0))],
        out_specs=[blk] * 5,
        out_shape=[jax.ShapeDtypeStruct((T, W), BF16), jax.ShapeDtypeStruct((T, W), F32),
                   jax.ShapeDtypeStruct((T, W), BF16), jax.ShapeDtypeStruct((T, W), F32),
                   jax.ShapeDtypeStruct((T, W), BF16)],
        compiler_params=_cparams(("parallel",)),
        name="odd_in",
    )(x, w, tabs)


def _router_kernel(x_ref, wr_ref, bias_ref, g_ref, gt_ref):
    tm = x_ref.shape[0]
    logits = _dot_nt(wr_ref[...], x_ref[...].astype(BF16))
    sc = jax.nn.sigmoid(logits)
    sel = sc + bias_ref[...]
    r = [sel[e:e + 1, :] for e in range(N_EXPERTS)]
    s = [sc[e:e + 1, :] for e in range(N_EXPERTS)]
    grp = []
    for g in range(N_GROUPS):
        a, b, c, d = r[4 * g:4 * g + 4]
        top2 = jnp.maximum(jnp.maximum(jnp.maximum(a + b, a + c), jnp.maximum(a + d, b + c)),
                           jnp.maximum(b + d, c + d))
        grp.append(top2)
    best = jnp.maximum(jnp.maximum(grp[0], grp[1]), jnp.maximum(grp[2], grp[3]))
    taken = jnp.zeros((1, tm), jnp.bool_)
    chosen = []
    for g in range(N_GROUPS):
        win = jnp.logical_and(grp[g] == best, jnp.logical_not(taken))
        chosen.append(win)
        taken = jnp.logical_or(taken, win)
    picked = []
    for e in range(N_EXPERTS):
        g = e // EXPERTS_PER_GROUP
        rank = jnp.zeros((1, tm), F32)
        for k in range(4 * g, 4 * g + 4):
            if k < e:
                rank = rank + (r[k] >= r[e]).astype(F32)
            elif k > e:
                rank = rank + (r[k] > r[e]).astype(F32)
        picked.append(jnp.logical_and(chosen[g], rank < 2.0))
    w = [jnp.where(picked[e], s[e], 0.0) for e in range(N_EXPERTS)]
    denom = w[0]
    for e in range(1, N_EXPERTS):
        denom = denom + w[e]
    gt_ref[...] = jnp.zeros_like(gt_ref)
    for e in range(N_EXPERTS):
        gt_ref[e:e + 1, :] = w[e] / denom
    g_ref[...] = gt_ref[...].T


def _router(x, wr_t, bias, tm):
    T, D = x.shape
    return pl.pallas_call(
        _router_kernel,
        grid=(T // tm,),
        in_specs=[pl.BlockSpec((tm, D), lambda i: (i, 0)), pl.BlockSpec(wr_t.shape, lambda i: (0, 0)),
                  pl.BlockSpec(bias.shape, lambda i: (0, 0))],
        out_specs=pl.BlockSpec((tm, LANES), lambda i: (i, 0)),
        out_shape=jax.ShapeDtypeStruct((T, LANES), F32),
        scratch_shapes=[pltpu.VMEM((LANES, tm), F32)],
        compiler_params=_cparams(("parallel",)),
        name="router",
    )(x, wr_t, bias)


def _moe_dense_kernel(x_ref, gate_ref, wg_ref, wu_ref, wd_ref, g_ref, b_ref, o_ref, xb_ref, acc_ref):
    e = pl.program_id(1)

    @pl.when(e == 0)
    def _():
        xb_ref[...] = x_ref[...].astype(BF16)
        acc_ref[...] = jnp.zeros_like(acc_ref)

    xb = xb_ref[...]
    h = jax.nn.silu(_dot(xb, wg_ref[0, 0])) * _dot(xb, wu_ref[0, 0])
    y = _dot(h.astype(BF16), wd_ref[0, 0])
    gates = gate_ref[...]
    lane = lax.broadcasted_iota(jnp.int32, gates.shape, 1)
    gcol = jnp.sum(jnp.where(lane == e, gates, 0.0), -1, keepdims=True)
    acc_ref[...] += gcol * y

    @pl.when(e == pl.num_programs(1) - 1)
    def _():
        o_ref[...] = _layer_norm(DN_ALPHA * x_ref[...] + acc_ref[...], g_ref[...], b_ref[...])


def _moe_dense(x, gates, wg, wu, wd, l, g, b, tm):
    T, D = x.shape
    _, E, _, H = wg.shape
    row = lambda i, e: (i, 0)
    const = lambda i, e: (0, 0)
    wsel = lambda i, e: (l, e, 0, 0)
    return pl.pallas_call(
        _moe_dense_kernel,
        grid=(T // tm, E),
        in_specs=[
            pl.BlockSpec((tm, D), row), pl.BlockSpec((tm, LANES), row),
            pl.BlockSpec((1, 1, D, H), wsel), pl.BlockSpec((1, 1, D, H), wsel),
            pl.BlockSpec((1, 1, H, D), wsel),
            pl.BlockSpec(g.shape, const), pl.BlockSpec(b.shape, const),
        ],
        out_specs=pl.BlockSpec((tm, D), row),
        out_shape=jax.ShapeDtypeStruct((T, D), F32),
        scratch_shapes=[pltpu.VMEM((tm, D), BF16), pltpu.VMEM((tm, D), F32)],
        compiler_params=_cparams(("parallel", "arbitrary")),
        name="moe_dense",
    )(x, gates, wg, wu, wd, g, b)


def _prep_weights(w_in_even, w_uq, w_ukv, w_out_even, w_in_odd, w_out_odd, w_router,
                  w_expert_gate, w_expert_up, w_expert_down):
    d = w_in_even.shape[1]
    n_main = 4 * RET_W + MLA_Q_RANK + MLA_KV_RANK
    w_in = w_in_even[0]
    kr_cols = jnp.pad(w_in[:, n_main:], ((0, 0), (MLA_NOPE, LANES - MLA_NOPE - MLA_ROPE)))
    w_a = jnp.concatenate([w_in[:, :n_main], kr_cols], 1).astype(BF16)
    qd = MLA_NOPE + MLA_ROPE
    wq = jnp.pad(w_uq[0].reshape(MLA_Q_RANK, MLA_HEADS, qd), ((0, 0), (0, 0), (0, LANES - qd)))
    wq = wq.reshape(MLA_Q_RANK, MLA_PAD).astype(BF16)
    wkv = w_ukv[0].reshape(MLA_KV_RANK, MLA_HEADS, MLA_NOPE + MLA_V)
    wk = jnp.pad(wkv[:, :, :MLA_NOPE], ((0, 0), (0, 0), (0, LANES - MLA_NOPE)))
    wk = wk.reshape(MLA_KV_RANK, MLA_PAD).astype(BF16)
    wv = wkv[:, :, MLA_NOPE:].reshape(MLA_KV_RANK, MLA_HEADS * MLA_V).astype(BF16)
    e_np = np.zeros((MLA_ROPE, MLA_HEADS, LANES), np.float32)
    for j in range(MLA_ROPE):
        e_np[j, :, MLA_NOPE + j] = 1.0
    e_mat = jnp.asarray(e_np.reshape(MLA_ROPE, MLA_PAD)).astype(BF16)
    return dict(
        w_a=w_a, wq=wq, wk=wk, wv=wv, e_mat=e_mat,
        w_out_even=w_out_even[0].astype(BF16), w_in_odd=w_in_odd[0].astype(BF16),
        w_out_odd=w_out_odd[0].astype(BF16), wr_t=w_router.T.astype(BF16),
        wg=w_expert_gate.astype(BF16), wu=w_expert_up.astype(BF16), wd=w_expert_down.astype(BF16),
    )


def _moe(x, wts, l, bias, ln_g, ln_b, tm):
    gates = _router(x, wts["wr_t"], bias, tm)
    return _moe_dense(x, gates, wts["wg"], wts["wu"], wts["wd"], l, ln_g[l, 1][None], ln_b[l, 1][None], tm)


def _trunk(x3, pos0, past, wts, prm):
    B, S, D = x3.shape
    T = B * S
    x = x3.reshape(T, D)
    tm = min(T, 512)
    rep = max(tm // S, 1)
    pos = np.tile(pos0 + np.arange(S), rep)
    ln_g, ln_b = prm["ln_g"], prm["ln_b"]

    rq, rk, rv, rg, q, lat, kr = _even_in(x, wts["w_a"], wts["wq"], prm["gq"], prm["gkv"], _even_tables(pos), tm)
    if past is None:
        state0 = jnp.zeros((B, RET_HEADS, RET_DK, RET_DV), F32)
        lat_all, kr_all, sk = lat, kr, S
    else:
        state0 = past["state"]
        sk = past["lat"].shape[1] + S
        lat_all = jnp.concatenate([past["lat"], lat.reshape(B, S, -1)], 1).reshape(B * sk, -1)
        kr_all = jnp.concatenate([past["kr"], kr.reshape(B, S, -1)], 1).reshape(B * sk, -1)
    ret_out, ret_state = _retention(rq, rk, rv, rg, state0, B, S)
    tkv = 512 if (B * sk) % 512 == 0 else sk
    k_mla, v_mla = _kv_up(lat_all, kr_all, wts["wk"], wts["wv"], wts["e_mat"], tkv)
    causal = past is None
    tq = min(S, 256)
    tk = tq if causal else sk
    mla_out = _mla_attn(q, k_mla, v_mla, B, S, sk, tq, tk, causal)
    x = _out_proj([ret_out, mla_out], wts["w_out_even"], x, ln_g[0, 0][None], ln_b[0, 0][None], tm)
    x = _moe(x, wts, 0, prm["bias"], ln_g, ln_b, tm)

    qd, kf, kb, vf, vb = _odd_in(x, wts["w_in_odd"], _odd_tables(pos), tm)
    if past is None:
        k_all, v_all = kb, vb
    else:
        W = kb.shape[1]
        k_all = jnp.concatenate([past["dk"], kb.reshape(B, S, W)], 1).reshape(B * sk, W)
        v_all = jnp.concatenate([past["dv"], vb.reshape(B, S, W)], 1).reshape(B * sk, W)
    lam_init = 0.8 - 0.6 * math.exp(-0.3 * 1)
    d_out = _diff_attn(prm["lam"], prm["gn"], qd, k_all, v_all, B, S, sk, tq, tk, causal, lam_init)
    x = _out_proj([d_out], wts["w_out_odd"], x, ln_g[1, 0][None], ln_b[1, 0][None], tm)
    x = _moe(x, wts, 1, prm["bias"], ln_g, ln_b, tm)

    return (x.reshape(B, S, D), ret_state[None], lat.reshape(1, B, S, -1), kr.reshape(1, B, S, -1),
            kf.reshape(1, B, S, 2 * DIFF_HEADS, DIFF_HD), vf.reshape(1, B, S, DIFF_HEADS, DIFF_V))


def kernel(x_prompt, x_sample, state_ret, cache_mla_latent, cache_mla_krope, cache_diff_k, cache_diff_v,
           w_in_even, w_uq, w_ukv, g_qnorm, g_kvnorm, w_out_even,
           w_in_odd, lambda_q1, lambda_k1, lambda_q2, lambda_k2, g_diff_norm, w_out_odd,
           ln_g, ln_b, w_router, router_bias, w_expert_gate, w_expert_up, w_expert_down):
    wts = _prep_weights(w_in_even, w_uq, w_ukv, w_out_even, w_in_odd, w_out_odd, w_router,
                        w_expert_gate, w_expert_up, w_expert_down)
    prm = dict(
        gq=g_qnorm[0][None].astype(F32), gkv=g_kvnorm[0][None].astype(F32),
        lam=jnp.stack([lambda_q1[0], lambda_k1[0], lambda_q2[0], lambda_k2[0]]).astype(F32),
        gn=g_diff_norm[0][None].astype(F32), bias=router_bias.reshape(N_EXPERTS, 1).astype(F32),
        ln_g=ln_g.astype(F32), ln_b=ln_b.astype(F32),
    )
    past_len = cache_mla_latent.shape[2]
    db = x_sample.shape[0]
    past = dict(
        state=state_ret[0].astype(F32), lat=cache_mla_latent[0], kr=cache_mla_krope[0],
        dk=cache_diff_k[0].reshape(db, past_len, -1).astype(BF16),
        dv=cache_diff_v[0].reshape(db, past_len, -1).astype(BF16),
    )
    outs_p = _trunk(x_prompt, 0, None, wts, prm)
    outs_s = _trunk(x_sample, past_len, past, wts, prm)
    return (outs_p[0], outs_s[0]) + outs_p[1:] + outs_s[1:]
```

```python
import functools
import math

import numpy as np
import jax
import jax.numpy as jnp
from jax import lax
from jax.experimental import pallas as pl
from jax.experimental.pallas import tpu as pltpu

F32 = jnp.float32
BF16 = jnp.bfloat16

CHUNK = 64
ROPE_THETA = 10000.0
NEG_INF = -1e30
LN_EPS = 1e-5
NORM_EPS = 1e-6
DEPTH = 2
DN_ALPHA = (2.0 * DEPTH) ** 0.25
RET_HEADS = 4
RET_DK = 128
RET_DV = 128
RET_LOG_GAMMA = tuple(math.log(1.0 - 2.0 ** (-5 - h)) for h in range(RET_HEADS))
MLA_HEADS = 8
MLA_Q_RANK = 384
MLA_KV_RANK = 256
MLA_NOPE = 64
MLA_ROPE = 32
MLA_V = 64
DIFF_HEADS = 8
DIFF_HD = 64
DIFF_V = 128
N_EXPERTS = 16
N_GROUPS = 4
EXPERTS_PER_GROUP = 4
LOG2E = math.log2(math.e)

LANES = 128
RET_W = RET_HEADS * RET_DK
MLA_PAD = MLA_HEADS * LANES
VMEM_LIMIT = 56 * 1024 * 1024


def _cparams(sem):
    return pltpu.CompilerParams(dimension_semantics=sem, vmem_limit_bytes=VMEM_LIMIT)


def _rope_tables(pos, d, group, offset, scale):
    pos = np.asarray(pos, np.float64)
    half = d // 2
    inv = 1.0 / (ROPE_THETA ** (np.arange(0, d, 2, dtype=np.float64) / d))
    ang = pos[:, None] * inv[None, :]
    cos = np.full((pos.shape[0], LANES), scale, np.float64)
    s_lo = np.zeros((pos.shape[0], LANES), np.float64)
    s_hi = np.zeros((pos.shape[0], LANES), np.float64)
    start = offset
    while start + d <= LANES:
        cos[:, start:start + half] = np.cos(ang) * scale
        cos[:, start + half:start + d] = np.cos(ang) * scale
        s_lo[:, start:start + half] = -np.sin(ang) * scale
        s_hi[:, start + half:start + d] = np.sin(ang) * scale
        start += group
    return cos, s_lo, s_hi


def _even_tables(pos):
    rq = _rope_tables(pos, RET_DK, LANES, 0, 1.0)
    rk = _rope_tables(pos, RET_DK, LANES, 0, RET_DK ** -0.5)
    c = (MLA_NOPE + MLA_ROPE) ** -0.5 * LOG2E
    mq = _rope_tables(pos, MLA_ROPE, LANES, MLA_NOPE, c)
    mk = _rope_tables(pos, MLA_ROPE, LANES, MLA_NOPE, 1.0)
    tabs = [rq[0], rq[1] + rq[2], rk[0], rk[1] + rk[2], mq[0], mq[1], mq[2], mk[0], mk[1], mk[2]]
    return jnp.asarray(np.stack(tabs).astype(np.float32))


def _odd_tables(pos):
    c = DIFF_HD ** -0.5 * LOG2E
    dq = _rope_tables(pos, DIFF_HD, DIFF_HD, 0, c)
    dk = _rope_tables(pos, DIFF_HD, DIFF_HD, 0, 1.0)
    return jnp.asarray(np.stack(list(dq) + list(dk)).astype(np.float32))


def _retention_tables(L):
    lg = np.asarray(RET_LOG_GAMMA, np.float64)
    idx = np.arange(L, dtype=np.float64)
    diff = idx[:, None] - idx[None, :]
    dmask = np.where(diff[None] >= 0, np.exp(np.maximum(diff, 0.0)[None] * lg[:, None, None]), 0.0)
    qd = np.exp((idx[None, :] + 1.0) * lg[:, None])
    kd = np.exp((L - 1.0 - idx)[None, :] * lg[:, None])
    gl = np.exp(L * lg)
    qd = np.broadcast_to(qd[:, :, None], (RET_HEADS, L, LANES))
    kd = np.broadcast_to(kd[:, :, None], (RET_HEADS, L, LANES))
    gl = np.broadcast_to(gl[:, None, None], (RET_HEADS, RET_DK, RET_DV))
    f = lambda a: jnp.asarray(np.ascontiguousarray(a).astype(np.float32))
    return f(dmask), f(qd), f(kd), f(gl)


def _dot(a, b):
    return jnp.dot(a, b, preferred_element_type=F32)


def _dot_nt(a, b):
    return lax.dot_general(a, b, (((1,), (1,)), ((), ())), preferred_element_type=F32)


def _dot_tn(a, b):
    return lax.dot_general(a, b, (((0,), (0,)), ((), ())), preferred_element_type=F32)


def _rope(x, cos, s_lo, s_hi, half):
    return x * cos + pltpu.roll(x, LANES - half, 1) * s_lo + pltpu.roll(x, half, 1) * s_hi


def _layer_norm(x, g, b):
    mu = jnp.mean(x, -1, keepdims=True)
    xc = x - mu
    var = jnp.mean(xc * xc, -1, keepdims=True)
    return xc * lax.rsqrt(var + LN_EPS) * g + b


def _rms_norm(x, g):
    return x * lax.rsqrt(jnp.mean(x * x, -1, keepdims=True) + NORM_EPS) * g


def _even_in_kernel(x_ref, w_ref, wq_ref, gq_ref, gkv_ref, tab_ref,
                    rq_ref, rk_ref, rv_ref, rg_ref, q_ref, lat_ref, kr_ref):
    xb = x_ref[...].astype(BF16)
    c_rq, s_rq, c_rk, s_rk = tab_ref[0], tab_ref[1], tab_ref[2], tab_ref[3]
    hq = _dot(xb, w_ref[:, 0:RET_W])
    hk = _dot(xb, w_ref[:, RET_W:2 * RET_W])
    for h in range(RET_HEADS):
        sl = slice(h * LANES, (h + 1) * LANES)
        xq = hq[:, sl]
        rq_ref[:, sl] = (xq * c_rq + pltpu.roll(xq, RET_DK // 2, 1) * s_rq).astype(BF16)
        xk = hk[:, sl]
        rk_ref[:, sl] = (xk * c_rk + pltpu.roll(xk, RET_DK // 2, 1) * s_rk).astype(BF16)
    rv_ref[...] = _dot(xb, w_ref[:, 2 * RET_W:3 * RET_W]).astype(BF16)
    rg_ref[...] = _dot(xb, w_ref[:, 3 * RET_W:4 * RET_W]).astype(BF16)
    o = 4 * RET_W
    cq = _dot(xb, w_ref[:, o:o + MLA_Q_RANK])
    qn = _rms_norm(cq, gq_ref[...]).astype(BF16)
    qf = _dot(qn, wq_ref[...])
    c_q, lo_q, hi_q = tab_ref[4], tab_ref[5], tab_ref[6]
    for h in range(MLA_HEADS):
        sl = slice(h * LANES, (h + 1) * LANES)
        q_ref[:, sl] = _rope(qf[:, sl], c_q, lo_q, hi_q, MLA_ROPE // 2).astype(BF16)
    o += MLA_Q_RANK
    ckv = _dot(xb, w_ref[:, o:o + MLA_KV_RANK])
    lat_ref[...] = _rms_norm(ckv, gkv_ref[...])
    o += MLA_KV_RANK
    krp = _dot(xb, w_ref[:, o:o + LANES])
    krp = _rope(krp, tab_ref[7], tab_ref[8], tab_ref[9], MLA_ROPE // 2)
    kr_ref[...] = krp[:, MLA_NOPE:MLA_NOPE + MLA_ROPE]


def _even_in(x, w_a, wq, gq, gkv, tabs, tm):
    T, D = x.shape
    P = tabs.shape[1]
    nt = P // tm
    row = lambda i: (i, 0)
    const = lambda i: (0, 0)
    outs = [
        jax.ShapeDtypeStruct((T, RET_W), BF16), jax.ShapeDtypeStruct((T, RET_W), BF16),
        jax.ShapeDtypeStruct((T, RET_W), BF16), jax.ShapeDtypeStruct((T, RET_W), BF16),
        jax.ShapeDtypeStruct((T, MLA_PAD), BF16),
        jax.ShapeDtypeStruct((T, MLA_KV_RANK), F32), jax.ShapeDtypeStruct((T, MLA_ROPE), F32),
    ]
    return pl.pallas_call(
        _even_in_kernel,
        grid=(T // tm,),
        in_specs=[
            pl.BlockSpec((tm, D), row),
            pl.BlockSpec(w_a.shape, const),
            pl.BlockSpec(wq.shape, const),
            pl.BlockSpec(gq.shape, const),
            pl.BlockSpec(gkv.shape, const),
            pl.BlockSpec((tabs.shape[0], tm, LANES), lambda i: (0, i % nt, 0)),
        ],
        out_specs=[
            pl.BlockSpec((tm, RET_W), row), pl.BlockSpec((tm, RET_W), row),
            pl.BlockSpec((tm, RET_W), row), pl.BlockSpec((tm, RET_W), row),
            pl.BlockSpec((tm, MLA_PAD), row),
            pl.BlockSpec((tm, MLA_KV_RANK), row), pl.BlockSpec((tm, MLA_ROPE), row),
        ],
        out_shape=outs,
        compiler_params=_cparams(("parallel",)),
        name="even_in",
    )(x, w_a, wq, gq, gkv, tabs)


def _kv_up_kernel(lat_ref, kr_ref, wk_ref, wvt_ref, e_ref, k_ref, vt_ref):
    lb = lat_ref[...].astype(BF16)
    krb = kr_ref[...].astype(BF16)
    k_ref[...] = (_dot(lb, wk_ref[...]) + _dot(krb, e_ref[...])).astype(BF16)
    vt_ref[0] = _dot_nt(wvt_ref[...], lb).astype(BF16)


def _vt_layout(T, S, tm, width):
    nb, cols = (T // S, S) if S % tm == 0 else (1, T)
    nt = cols // tm
    spec = pl.BlockSpec((1, width, tm), lambda i: (i // nt, 0, i % nt))
    return spec, jax.ShapeDtypeStruct((nb, width, cols), BF16)


def _kv_up(lat, kr, wk, wvt, e_mat, S, tm):
    T = lat.shape[0]
    row = lambda i: (i, 0)
    const = lambda i: (0, 0)
    vt_spec, vt_shape = _vt_layout(T, S, tm, MLA_HEADS * MLA_V)
    return pl.pallas_call(
        _kv_up_kernel,
        grid=(T // tm,),
        in_specs=[
            pl.BlockSpec((tm, MLA_KV_RANK), row), pl.BlockSpec((tm, MLA_ROPE), row),
            pl.BlockSpec(wk.shape, const), pl.BlockSpec(wvt.shape, const), pl.BlockSpec(e_mat.shape, const),
        ],
        out_specs=[pl.BlockSpec((tm, MLA_PAD), row), vt_spec],
        out_shape=[jax.ShapeDtypeStruct((T, MLA_PAD), BF16), vt_shape],
        compiler_params=_cparams(("parallel",)),
        name="kv_up",
    )(lat, kr, wk, wvt, e_mat)


def _retention_kernel(q_ref, k_ref, v_ref, g_ref, s0_ref, dm_ref, qd_ref, kd_ref, gl_ref,
                      o_ref, st_ref, *, L, nchunk):
    @pl.when(pl.program_id(1) == 0)
    def _():
        st_ref[...] = s0_ref[...]

    for c in range(nchunk):
        rows = slice(c * L, (c + 1) * L)
        for h in range(RET_HEADS):
            sl = slice(h * LANES, (h + 1) * LANES)
            q = q_ref[rows, sl]
            k = k_ref[rows, sl]
            v = v_ref[rows, sl]
            st = st_ref[0, h]
            a = (_dot_nt(q, k) * dm_ref[h]).astype(BF16)
            o = _dot(a, v) + _dot(q, st.astype(BF16)) * qd_ref[h]
            kdec = (k.astype(F32) * kd_ref[h]).astype(BF16)
            st_ref[0, h] = st * gl_ref[h] + _dot_tn(kdec, v)
            mu = jnp.mean(o, -1, keepdims=True)
            oc = o - mu
            var = jnp.mean(oc * oc, -1, keepdims=True)
            on = oc * lax.rsqrt(var + LN_EPS)
            g = g_ref[rows, sl].astype(F32)
            o_ref[rows, sl] = (g * jax.nn.sigmoid(g) * on).astype(BF16)


def _retention(rq, rk, rv, rg, state0, B, S):
    L = min(S, 256)
    lt = min(S, 512)
    nj = S // lt
    dm, qd, kd, gl = _retention_tables(L)
    row = lambda b, j: (b * nj + j, 0)
    c3 = lambda b, j: (0, 0, 0)
    st_spec = pl.BlockSpec((1, RET_HEADS, RET_DK, RET_DV), lambda b, j: (b, 0, 0, 0))
    return pl.pallas_call(
        functools.partial(_retention_kernel, L=L, nchunk=lt // L),
        grid=(B, nj),
        in_specs=[pl.BlockSpec((lt, RET_W), row)] * 4 + [
            st_spec,
            pl.BlockSpec(dm.shape, c3), pl.BlockSpec(qd.shape, c3),
            pl.BlockSpec(kd.shape, c3), pl.BlockSpec(gl.shape, c3),
        ],
        out_specs=[pl.BlockSpec((lt, RET_W), row), st_spec],
        out_shape=[jax.ShapeDtypeStruct((B * S, RET_W), BF16),
                   jax.ShapeDtypeStruct((B, RET_HEADS, RET_DK, RET_DV), F32)],
        compiler_params=_cparams(("parallel", "arbitrary")),
        name="retention",
    )(rq, rk, rv, rg, state0, dm, qd, kd, gl)


def _key_tile(j, tk):
    return pl.ds(j * tk, tk) if isinstance(j, int) else pl.ds(pl.multiple_of(j * tk, tk), tk)


def _query_t(q, tq):
    q = q.astype(F32)
    if tq < LANES:
        q = jnp.concatenate([q, jnp.zeros((LANES - tq, LANES), F32)], 0)
    return q.T


def _flash_t(streams, qi, tq, tk, sk, sk_valid, causal, m_ref, l_ref, acc_ref):
    for s in range(len(streams)):
        m_ref[s] = jnp.full(m_ref.shape[1:], NEG_INF, F32)
        l_ref[s] = jnp.zeros(l_ref.shape[1:], F32)
        acc_ref[s] = jnp.zeros(acc_ref.shape[1:], F32)

    def step(j, rel=None, valid=None):
        for s, (q_t, k_at, vt_at) in enumerate(streams):
            st = _dot(k_at(j), q_t)
            if rel is not None:
                kc = (lax.broadcasted_iota(jnp.int32, st.shape, 0) + rel) // CHUNK
                qc = lax.broadcasted_iota(jnp.int32, st.shape, 1) // CHUNK
                st = jnp.where(kc <= qc, st, NEG_INF)
            if valid is not None:
                st = jnp.where(lax.broadcasted_iota(jnp.int32, st.shape, 0) < valid, st, NEG_INF)
            m_old = m_ref[s]
            m_new = jnp.maximum(m_old, jnp.max(st, 0, keepdims=True))
            p = jnp.exp2(st - m_new)
            alpha = jnp.exp2(m_old - m_new)
            l_ref[s] = alpha * l_ref[s] + jnp.sum(p, 0, keepdims=True)
            acc_ref[s] = acc_ref[s] * alpha + _dot(vt_at(j), p.astype(BF16))
            m_ref[s] = m_new

    if causal:
        n_full = qi * (tq // tk)

        def body(j, c):
            step(j)
            return c

        lax.fori_loop(0, n_full, body, 0)
        for d in range(tq // tk):
            step(n_full + d, rel=d * tk)
    else:
        n = sk // tk
        for j in range(n):
            last_valid = sk_valid - j * tk
            step(j, valid=last_valid if last_valid < tk else None)
    return [acc_ref[s] / l_ref[s] for s in range(len(streams))]


def _attn_scratch(n_streams, dv, tq):
    tqp = max(tq, LANES)
    return [pltpu.VMEM((n_streams, 1, tqp), F32), pltpu.VMEM((n_streams, 1, tqp), F32),
            pltpu.VMEM((n_streams, dv, tqp), F32)]


def _mla_attn_kernel(q_ref, k_ref, vt_ref, o_ref, m_ref, l_ref, acc_ref, *, tq, tk, sk, sk_valid, causal):
    streams = []
    for hh in range(2):
        sl = slice(hh * LANES, (hh + 1) * LANES)
        vrows = slice(hh * MLA_V, (hh + 1) * MLA_V)
        k_at = lambda j, sl=sl: k_ref[_key_tile(j, tk), sl]
        vt_at = lambda j, vrows=vrows: vt_ref[0, vrows, _key_tile(j, tk)]
        streams.append((_query_t(q_ref[:, sl], tq).astype(BF16), k_at, vt_at))
    outs = _flash_t(streams, pl.program_id(2), tq, tk, sk, sk_valid, causal, m_ref, l_ref, acc_ref)
    o_ref[...] = jnp.concatenate(outs, 0).T[:tq].astype(BF16)


def _mla_attn(q, k, vt, B, sq, sk, sk_valid, tq, tk, causal):
    nq = sq // tq
    npair = MLA_HEADS // 2
    return pl.pallas_call(
        functools.partial(_mla_attn_kernel, tq=tq, tk=tk, sk=sk, sk_valid=sk_valid, causal=causal),
        grid=(B, npair, nq),
        in_specs=[
            pl.BlockSpec((tq, 2 * LANES), lambda b, p, i: (b * nq + i, p)),
            pl.BlockSpec((sk, 2 * LANES), lambda b, p, i: (b, p)),
            pl.BlockSpec((1, 2 * MLA_V, sk), lambda b, p, i: (b, p, 0)),
        ],
        out_specs=pl.BlockSpec((tq, LANES), lambda b, p, i: (b * nq + i, p)),
        out_shape=jax.ShapeDtypeStruct((B * sq, MLA_HEADS * MLA_V), BF16),
        scratch_shapes=_attn_scratch(2, MLA_V, tq),
        compiler_params=_cparams(("parallel", "parallel", "arbitrary")),
        name="mla_attn",
    )(q, k, vt)


def _diff_attn_kernel(lam_ref, gn_ref, q_ref, k_ref, vt_ref, o_ref, m_ref, l_ref, acc_ref,
                      *, tq, tk, sk, sk_valid, causal, lam_init):
    k_at = lambda j: k_ref[_key_tile(j, tk), :]
    vt_at = lambda j: vt_ref[0, :, _key_tile(j, tk)]
    q_t = _query_t(q_ref[...], tq)
    feat = lax.broadcasted_iota(jnp.int32, q_t.shape, 0)
    zero = jnp.zeros_like(q_t)
    q1 = jnp.where(feat < DIFF_HD, q_t, zero).astype(BF16)
    q2 = jnp.where(feat < DIFF_HD, zero, q_t).astype(BF16)
    o1, o2 = _flash_t([(q1, k_at, vt_at), (q2, k_at, vt_at)], pl.program_id(2), tq, tk, sk, sk_valid,
                      causal, m_ref, l_ref, acc_ref)
    lv = lam_ref[...]
    lam = (jnp.exp(jnp.sum(lv[0:1] * lv[1:2], -1, keepdims=True))
           - jnp.exp(jnp.sum(lv[2:3] * lv[3:4], -1, keepdims=True)) + lam_init)
    o = (o1 - lam * o2).T[:tq]
    o_ref[...] = (_rms_norm(o, gn_ref[...]) * (1.0 - lam_init)).astype(BF16)


def _diff_attn(lam_vecs, gn, q, k, vt, B, sq, sk, sk_valid, tq, tk, causal, lam_init):
    nq = sq // tq
    const = lambda b, h, i: (0, 0)
    return pl.pallas_call(
        functools.partial(_diff_attn_kernel, tq=tq, tk=tk, sk=sk, sk_valid=sk_valid, causal=causal,
                          lam_init=lam_init),
        grid=(B, DIFF_HEADS, nq),
        in_specs=[
            pl.BlockSpec(lam_vecs.shape, const), pl.BlockSpec(gn.shape, const),
            pl.BlockSpec((tq, LANES), lambda b, h, i: (b * nq + i, h)),
            pl.BlockSpec((sk, LANES), lambda b, h, i: (b, h)),
            pl.BlockSpec((1, DIFF_V, sk), lambda b, h, i: (b, h, 0)),
        ],
        out_specs=pl.BlockSpec((tq, LANES), lambda b, h, i: (b * nq + i, h)),
        out_shape=jax.ShapeDtypeStruct((B * sq, DIFF_HEADS * DIFF_V), BF16),
        scratch_shapes=_attn_scratch(2, DIFF_V, tq),
        compiler_params=_cparams(("parallel", "parallel", "arbitrary")),
        name="diff_attn",
    )(lam_vecs, gn, q, k, vt)


def _out_proj_kernel(*refs, n_in):
    a_refs = refs[:n_in]
    w_ref, x_ref, g_ref, b_ref, o_ref = refs[n_in:]
    y = None
    off = 0
    for a_ref in a_refs:
        width = a_ref.shape[1]
        part = _dot(a_ref[...], w_ref[off:off + width, :])
        y = part if y is None else y + part
        off += width
    o_ref[...] = _layer_norm(DN_ALPHA * x_ref[...] + y, g_ref[...], b_ref[...])


def _out_proj(acts, w, x, g, b, tm):
    T, D = x.shape
    row = lambda i: (i, 0)
    const = lambda i: (0, 0)
    return pl.pallas_call(
        functools.partial(_out_proj_kernel, n_in=len(acts)),
        grid=(T // tm,),
        in_specs=[pl.BlockSpec((tm, a.shape[1]), row) for a in acts] + [
            pl.BlockSpec(w.shape, const), pl.BlockSpec((tm, D), row),
            pl.BlockSpec(g.shape, const), pl.BlockSpec(b.shape, const),
        ],
        out_specs=pl.BlockSpec((tm, D), row),
        out_shape=jax.ShapeDtypeStruct((T, D), F32),
        compiler_params=_cparams(("parallel",)),
        name="out_proj",
    )(*acts, w, x, g, b)


def _odd_in_kernel(x_ref, w_ref, tab_ref, q_ref, kf_ref, kb_ref, vf_ref, vt_ref):
    xb = x_ref[...].astype(BF16)
    W = DIFF_HEADS * 2 * DIFF_HD
    hq = _dot(xb, w_ref[:, 0:W])
    hk = _dot(xb, w_ref[:, W:2 * W])
    for h in range(W // LANES):
        sl = slice(h * LANES, (h + 1) * LANES)
        q_ref[:, sl] = _rope(hq[:, sl], tab_ref[0], tab_ref[1], tab_ref[2], DIFF_HD // 2).astype(BF16)
        kk = _rope(hk[:, sl], tab_ref[3], tab_ref[4], tab_ref[5], DIFF_HD // 2)
        kf_ref[:, sl] = kk
        kb_ref[:, sl] = kk.astype(BF16)
    hv = _dot(xb, w_ref[:, 2 * W:])
    vf_ref[...] = hv
    vt_ref[0] = hv.T.astype(BF16)


def _odd_in(x, w, tabs, S, tm):
    T, D = x.shape
    W = DIFF_HEADS * 2 * DIFF_HD
    nt = tabs.shape[1] // tm
    row = lambda i: (i, 0)
    blk = pl.BlockSpec((tm, W), row)
    vt_spec, vt_shape = _vt_layout(T, S, tm, DIFF_HEADS * DIFF_V)
    return pl.pallas_call(
        _odd_in_kernel,
        grid=(T // tm,),
        in_specs=[pl.BlockSpec((tm, D), row), pl.BlockSpec(w.shape, lambda i: (0, 0)),
                  pl.BlockSpec((tabs.shape[0], tm, LANES), lambda i: (0, i % nt, 0))],
        out_specs=[blk] * 4 + [vt_spec],
        out_shape=[jax.ShapeDtypeStruct((T, W), BF16), jax.ShapeDtypeStruct((T, W), F32),
                   jax.ShapeDtypeStruct((T, W), BF16), jax.ShapeDtypeStruct((T, W), F32), vt_shape],
        compiler_params=_cparams(("parallel",)),
        name="odd_in",
    )(x, w, tabs)


def _router_kernel(x_ref, wr_ref, bias_ref, g_ref, gt_ref):
    tm = x_ref.shape[0]
    logits = _dot_nt(wr_ref[...], x_ref[...].astype(BF16))
    sc = jax.nn.sigmoid(logits)
    sel = sc + bias_ref[...]
    r = [sel[e:e + 1, :] for e in range(N_EXPERTS)]
    s = [sc[e:e + 1, :] for e in range(N_EXPERTS)]
    grp = []
    for g in range(N_GROUPS):
        a, b, c, d = r[4 * g:4 * g + 4]
        top2 = jnp.maximum(jnp.maximum(jnp.maximum(a + b, a + c), jnp.maximum(a + d, b + c)),
                           jnp.maximum(b + d, c + d))
        grp.append(top2)
    best = jnp.maximum(jnp.maximum(grp[0], grp[1]), jnp.maximum(grp[2], grp[3]))
    taken = jnp.zeros((1, tm), jnp.bool_)
    chosen = []
    for g in range(N_GROUPS):
        win = jnp.logical_and(grp[g] == best, jnp.logical_not(taken))
        chosen.append(win)
        taken = jnp.logical_or(taken, win)
    picked = []
    for e in range(N_EXPERTS):
        g = e // EXPERTS_PER_GROUP
        rank = jnp.zeros((1, tm), F32)
        for k in range(4 * g, 4 * g + 4):
            if k < e:
                rank = rank + (r[k] >= r[e]).astype(F32)
            elif k > e:
                rank = rank + (r[k] > r[e]).astype(F32)
        picked.append(jnp.logical_and(chosen[g], rank < 2.0))
    w = [jnp.where(picked[e], s[e], 0.0) for e in range(N_EXPERTS)]
    denom = w[0]
    for e in range(1, N_EXPERTS):
        denom = denom + w[e]
    gt_ref[...] = jnp.zeros_like(gt_ref)
    for e in range(N_EXPERTS):
        gt_ref[e:e + 1, :] = w[e] / denom
    g_ref[...] = gt_ref[...].T


def _router(x, wr_t, bias, tm):
    T, D = x.shape
    return pl.pallas_call(
        _router_kernel,
        grid=(T // tm,),
        in_specs=[pl.BlockSpec((tm, D), lambda i: (i, 0)), pl.BlockSpec(wr_t.shape, lambda i: (0, 0)),
                  pl.BlockSpec(bias.shape, lambda i: (0, 0))],
        out_specs=pl.BlockSpec((tm, LANES), lambda i: (i, 0)),
        out_shape=jax.ShapeDtypeStruct((T, LANES), F32),
        scratch_shapes=[pltpu.VMEM((LANES, tm), F32)],
        compiler_params=_cparams(("parallel",)),
        name="router",
    )(x, wr_t, bias)


def _moe_dense_kernel(x_ref, gate_ref, wg_ref, wu_ref, wd_ref, g_ref, b_ref, o_ref, xb_ref, acc_ref):
    e = pl.program_id(1)

    @pl.when(e == 0)
    def _():
        xb_ref[...] = x_ref[...].astype(BF16)
        acc_ref[...] = jnp.zeros_like(acc_ref)

    xb = xb_ref[...]
    h = jax.nn.silu(_dot(xb, wg_ref[0, 0])) * _dot(xb, wu_ref[0, 0])
    y = _dot(h.astype(BF16), wd_ref[0, 0])
    gates = gate_ref[...]
    lane = lax.broadcasted_iota(jnp.int32, gates.shape, 1)
    gcol = jnp.sum(jnp.where(lane == e, gates, 0.0), -1, keepdims=True)
    acc_ref[...] += gcol * y

    @pl.when(e == pl.num_programs(1) - 1)
    def _():
        o_ref[...] = _layer_norm(DN_ALPHA * x_ref[...] + acc_ref[...], g_ref[...], b_ref[...])


def _moe_dense(x, gates, wg, wu, wd, l, g, b, tm):
    T, D = x.shape
    _, E, _, H = wg.shape
    row = lambda i, e: (i, 0)
    const = lambda i, e: (0, 0)
    wsel = lambda i, e: (l, e, 0, 0)
    return pl.pallas_call(
        _moe_dense_kernel,
        grid=(T // tm, E),
        in_specs=[
            pl.BlockSpec((tm, D), row), pl.BlockSpec((tm, LANES), row),
            pl.BlockSpec((1, 1, D, H), wsel), pl.BlockSpec((1, 1, D, H), wsel),
            pl.BlockSpec((1, 1, H, D), wsel),
            pl.BlockSpec(g.shape, const), pl.BlockSpec(b.shape, const),
        ],
        out_specs=pl.BlockSpec((tm, D), row),
        out_shape=jax.ShapeDtypeStruct((T, D), F32),
        scratch_shapes=[pltpu.VMEM((tm, D), BF16), pltpu.VMEM((tm, D), F32)],
        compiler_params=_cparams(("parallel", "arbitrary")),
        name="moe_dense",
    )(x, gates, wg, wu, wd, g, b)


def _prep_weights(w_in_even, w_uq, w_ukv, w_out_even, w_in_odd, w_out_odd, w_router,
                  w_expert_gate, w_expert_up, w_expert_down):
    d = w_in_even.shape[1]
    n_main = 4 * RET_W + MLA_Q_RANK + MLA_KV_RANK
    w_in = w_in_even[0]
    kr_cols = jnp.pad(w_in[:, n_main:], ((0, 0), (MLA_NOPE, LANES - MLA_NOPE - MLA_ROPE)))
    w_a = jnp.concatenate([w_in[:, :n_main], kr_cols], 1).astype(BF16)
    qd = MLA_NOPE + MLA_ROPE
    wq = jnp.pad(w_uq[0].reshape(MLA_Q_RANK, MLA_HEADS, qd), ((0, 0), (0, 0), (0, LANES - qd)))
    wq = wq.reshape(MLA_Q_RANK, MLA_PAD).astype(BF16)
    wkv = w_ukv[0].reshape(MLA_KV_RANK, MLA_HEADS, MLA_NOPE + MLA_V)
    wk = jnp.pad(wkv[:, :, :MLA_NOPE], ((0, 0), (0, 0), (0, LANES - MLA_NOPE)))
    wk = wk.reshape(MLA_KV_RANK, MLA_PAD).astype(BF16)
    wvt = wkv[:, :, MLA_NOPE:].reshape(MLA_KV_RANK, MLA_HEADS * MLA_V).T.astype(BF16)
    e_np = np.zeros((MLA_ROPE, MLA_HEADS, LANES), np.float32)
    for j in range(MLA_ROPE):
        e_np[j, :, MLA_NOPE + j] = 1.0
    e_mat = jnp.asarray(e_np.reshape(MLA_ROPE, MLA_PAD)).astype(BF16)
    return dict(
        w_a=w_a, wq=wq, wk=wk, wvt=wvt, e_mat=e_mat,
        w_out_even=w_out_even[0].astype(BF16), w_in_odd=w_in_odd[0].astype(BF16),
        w_out_odd=w_out_odd[0].astype(BF16), wr_t=w_router.T.astype(BF16),
        wg=w_expert_gate.astype(BF16), wu=w_expert_up.astype(BF16), wd=w_expert_down.astype(BF16),
    )


def _moe(x, wts, l, bias, ln_g, ln_b, tm):
    gates = _router(x, wts["wr_t"], bias, tm)
    return _moe_dense(x, gates, wts["wg"], wts["wu"], wts["wd"], l, ln_g[l, 1][None], ln_b[l, 1][None], tm)


def _trunk(x3, pos0, past, wts, prm):
    B, S, D = x3.shape
    T = B * S
    x = x3.reshape(T, D)
    tm = min(T, 512)
    rep = max(tm // S, 1)
    pos = np.tile(pos0 + np.arange(S), rep)
    ln_g, ln_b = prm["ln_g"], prm["ln_b"]

    rq, rk, rv, rg, q, lat, kr = _even_in(x, wts["w_a"], wts["wq"], prm["gq"], prm["gkv"], _even_tables(pos), tm)
    causal = past is None
    if causal:
        state0 = jnp.zeros((B, RET_HEADS, RET_DK, RET_DV), F32)
        lat_all, kr_all, sk, skp = lat, kr, S, S
        tq = tk = min(S, 512)
    else:
        state0 = past["state"]
        sk = past["lat"].shape[1] + S
        skp = -(-sk // LANES) * LANES
        padk = lambda parts: jnp.concatenate(
            parts + [jnp.zeros((B, skp - sk, parts[0].shape[2]), parts[0].dtype)], 1).reshape(B * skp, -1)
        lat_all = padk([past["lat"], lat.reshape(B, S, -1)])
        kr_all = padk([past["kr"], kr.reshape(B, S, -1)])
        tq, tk = S, skp
    ret_out, ret_state = _retention(rq, rk, rv, rg, state0, B, S)
    tkv = 512 if (B * skp) % 512 == 0 else skp
    k_mla, vt_mla = _kv_up(lat_all, kr_all, wts["wk"], wts["wvt"], wts["e_mat"], skp, tkv)
    if vt_mla.shape[0] != B:
        vt_mla = vt_mla.reshape(-1, B, skp).transpose(1, 0, 2)
    mla_out = _mla_attn(q, k_mla, vt_mla, B, S, skp, sk, tq, tk, causal)
    x = _out_proj([ret_out, mla_out], wts["w_out_even"], x, ln_g[0, 0][None], ln_b[0, 0][None], tm)
    x = _moe(x, wts, 0, prm["bias"], ln_g, ln_b, tm)

    qd, kf, kb, vf, vt = _odd_in(x, wts["w_in_odd"], _odd_tables(pos), S, tm)
    if vt.shape[0] != B:
        vt = vt.reshape(-1, B, S).transpose(1, 0, 2)
    if causal:
        k_all, vt_all = kb, vt
    else:
        k_all = padk([past["dk"], kb.reshape(B, S, -1)])
        vt_all = jnp.concatenate([past["dv"].transpose(0, 2, 1), vt,
                                  jnp.zeros((B, vt.shape[1], skp - sk), BF16)], 2)
    lam_init = 0.8 - 0.6 * math.exp(-0.3 * 1)
    d_out = _diff_attn(prm["lam"], prm["gn"], qd, k_all, vt_all, B, S, skp, sk, tq, tk, causal, lam_init)
    x = _out_proj([d_out], wts["w_out_odd"], x, ln_g[1, 0][None], ln_b[1, 0][None], tm)
    x = _moe(x, wts, 1, prm["bias"], ln_g, ln_b, tm)

    return (x.reshape(B, S, D), ret_state[None], lat.reshape(1, B, S, -1), kr.reshape(1, B, S, -1),
            kf.reshape(1, B, S, 2 * DIFF_HEADS, DIFF_HD), vf.reshape(1, B, S, DIFF_HEADS, DIFF_V))


def kernel(x_prompt, x_sample, state_ret, cache_mla_latent, cache_mla_krope, cache_diff_k, cache_diff_v,
           w_in_even, w_uq, w_ukv, g_qnorm, g_kvnorm, w_out_even,
           w_in_odd, lambda_q1, lambda_k1, lambda_q2, lambda_k2, g_diff_norm, w_out_odd,
           ln_g, ln_b, w_router, router_bias, w_expert_gate, w_expert_up, w_expert_down):
    wts = _prep_weights(w_in_even, w_uq, w_ukv, w_out_even, w_in_odd, w_out_odd, w_router,
                        w_expert_gate, w_expert_up, w_expert_down)
    prm = dict(
        gq=g_qnorm[0][None].astype(F32), gkv=g_kvnorm[0][None].astype(F32),
        lam=jnp.stack([lambda_q1[0], lambda_k1[0], lambda_q2[0], lambda_k2[0]]).astype(F32),
        gn=g_diff_norm[0][None].astype(F32), bias=router_bias.reshape(N_EXPERTS, 1).astype(F32),
        ln_g=ln_g.astype(F32), ln_b=ln_b.astype(F32),
    )
    past_len = cache_mla_latent.shape[2]
    db = x_sample.shape[0]
    past = dict(
        state=state_ret[0].astype(F32), lat=cache_mla_latent[0], kr=cache_mla_krope[0],
        dk=cache_diff_k[0].reshape(db, past_len, -1).astype(BF16),
        dv=cache_diff_v[0].reshape(db, past_len, -1).astype(BF16),
    )
    outs_p = _trunk(x_prompt, 0, None, wts, prm)
    outs_s = _trunk(x_sample, past_len, past, wts, prm)
    return (outs_p[0], outs_s[0]) + outs_p[1:] + outs_s[1:]
```

```python
import functools
import math

import numpy as np
import jax
import jax.numpy as jnp
from jax import lax
from jax.experimental import pallas as pl
from jax.experimental.pallas import tpu as pltpu
from jax.experimental.pallas import tpu_sc as plsc

F32 = jnp.float32
BF16 = jnp.bfloat16

CHUNK = 64
ROPE_THETA = 10000.0
NEG_INF = -1e30
LN_EPS = 1e-5
NORM_EPS = 1e-6
DEPTH = 2
DN_ALPHA = (2.0 * DEPTH) ** 0.25
RET_HEADS = 4
RET_DK = 128
RET_DV = 128
RET_LOG_GAMMA = tuple(math.log(1.0 - 2.0 ** (-5 - h)) for h in range(RET_HEADS))
MLA_HEADS = 8
MLA_Q_RANK = 384
MLA_KV_RANK = 256
MLA_NOPE = 64
MLA_ROPE = 32
MLA_V = 64
DIFF_HEADS = 8
DIFF_HD = 64
DIFF_V = 128
N_EXPERTS = 16
N_GROUPS = 4
EXPERTS_PER_GROUP = 4
LOG2E = math.log2(math.e)

LANES = 128
RET_W = RET_HEADS * RET_DK
MLA_PAD = MLA_HEADS * LANES
VMEM_LIMIT = 56 * 1024 * 1024


def _cparams(sem):
    return pltpu.CompilerParams(dimension_semantics=sem, vmem_limit_bytes=VMEM_LIMIT)


def _rope_tables(pos, d, group, offset, scale):
    pos = np.asarray(pos, np.float64)
    half = d // 2
    inv = 1.0 / (ROPE_THETA ** (np.arange(0, d, 2, dtype=np.float64) / d))
    ang = pos[:, None] * inv[None, :]
    cos = np.full((pos.shape[0], LANES), scale, np.float64)
    s_lo = np.zeros((pos.shape[0], LANES), np.float64)
    s_hi = np.zeros((pos.shape[0], LANES), np.float64)
    start = offset
    while start + d <= LANES:
        cos[:, start:start + half] = np.cos(ang) * scale
        cos[:, start + half:start + d] = np.cos(ang) * scale
        s_lo[:, start:start + half] = -np.sin(ang) * scale
        s_hi[:, start + half:start + d] = np.sin(ang) * scale
        start += group
    return cos, s_lo, s_hi


def _even_tables(pos):
    rq = _rope_tables(pos, RET_DK, LANES, 0, 1.0)
    rk = _rope_tables(pos, RET_DK, LANES, 0, RET_DK ** -0.5)
    c = (MLA_NOPE + MLA_ROPE) ** -0.5 * LOG2E
    mq = _rope_tables(pos, MLA_ROPE, LANES, MLA_NOPE, c)
    mk = _rope_tables(pos, MLA_ROPE, LANES, MLA_NOPE, 1.0)
    tabs = [rq[0], rq[1] + rq[2], rk[0], rk[1] + rk[2], mq[0], mq[1], mq[2], mk[0], mk[1], mk[2]]
    return jnp.asarray(np.stack(tabs).astype(np.float32))


def _odd_tables(pos):
    c = DIFF_HD ** -0.5 * LOG2E
    dq = _rope_tables(pos, DIFF_HD, DIFF_HD, 0, c)
    dk = _rope_tables(pos, DIFF_HD, DIFF_HD, 0, 1.0)
    return jnp.asarray(np.stack(list(dq) + list(dk)).astype(np.float32))


def _retention_tables(L):
    lg = np.asarray(RET_LOG_GAMMA, np.float64)
    idx = np.arange(L, dtype=np.float64)
    diff = idx[:, None] - idx[None, :]
    dmask = np.where(diff[None] >= 0, np.exp(np.maximum(diff, 0.0)[None] * lg[:, None, None]), 0.0)
    qd = np.exp((idx[None, :] + 1.0) * lg[:, None])
    kd = np.exp((L - 1.0 - idx)[None, :] * lg[:, None])
    gl = np.exp(L * lg)
    qd = np.broadcast_to(qd[:, :, None], (RET_HEADS, L, LANES))
    kd = np.broadcast_to(kd[:, :, None], (RET_HEADS, L, LANES))
    gl = np.broadcast_to(gl[:, None, None], (RET_HEADS, RET_DK, RET_DV))
    f = lambda a: jnp.asarray(np.ascontiguousarray(a).astype(np.float32))
    return f(dmask), f(qd), f(kd), f(gl)


def _dot(a, b):
    return jnp.dot(a, b, preferred_element_type=F32)


def _dot_nt(a, b):
    return lax.dot_general(a, b, (((1,), (1,)), ((), ())), preferred_element_type=F32)


def _dot_tn(a, b):
    return lax.dot_general(a, b, (((0,), (0,)), ((), ())), preferred_element_type=F32)


def _rope(x, cos, s_lo, s_hi, half):
    return x * cos + pltpu.roll(x, LANES - half, 1) * s_lo + pltpu.roll(x, half, 1) * s_hi


def _layer_norm(x, g, b):
    mu = jnp.mean(x, -1, keepdims=True)
    xc = x - mu
    var = jnp.mean(xc * xc, -1, keepdims=True)
    return xc * lax.rsqrt(var + LN_EPS) * g + b


def _rms_norm(x, g):
    return x * lax.rsqrt(jnp.mean(x * x, -1, keepdims=True) + NORM_EPS) * g


def _even_in_kernel(x_ref, w_ref, wq_ref, gq_ref, gkv_ref, tab_ref,
                    rq_ref, rk_ref, rv_ref, rg_ref, q_ref, lat_ref, kr_ref):
    xb = x_ref[...].astype(BF16)
    c_rq, s_rq, c_rk, s_rk = tab_ref[0], tab_ref[1], tab_ref[2], tab_ref[3]
    hq = _dot(xb, w_ref[:, 0:RET_W])
    hk = _dot(xb, w_ref[:, RET_W:2 * RET_W])
    for h in range(RET_HEADS):
        sl = slice(h * LANES, (h + 1) * LANES)
        xq = hq[:, sl]
        rq_ref[:, sl] = (xq * c_rq + pltpu.roll(xq, RET_DK // 2, 1) * s_rq).astype(BF16)
        xk = hk[:, sl]
        rk_ref[:, sl] = (xk * c_rk + pltpu.roll(xk, RET_DK // 2, 1) * s_rk).astype(BF16)
    rv_ref[...] = _dot(xb, w_ref[:, 2 * RET_W:3 * RET_W]).astype(BF16)
    rg_ref[...] = _dot(xb, w_ref[:, 3 * RET_W:4 * RET_W]).astype(BF16)
    o = 4 * RET_W
    cq = _dot(xb, w_ref[:, o:o + MLA_Q_RANK])
    qn = _rms_norm(cq, gq_ref[...]).astype(BF16)
    qf = _dot(qn, wq_ref[...])
    c_q, lo_q, hi_q = tab_ref[4], tab_ref[5], tab_ref[6]
    for h in range(MLA_HEADS):
        sl = slice(h * LANES, (h + 1) * LANES)
        q_ref[:, sl] = _rope(qf[:, sl], c_q, lo_q, hi_q, MLA_ROPE // 2).astype(BF16)
    o += MLA_Q_RANK
    ckv = _dot(xb, w_ref[:, o:o + MLA_KV_RANK])
    lat_ref[...] = _rms_norm(ckv, gkv_ref[...])
    o += MLA_KV_RANK
    krp = _dot(xb, w_ref[:, o:o + LANES])
    krp = _rope(krp, tab_ref[7], tab_ref[8], tab_ref[9], MLA_ROPE // 2)
    kr_ref[...] = krp[:, MLA_NOPE:MLA_NOPE + MLA_ROPE]


def _even_in(x, w_a, wq, gq, gkv, tabs, tm):
    T, D = x.shape
    P = tabs.shape[1]
    nt = P // tm
    row = lambda i: (i, 0)
    const = lambda i: (0, 0)
    outs = [
        jax.ShapeDtypeStruct((T, RET_W), BF16), jax.ShapeDtypeStruct((T, RET_W), BF16),
        jax.ShapeDtypeStruct((T, RET_W), BF16), jax.ShapeDtypeStruct((T, RET_W), BF16),
        jax.ShapeDtypeStruct((T, MLA_PAD), BF16),
        jax.ShapeDtypeStruct((T, MLA_KV_RANK), F32), jax.ShapeDtypeStruct((T, MLA_ROPE), F32),
    ]
    return pl.pallas_call(
        _even_in_kernel,
        grid=(T // tm,),
        in_specs=[
            pl.BlockSpec((tm, D), row),
            pl.BlockSpec(w_a.shape, const),
            pl.BlockSpec(wq.shape, const),
            pl.BlockSpec(gq.shape, const),
            pl.BlockSpec(gkv.shape, const),
            pl.BlockSpec((tabs.shape[0], tm, LANES), lambda i: (0, i % nt, 0)),
        ],
        out_specs=[
            pl.BlockSpec((tm, RET_W), row), pl.BlockSpec((tm, RET_W), row),
            pl.BlockSpec((tm, RET_W), row), pl.BlockSpec((tm, RET_W), row),
            pl.BlockSpec((tm, MLA_PAD), row),
            pl.BlockSpec((tm, MLA_KV_RANK), row), pl.BlockSpec((tm, MLA_ROPE), row),
        ],
        out_shape=outs,
        compiler_params=_cparams(("parallel",)),
        name="even_in",
    )(x, w_a, wq, gq, gkv, tabs)


def _kv_up_kernel(lat_ref, kr_ref, wk_ref, wvt_ref, e_ref, k_ref, vt_ref):
    lb = lat_ref[...].astype(BF16)
    krb = kr_ref[...].astype(BF16)
    k_ref[...] = (_dot(lb, wk_ref[...]) + _dot(krb, e_ref[...])).astype(BF16)
    vt_ref[0] = _dot_nt(wvt_ref[...], lb).astype(BF16)


def _vt_layout(T, S, tm, width):
    nb, cols = (T // S, S) if S % tm == 0 else (1, T)
    nt = cols // tm
    spec = pl.BlockSpec((1, width, tm), lambda i: (i // nt, 0, i % nt))
    return spec, jax.ShapeDtypeStruct((nb, width, cols), BF16)


def _kv_up(lat, kr, wk, wvt, e_mat, S, tm):
    T = lat.shape[0]
    row = lambda i: (i, 0)
    const = lambda i: (0, 0)
    vt_spec, vt_shape = _vt_layout(T, S, tm, MLA_HEADS * MLA_V)
    return pl.pallas_call(
        _kv_up_kernel,
        grid=(T // tm,),
        in_specs=[
            pl.BlockSpec((tm, MLA_KV_RANK), row), pl.BlockSpec((tm, MLA_ROPE), row),
            pl.BlockSpec(wk.shape, const), pl.BlockSpec(wvt.shape, const), pl.BlockSpec(e_mat.shape, const),
        ],
        out_specs=[pl.BlockSpec((tm, MLA_PAD), row), vt_spec],
        out_shape=[jax.ShapeDtypeStruct((T, MLA_PAD), BF16), vt_shape],
        compiler_params=_cparams(("parallel",)),
        name="kv_up",
    )(lat, kr, wk, wvt, e_mat)


def _retention_kernel(q_ref, k_ref, v_ref, g_ref, s0_ref, dm_ref, qd_ref, kd_ref, gl_ref,
                      o_ref, st_ref, *, L, nchunk):
    @pl.when(pl.program_id(1) == 0)
    def _():
        st_ref[...] = s0_ref[...]

    for c in range(nchunk):
        rows = slice(c * L, (c + 1) * L)
        for h in range(RET_HEADS):
            sl = slice(h * LANES, (h + 1) * LANES)
            q = q_ref[rows, sl]
            k = k_ref[rows, sl]
            v = v_ref[rows, sl]
            st = st_ref[0, h]
            a = (_dot_nt(q, k) * dm_ref[h]).astype(BF16)
            o = _dot(a, v) + _dot(q, st.astype(BF16)) * qd_ref[h]
            kdec = (k.astype(F32) * kd_ref[h]).astype(BF16)
            st_ref[0, h] = st * gl_ref[h] + _dot_tn(kdec, v)
            mu = jnp.mean(o, -1, keepdims=True)
            oc = o - mu
            var = jnp.mean(oc * oc, -1, keepdims=True)
            on = oc * lax.rsqrt(var + LN_EPS)
            g = g_ref[rows, sl].astype(F32)
            o_ref[rows, sl] = (g * jax.nn.sigmoid(g) * on).astype(BF16)


def _retention(rq, rk, rv, rg, state0, B, S):
    L = min(S, 256)
    lt = min(S, 512)
    nj = S // lt
    dm, qd, kd, gl = _retention_tables(L)
    row = lambda b, j: (b * nj + j, 0)
    c3 = lambda b, j: (0, 0, 0)
    st_spec = pl.BlockSpec((1, RET_HEADS, RET_DK, RET_DV), lambda b, j: (b, 0, 0, 0))
    return pl.pallas_call(
        functools.partial(_retention_kernel, L=L, nchunk=lt // L),
        grid=(B, nj),
        in_specs=[pl.BlockSpec((lt, RET_W), row)] * 4 + [
            st_spec,
            pl.BlockSpec(dm.shape, c3), pl.BlockSpec(qd.shape, c3),
            pl.BlockSpec(kd.shape, c3), pl.BlockSpec(gl.shape, c3),
        ],
        out_specs=[pl.BlockSpec((lt, RET_W), row), st_spec],
        out_shape=[jax.ShapeDtypeStruct((B * S, RET_W), BF16),
                   jax.ShapeDtypeStruct((B, RET_HEADS, RET_DK, RET_DV), F32)],
        compiler_params=_cparams(("parallel", "arbitrary")),
        name="retention",
    )(rq, rk, rv, rg, state0, dm, qd, kd, gl)


def _key_tile(j, tk):
    return pl.ds(j * tk, tk) if isinstance(j, int) else pl.ds(pl.multiple_of(j * tk, tk), tk)


def _query_t(q, tq):
    q = q.astype(F32)
    if tq < LANES:
        q = jnp.concatenate([q, jnp.zeros((LANES - tq, LANES), F32)], 0)
    return q.T


def _flash_t(streams, qi, tq, tk, sk, sk_valid, causal, m_ref, l_ref, acc_ref):
    for s in range(len(streams)):
        m_ref[s] = jnp.full(m_ref.shape[1:], NEG_INF, F32)
        l_ref[s] = jnp.zeros(l_ref.shape[1:], F32)
        acc_ref[s] = jnp.zeros(acc_ref.shape[1:], F32)

    def step(j, rel=None, valid=None):
        for s, (q_t, k_at, vt_at) in enumerate(streams):
            st = _dot(k_at(j), q_t)
            if rel is not None:
                kc = (lax.broadcasted_iota(jnp.int32, st.shape, 0) + rel) // CHUNK
                qc = lax.broadcasted_iota(jnp.int32, st.shape, 1) // CHUNK
                st = jnp.where(kc <= qc, st, NEG_INF)
            if valid is not None:
                st = jnp.where(lax.broadcasted_iota(jnp.int32, st.shape, 0) < valid, st, NEG_INF)
            m_old = m_ref[s]
            m_new = jnp.maximum(m_old, jnp.max(st, 0, keepdims=True))
            p = jnp.exp2(st - m_new)
            alpha = jnp.exp2(m_old - m_new)
            l_ref[s] = alpha * l_ref[s] + jnp.sum(p, 0, keepdims=True)
            acc_ref[s] = acc_ref[s] * alpha + _dot(vt_at(j), p.astype(BF16))
            m_ref[s] = m_new

    if causal:
        n_full = qi * (tq // tk)

        def body(j, c):
            step(j)
            return c

        lax.fori_loop(0, n_full, body, 0)
        for d in range(tq // tk):
            step(n_full + d, rel=d * tk)
    else:
        n = sk // tk
        for j in range(n):
            last_valid = sk_valid - j * tk
            step(j, valid=last_valid if last_valid < tk else None)
    return [acc_ref[s] / l_ref[s] for s in range(len(streams))]


def _attn_scratch(n_streams, dv, tq):
    tqp = max(tq, LANES)
    return [pltpu.VMEM((n_streams, 1, tqp), F32), pltpu.VMEM((n_streams, 1, tqp), F32),
            pltpu.VMEM((n_streams, dv, tqp), F32)]


def _mla_attn_kernel(q_ref, k_ref, vt_ref, o_ref, m_ref, l_ref, acc_ref, *, tq, tk, sk, sk_valid, causal):
    streams = []
    for hh in range(2):
        sl = slice(hh * LANES, (hh + 1) * LANES)
        vrows = slice(hh * MLA_V, (hh + 1) * MLA_V)
        k_at = lambda j, sl=sl: k_ref[_key_tile(j, tk), sl]
        vt_at = lambda j, vrows=vrows: vt_ref[0, vrows, _key_tile(j, tk)]
        streams.append((_query_t(q_ref[:, sl], tq).astype(BF16), k_at, vt_at))
    outs = _flash_t(streams, pl.program_id(2), tq, tk, sk, sk_valid, causal, m_ref, l_ref, acc_ref)
    o_ref[...] = jnp.concatenate(outs, 0).T[:tq].astype(BF16)


def _mla_attn(q, k, vt, B, sq, sk, sk_valid, tq, tk, causal):
    nq = sq // tq
    npair = MLA_HEADS // 2
    return pl.pallas_call(
        functools.partial(_mla_attn_kernel, tq=tq, tk=tk, sk=sk, sk_valid=sk_valid, causal=causal),
        grid=(B, npair, nq),
        in_specs=[
            pl.BlockSpec((tq, 2 * LANES), lambda b, p, i: (b * nq + i, p)),
            pl.BlockSpec((sk, 2 * LANES), lambda b, p, i: (b, p)),
            pl.BlockSpec((1, 2 * MLA_V, sk), lambda b, p, i: (b, p, 0)),
        ],
        out_specs=pl.BlockSpec((tq, LANES), lambda b, p, i: (b * nq + i, p)),
        out_shape=jax.ShapeDtypeStruct((B * sq, MLA_HEADS * MLA_V), BF16),
        scratch_shapes=_attn_scratch(2, MLA_V, tq),
        compiler_params=_cparams(("parallel", "parallel", "arbitrary")),
        name="mla_attn",
    )(q, k, vt)


def _diff_attn_kernel(lam_ref, gn_ref, q_ref, k_ref, vt_ref, o_ref, m_ref, l_ref, acc_ref,
                      *, tq, tk, sk, sk_valid, causal, lam_init):
    k_at = lambda j: k_ref[_key_tile(j, tk), :]
    vt_at = lambda j: vt_ref[0, :, _key_tile(j, tk)]
    q_t = _query_t(q_ref[...], tq)
    feat = lax.broadcasted_iota(jnp.int32, q_t.shape, 0)
    zero = jnp.zeros_like(q_t)
    q1 = jnp.where(feat < DIFF_HD, q_t, zero).astype(BF16)
    q2 = jnp.where(feat < DIFF_HD, zero, q_t).astype(BF16)
    o1, o2 = _flash_t([(q1, k_at, vt_at), (q2, k_at, vt_at)], pl.program_id(2), tq, tk, sk, sk_valid,
                      causal, m_ref, l_ref, acc_ref)
    lv = lam_ref[...]
    lam = (jnp.exp(jnp.sum(lv[0:1] * lv[1:2], -1, keepdims=True))
           - jnp.exp(jnp.sum(lv[2:3] * lv[3:4], -1, keepdims=True)) + lam_init)
    o = (o1 - lam * o2).T[:tq]
    o_ref[...] = (_rms_norm(o, gn_ref[...]) * (1.0 - lam_init)).astype(BF16)


def _diff_attn(lam_vecs, gn, q, k, vt, B, sq, sk, sk_valid, tq, tk, causal, lam_init):
    nq = sq // tq
    const = lambda b, h, i: (0, 0)
    return pl.pallas_call(
        functools.partial(_diff_attn_kernel, tq=tq, tk=tk, sk=sk, sk_valid=sk_valid, causal=causal,
                          lam_init=lam_init),
        grid=(B, DIFF_HEADS, nq),
        in_specs=[
            pl.BlockSpec(lam_vecs.shape, const), pl.BlockSpec(gn.shape, const),
            pl.BlockSpec((tq, LANES), lambda b, h, i: (b * nq + i, h)),
            pl.BlockSpec((sk, LANES), lambda b, h, i: (b, h)),
            pl.BlockSpec((1, DIFF_V, sk), lambda b, h, i: (b, h, 0)),
        ],
        out_specs=pl.BlockSpec((tq, LANES), lambda b, h, i: (b * nq + i, h)),
        out_shape=jax.ShapeDtypeStruct((B * sq, DIFF_HEADS * DIFF_V), BF16),
        scratch_shapes=_attn_scratch(2, DIFF_V, tq),
        compiler_params=_cparams(("parallel", "parallel", "arbitrary")),
        name="diff_attn",
    )(lam_vecs, gn, q, k, vt)


def _out_proj_kernel(*refs, n_in):
    a_refs = refs[:n_in]
    w_ref, x_ref, g_ref, b_ref, o_ref = refs[n_in:]
    y = None
    off = 0
    for a_ref in a_refs:
        width = a_ref.shape[1]
        part = _dot(a_ref[...], w_ref[off:off + width, :])
        y = part if y is None else y + part
        off += width
    o_ref[...] = _layer_norm(DN_ALPHA * x_ref[...] + y, g_ref[...], b_ref[...])


def _out_proj(acts, w, x, g, b, tm):
    T, D = x.shape
    row = lambda i: (i, 0)
    const = lambda i: (0, 0)
    return pl.pallas_call(
        functools.partial(_out_proj_kernel, n_in=len(acts)),
        grid=(T // tm,),
        in_specs=[pl.BlockSpec((tm, a.shape[1]), row) for a in acts] + [
            pl.BlockSpec(w.shape, const), pl.BlockSpec((tm, D), row),
            pl.BlockSpec(g.shape, const), pl.BlockSpec(b.shape, const),
        ],
        out_specs=pl.BlockSpec((tm, D), row),
        out_shape=jax.ShapeDtypeStruct((T, D), F32),
        compiler_params=_cparams(("parallel",)),
        name="out_proj",
    )(*acts, w, x, g, b)


def _odd_in_kernel(x_ref, w_ref, tab_ref, q_ref, kf_ref, kb_ref, vf_ref, vt_ref):
    xb = x_ref[...].astype(BF16)
    W = DIFF_HEADS * 2 * DIFF_HD
    hq = _dot(xb, w_ref[:, 0:W])
    hk = _dot(xb, w_ref[:, W:2 * W])
    for h in range(W // LANES):
        sl = slice(h * LANES, (h + 1) * LANES)
        q_ref[:, sl] = _rope(hq[:, sl], tab_ref[0], tab_ref[1], tab_ref[2], DIFF_HD // 2).astype(BF16)
        kk = _rope(hk[:, sl], tab_ref[3], tab_ref[4], tab_ref[5], DIFF_HD // 2)
        kf_ref[:, sl] = kk
        kb_ref[:, sl] = kk.astype(BF16)
    hv = _dot(xb, w_ref[:, 2 * W:])
    vf_ref[...] = hv
    vt_ref[0] = hv.T.astype(BF16)


def _odd_in(x, w, tabs, S, tm):
    T, D = x.shape
    W = DIFF_HEADS * 2 * DIFF_HD
    nt = tabs.shape[1] // tm
    row = lambda i: (i, 0)
    blk = pl.BlockSpec((tm, W), row)
    vt_spec, vt_shape = _vt_layout(T, S, tm, DIFF_HEADS * DIFF_V)
    return pl.pallas_call(
        _odd_in_kernel,
        grid=(T // tm,),
        in_specs=[pl.BlockSpec((tm, D), row), pl.BlockSpec(w.shape, lambda i: (0, 0)),
                  pl.BlockSpec((tabs.shape[0], tm, LANES), lambda i: (0, i % nt, 0))],
        out_specs=[blk] * 4 + [vt_spec],
        out_shape=[jax.ShapeDtypeStruct((T, W), BF16), jax.ShapeDtypeStruct((T, W), F32),
                   jax.ShapeDtypeStruct((T, W), BF16), jax.ShapeDtypeStruct((T, W), F32), vt_shape],
        compiler_params=_cparams(("parallel",)),
        name="odd_in",
    )(x, w, tabs)


def _router_kernel(x_ref, wr_ref, bias_ref, tri_ref, g_ref, route_ref, cnt_out_ref, gt_ref, oh_ref, cnt_ref):
    tm = x_ref.shape[0]
    logits = _dot_nt(wr_ref[...], x_ref[...].astype(BF16))
    sc = jax.nn.sigmoid(logits)
    sel = sc + bias_ref[...]
    r = [sel[e:e + 1, :] for e in range(N_EXPERTS)]
    s = [sc[e:e + 1, :] for e in range(N_EXPERTS)]
    grp = []
    for g in range(N_GROUPS):
        a, b, c, d = r[4 * g:4 * g + 4]
        top2 = jnp.maximum(jnp.maximum(jnp.maximum(a + b, a + c), jnp.maximum(a + d, b + c)),
                           jnp.maximum(b + d, c + d))
        grp.append(top2)
    best = jnp.maximum(jnp.maximum(grp[0], grp[1]), jnp.maximum(grp[2], grp[3]))
    taken = jnp.zeros((1, tm), jnp.bool_)
    chosen = []
    for g in range(N_GROUPS):
        win = jnp.logical_and(grp[g] == best, jnp.logical_not(taken))
        chosen.append(win)
        taken = jnp.logical_or(taken, win)
    picked = []
    for e in range(N_EXPERTS):
        g = e // EXPERTS_PER_GROUP
        rank = jnp.zeros((1, tm), F32)
        for k in range(4 * g, 4 * g + 4):
            if k < e:
                rank = rank + (r[k] >= r[e]).astype(F32)
            elif k > e:
                rank = rank + (r[k] > r[e]).astype(F32)
        picked.append(jnp.logical_and(chosen[g], rank < 2.0))
    w = [jnp.where(picked[e], s[e], 0.0) for e in range(N_EXPERTS)]
    denom = w[0]
    for e in range(1, N_EXPERTS):
        denom = denom + w[e]
    gt_ref[...] = jnp.zeros_like(gt_ref)
    for e in range(N_EXPERTS):
        gt_ref[e:e + 1, :] = w[e] / denom
    g_ref[...] = gt_ref[...].T

    @pl.when(pl.program_id(0) == 0)
    def _():
        cnt_ref[...] = jnp.zeros_like(cnt_ref)

    oh_ref[...] = jnp.zeros_like(oh_ref)
    for g in range(N_GROUPS):
        oh_ref[g:g + 1, :] = chosen[g].astype(F32)
    oh = oh_ref[...]
    before = _dot(oh.astype(BF16), tri_ref[...])
    base = cnt_ref[:, 0:1]
    rank = jnp.sum(oh * (base + before), 0, keepdims=True)
    gid = jnp.sum(oh * lax.broadcasted_iota(jnp.int32, oh.shape, 0).astype(F32), 0, keepdims=True)
    row = lax.broadcasted_iota(jnp.int32, oh.shape, 0)
    route_ref[...] = jnp.where(row == 0, gid, jnp.where(row == 1, rank, 0.0)).astype(jnp.int32)
    cnt_ref[...] = cnt_ref[...] + jnp.sum(oh, 1, keepdims=True)
    cnt_out_ref[...] = cnt_ref[...]


def _router(x, wr_t, bias, tm):
    T, D = x.shape
    tri = jnp.asarray(np.triu(np.ones((tm, tm), np.float32), 1)).astype(BF16)
    const = lambda i: (0, 0)
    return pl.pallas_call(
        _router_kernel,
        grid=(T // tm,),
        in_specs=[pl.BlockSpec((tm, D), lambda i: (i, 0)), pl.BlockSpec(wr_t.shape, const),
                  pl.BlockSpec(bias.shape, const), pl.BlockSpec(tri.shape, const)],
        out_specs=[pl.BlockSpec((tm, LANES), lambda i: (i, 0)), pl.BlockSpec((8, tm), lambda i: (0, i)),
                   pl.BlockSpec((8, LANES), const)],
        out_shape=[jax.ShapeDtypeStruct((T, LANES), F32), jax.ShapeDtypeStruct((8, T), jnp.int32),
                   jax.ShapeDtypeStruct((8, LANES), F32)],
        scratch_shapes=[pltpu.VMEM((LANES, tm), F32), pltpu.VMEM((8, tm), F32), pltpu.VMEM((8, LANES), F32)],
        compiler_params=_cparams(("arbitrary",)),
        name="router",
    )(x, wr_t, bias, tri)


def _moe_dense_kernel(x_ref, gate_ref, wg_ref, wu_ref, wd_ref, g_ref, b_ref, o_ref, xb_ref, acc_ref):
    e = pl.program_id(1)

    @pl.when(e == 0)
    def _():
        xb_ref[...] = x_ref[...].astype(BF16)
        acc_ref[...] = jnp.zeros_like(acc_ref)

    xb = xb_ref[...]
    h = jax.nn.silu(_dot(xb, wg_ref[0, 0])) * _dot(xb, wu_ref[0, 0])
    y = _dot(h.astype(BF16), wd_ref[0, 0])
    gates = gate_ref[...]
    lane = lax.broadcasted_iota(jnp.int32, gates.shape, 1)
    gcol = jnp.sum(jnp.where(lane == e, gates, 0.0), -1, keepdims=True)
    acc_ref[...] += gcol * y

    @pl.when(e == pl.num_programs(1) - 1)
    def _():
        o_ref[...] = _layer_norm(DN_ALPHA * x_ref[...] + acc_ref[...], g_ref[...], b_ref[...])


def _moe_dense(x, gates, wg, wu, wd, l, g, b, tm):
    T, D = x.shape
    _, E, _, H = wg.shape
    row = lambda i, e: (i, 0)
    const = lambda i, e: (0, 0)
    wsel = lambda i, e: (l, e, 0, 0)
    return pl.pallas_call(
        _moe_dense_kernel,
        grid=(T // tm, E),
        in_specs=[
            pl.BlockSpec((tm, D), row), pl.BlockSpec((tm, LANES), row),
            pl.BlockSpec((1, 1, D, H), wsel), pl.BlockSpec((1, 1, D, H), wsel),
            pl.BlockSpec((1, 1, H, D), wsel),
            pl.BlockSpec(g.shape, const), pl.BlockSpec(b.shape, const),
        ],
        out_specs=pl.BlockSpec((tm, D), row),
        out_shape=jax.ShapeDtypeStruct((T, D), F32),
        scratch_shapes=[pltpu.VMEM((tm, D), BF16), pltpu.VMEM((tm, D), F32)],
        compiler_params=_cparams(("parallel", "arbitrary")),
        name="moe_dense",
    )(x, gates, wg, wu, wd, g, b)


SC_CORES = 2
SC_SUBCORES = 16
SC_WORKERS = SC_CORES * SC_SUBCORES
SC_TILE_BYTES = 384 * 1024
SPARSE_ROW_TILE = 512
SPARSE_MIN_TOKENS = 4096


def _sc_ring(per_w, row_bytes):
    for ch, nbuf in ((16, 4), (16, 2), (8, 2)):
        if per_w % (ch * nbuf) == 0 and ch * nbuf * row_bytes <= SC_TILE_BYTES:
            return ch, nbuf
    raise ValueError(f"no SparseCore gather ring for {per_w} rows of {row_bytes} bytes per subcore")


def _sc_gather(table, idx):
    R = idx.shape[0]
    D = table.shape[1]
    per_w = R // SC_WORKERS
    assert per_w * SC_WORKERS == R and per_w % 8 == 0
    ch, nbuf = _sc_ring(per_w, D * table.dtype.itemsize)
    nchunk = per_w // ch
    mesh = plsc.VectorSubcoreMesh(core_axis_name="c", subcore_axis_name="s")

    @functools.partial(
        pl.kernel, mesh=mesh,
        out_type=jax.ShapeDtypeStruct((R, D), table.dtype),
        scratch_types=[pltpu.VMEM((per_w,), jnp.int32), pltpu.VMEM((nbuf, ch, D), table.dtype),
                       pltpu.SemaphoreType.DMA((nbuf,)), pltpu.SemaphoreType.DMA((nbuf,))],
    )
    def gather_kernel(table_hbm, idx_hbm, out_hbm, idx_v, rows_v, gsem, wsem):
        base = (lax.axis_index("s") * SC_CORES + lax.axis_index("c")) * per_w
        pltpu.sync_copy(idx_hbm.at[pl.ds(base, per_w)], idx_v)

        def gather(c, b):
            return pltpu.make_async_copy(table_hbm.at[idx_v.at[pl.ds(c * ch, ch)]], rows_v.at[b], gsem.at[b])

        def write(c, b):
            return pltpu.make_async_copy(rows_v.at[b], out_hbm.at[pl.ds(base + c * ch, ch)], wsem.at[b])

        for b in range(nbuf - 1):
            gather(b, b).start()

        @pl.loop(0, nchunk, step=nbuf)
        def _(c):
            for b in range(nbuf):
                cc = c + b
                gather(cc, b).wait()
                write(cc, b).start()
                pb = (b - 1) % nbuf

                @pl.when(cc + nbuf - 1 < nchunk)
                def _():
                    @pl.when(cc >= 1)
                    def _():
                        write(cc - 1, pb).wait()

                    gather(cc + nbuf - 1, pb).start()

        for b in range(nbuf):
            write(nchunk - nbuf + b, b).wait()

    return gather_kernel(table, idx)


def _moe_group_kernel(tg_ref, nv_ref, x_ref, gate_ref, wg_ref, wu_ref, wd_ref, g_ref, b_ref, o_ref,
                      xb_ref, acc_ref):
    j = pl.program_id(0)
    e = pl.program_id(1)
    live = j < nv_ref[0]

    @pl.when(jnp.logical_and(live, e == 0))
    def _():
        xb_ref[...] = x_ref[...].astype(BF16)
        acc_ref[...] = jnp.zeros_like(acc_ref)

    @pl.when(live)
    def _():
        xb = xb_ref[...]
        h = jax.nn.silu(_dot(xb, wg_ref[0, 0])) * _dot(xb, wu_ref[0, 0])
        y = _dot(h.astype(BF16), wd_ref[0, 0])
        gates = gate_ref[...]
        lane = lax.broadcasted_iota(jnp.int32, gates.shape, 1)
        expert = tg_ref[j] * EXPERTS_PER_GROUP + e
        gcol = jnp.sum(jnp.where(lane == expert, gates, 0.0), -1, keepdims=True)
        acc_ref[...] += gcol * y

    last = e == pl.num_programs(1) - 1

    @pl.when(jnp.logical_and(live, last))
    def _():
        o_ref[...] = _layer_norm(DN_ALPHA * x_ref[...] + acc_ref[...], g_ref[...], b_ref[...])

    @pl.when(jnp.logical_and(jnp.logical_not(live), last))
    def _():
        o_ref[...] = jnp.zeros_like(o_ref)


def _moe_group(tile_group, n_valid, xs, gs, wg, wu, wd, l, g, b, tm):
    R, D = xs.shape
    H = wg.shape[3]
    row = lambda j, e, tg, nv: (j, 0)
    const = lambda j, e, tg, nv: (0, 0)
    wsel = lambda j, e, tg, nv: (l, tg[j] * EXPERTS_PER_GROUP + e, 0, 0)
    grid_spec = pltpu.PrefetchScalarGridSpec(
        num_scalar_prefetch=2,
        grid=(R // tm, EXPERTS_PER_GROUP),
        in_specs=[
            pl.BlockSpec((tm, D), row), pl.BlockSpec((tm, LANES), row),
            pl.BlockSpec((1, 1, D, H), wsel), pl.BlockSpec((1, 1, D, H), wsel),
            pl.BlockSpec((1, 1, H, D), wsel),
            pl.BlockSpec(g.shape, const), pl.BlockSpec(b.shape, const),
        ],
        out_specs=pl.BlockSpec((tm, D), row),
        scratch_shapes=[pltpu.VMEM((tm, D), BF16), pltpu.VMEM((tm, D), F32)],
    )
    return pl.pallas_call(
        _moe_group_kernel,
        grid_spec=grid_spec,
        out_shape=jax.ShapeDtypeStruct((R, D), F32),
        compiler_params=_cparams(("parallel", "arbitrary")),
        name="moe_group",
    )(tile_group, n_valid, xs, gs, wg, wu, wd, g, b)


def _moe_sparse(x, gates, route, counts, wg, wu, wd, l, g, b):
    T = x.shape[0]
    tm = SPARSE_ROW_TILE
    n_tiles = T // tm + N_GROUPS
    gid, rank = route[0], route[1]
    cnt = counts[:N_GROUPS, 0].astype(jnp.int32)
    tiles = (cnt + tm - 1) // tm
    ends = jnp.cumsum(tiles)
    starts = (ends - tiles) * tm
    dest = rank + sum(jnp.where(gid == k, starts[k], 0) for k in range(N_GROUPS))
    src = jnp.zeros((n_tiles * tm,), jnp.int32).at[dest].set(jnp.arange(T, dtype=jnp.int32))
    tile_group = jnp.minimum(jnp.searchsorted(ends, jnp.arange(n_tiles, dtype=jnp.int32), side="right"),
                             N_GROUPS - 1).astype(jnp.int32)
    xs = _sc_gather(x, src)
    gs = _sc_gather(gates, src)
    ys = _moe_group(tile_group, ends[N_GROUPS - 1:], xs, gs, wg, wu, wd, l, g, b, tm)
    return _sc_gather(ys, dest)


def _prep_weights(w_in_even, w_uq, w_ukv, w_out_even, w_in_odd, w_out_odd, w_router,
                  w_expert_gate, w_expert_up, w_expert_down):
    d = w_in_even.shape[1]
    n_main = 4 * RET_W + MLA_Q_RANK + MLA_KV_RANK
    w_in = w_in_even[0]
    kr_cols = jnp.pad(w_in[:, n_main:], ((0, 0), (MLA_NOPE, LANES - MLA_NOPE - MLA_ROPE)))
    w_a = jnp.concatenate([w_in[:, :n_main], kr_cols], 1).astype(BF16)
    qd = MLA_NOPE + MLA_ROPE
    wq = jnp.pad(w_uq[0].reshape(MLA_Q_RANK, MLA_HEADS, qd), ((0, 0), (0, 0), (0, LANES - qd)))
    wq = wq.reshape(MLA_Q_RANK, MLA_PAD).astype(BF16)
    wkv = w_ukv[0].reshape(MLA_KV_RANK, MLA_HEADS, MLA_NOPE + MLA_V)
    wk = jnp.pad(wkv[:, :, :MLA_NOPE], ((0, 0), (0, 0), (0, LANES - MLA_NOPE)))
    wk = wk.reshape(MLA_KV_RANK, MLA_PAD).astype(BF16)
    wvt = wkv[:, :, MLA_NOPE:].reshape(MLA_KV_RANK, MLA_HEADS * MLA_V).T.astype(BF16)
    e_np = np.zeros((MLA_ROPE, MLA_HEADS, LANES), np.float32)
    for j in range(MLA_ROPE):
        e_np[j, :, MLA_NOPE + j] = 1.0
    e_mat = jnp.asarray(e_np.reshape(MLA_ROPE, MLA_PAD)).astype(BF16)
    return dict(
        w_a=w_a, wq=wq, wk=wk, wvt=wvt, e_mat=e_mat,
        w_out_even=w_out_even[0].astype(BF16), w_in_odd=w_in_odd[0].astype(BF16),
        w_out_odd=w_out_odd[0].astype(BF16), wr_t=w_router.T.astype(BF16),
        wg=w_expert_gate.astype(BF16), wu=w_expert_up.astype(BF16), wd=w_expert_down.astype(BF16),
    )


def _moe(x, wts, l, bias, ln_g, ln_b, tm):
    gates, route, counts = _router(x, wts["wr_t"], bias, tm)
    g, b = ln_g[l, 1][None], ln_b[l, 1][None]
    if x.shape[0] >= SPARSE_MIN_TOKENS:
        return _moe_sparse(x, gates, route, counts, wts["wg"], wts["wu"], wts["wd"], l, g, b)
    return _moe_dense(x, gates, wts["wg"], wts["wu"], wts["wd"], l, g, b, tm)


def _trunk(x3, pos0, past, wts, prm):
    B, S, D = x3.shape
    T = B * S
    x = x3.reshape(T, D)
    tm = min(T, 512)
    rep = max(tm // S, 1)
    pos = np.tile(pos0 + np.arange(S), rep)
    ln_g, ln_b = prm["ln_g"], prm["ln_b"]

    rq, rk, rv, rg, q, lat, kr = _even_in(x, wts["w_a"], wts["wq"], prm["gq"], prm["gkv"], _even_tables(pos), tm)
    causal = past is None
    if causal:
        state0 = jnp.zeros((B, RET_HEADS, RET_DK, RET_DV), F32)
        lat_all, kr_all, sk, skp = lat, kr, S, S
        tq = tk = min(S, 512)
    else:
        state0 = past["state"]
        sk = past["lat"].shape[1] + S
        skp = -(-sk // LANES) * LANES
        padk = lambda parts: jnp.concatenate(
            parts + [jnp.zeros((B, skp - sk, parts[0].shape[2]), parts[0].dtype)], 1).reshape(B * skp, -1)
        lat_all = padk([past["lat"], lat.reshape(B, S, -1)])
        kr_all = padk([past["kr"], kr.reshape(B, S, -1)])
        tq, tk = S, skp
    ret_out, ret_state = _retention(rq, rk, rv, rg, state0, B, S)
    tkv = 512 if (B * skp) % 512 == 0 else skp
    k_mla, vt_mla = _kv_up(lat_all, kr_all, wts["wk"], wts["wvt"], wts["e_mat"], skp, tkv)
    if vt_mla.shape[0] != B:
        vt_mla = vt_mla.reshape(-1, B, skp).transpose(1, 0, 2)
    mla_out = _mla_attn(q, k_mla, vt_mla, B, S, skp, sk, tq, tk, causal)
    x = _out_proj([ret_out, mla_out], wts["w_out_even"], x, ln_g[0, 0][None], ln_b[0, 0][None], tm)
    x = _moe(x, wts, 0, prm["bias"], ln_g, ln_b, tm)

    qd, kf, kb, vf, vt = _odd_in(x, wts["w_in_odd"], _odd_tables(pos), S, tm)
    if vt.shape[0] != B:
        vt = vt.reshape(-1, B, S).transpose(1, 0, 2)
    if causal:
        k_all, vt_all = kb, vt
    else:
        k_all = padk([past["dk"], kb.reshape(B, S, -1)])
        vt_all = jnp.concatenate([past["dv"].transpose(0, 2, 1), vt,
                                  jnp.zeros((B, vt.shape[1], skp - sk), BF16)], 2)
    lam_init = 0.8 - 0.6 * math.exp(-0.3 * 1)
    d_out = _diff_attn(prm["lam"], prm["gn"], qd, k_all, vt_all, B, S, skp, sk, tq, tk, causal, lam_init)
    x = _out_proj([d_out], wts["w_out_odd"], x, ln_g[1, 0][None], ln_b[1, 0][None], tm)
    x = _moe(x, wts, 1, prm["bias"], ln_g, ln_b, tm)

    return (x.reshape(B, S, D), ret_state[None], lat.reshape(1, B, S, -1), kr.reshape(1, B, S, -1),
            kf.reshape(1, B, S, 2 * DIFF_HEADS, DIFF_HD), vf.reshape(1, B, S, DIFF_HEADS, DIFF_V))


def kernel(x_prompt, x_sample, state_ret, cache_mla_latent, cache_mla_krope, cache_diff_k, cache_diff_v,
           w_in_even, w_uq, w_ukv, g_qnorm, g_kvnorm, w_out_even,
           w_in_odd, lambda_q1, lambda_k1, lambda_q2, lambda_k2, g_diff_norm, w_out_odd,
           ln_g, ln_b, w_router, router_bias, w_expert_gate, w_expert_up, w_expert_down):
    wts = _prep_weights(w_in_even, w_uq, w_ukv, w_out_even, w_in_odd, w_out_odd, w_router,
                        w_expert_gate, w_expert_up, w_expert_down)
    prm = dict(
        gq=g_qnorm[0][None].astype(F32), gkv=g_kvnorm[0][None].astype(F32),
        lam=jnp.stack([lambda_q1[0], lambda_k1[0], lambda_q2[0], lambda_k2[0]]).astype(F32),
        gn=g_diff_norm[0][None].astype(F32), bias=router_bias.reshape(N_EXPERTS, 1).astype(F32),
        ln_g=ln_g.astype(F32), ln_b=ln_b.astype(F32),
    )
    past_len = cache_mla_latent.shape[2]
    db = x_sample.shape[0]
    past = dict(
        state=state_ret[0].astype(F32), lat=cache_mla_latent[0], kr=cache_mla_krope[0],
        dk=cache_diff_k[0].reshape(db, past_len, -1).astype(BF16),
        dv=cache_diff_v[0].reshape(db, past_len, -1).astype(BF16),
    )
    outs_p = _trunk(x_prompt, 0, None, wts, prm)
    outs_s = _trunk(x_sample, past_len, past, wts, prm)
    return (outs_p[0], outs_s[0]) + outs_p[1:] + outs_s[1:]
```

```python
import functools
import math

import numpy as np
import jax
import jax.numpy as jnp
from jax import lax
from jax.experimental import pallas as pl
from jax.experimental.pallas import tpu as pltpu
from jax.experimental.pallas import tpu_sc as plsc

F32 = jnp.float32
BF16 = jnp.bfloat16

CHUNK = 64
ROPE_THETA = 10000.0
NEG_INF = -1e30
LN_EPS = 1e-5
NORM_EPS = 1e-6
DEPTH = 2
DN_ALPHA = (2.0 * DEPTH) ** 0.25
RET_HEADS = 4
RET_DK = 128
RET_DV = 128
RET_LOG_GAMMA = tuple(math.log(1.0 - 2.0 ** (-5 - h)) for h in range(RET_HEADS))
MLA_HEADS = 8
MLA_Q_RANK = 384
MLA_KV_RANK = 256
MLA_NOPE = 64
MLA_ROPE = 32
MLA_V = 64
DIFF_HEADS = 8
DIFF_HD = 64
DIFF_V = 128
N_EXPERTS = 16
N_GROUPS = 4
EXPERTS_PER_GROUP = 4
LOG2E = math.log2(math.e)

LANES = 128
RET_W = RET_HEADS * RET_DK
MLA_PAD = MLA_HEADS * LANES
VMEM_LIMIT = 56 * 1024 * 1024


def _cparams(sem):
    return pltpu.CompilerParams(dimension_semantics=sem, vmem_limit_bytes=VMEM_LIMIT)


def _rope_tables(pos, d, group, offset, scale):
    pos = np.asarray(pos, np.float64)
    half = d // 2
    inv = 1.0 / (ROPE_THETA ** (np.arange(0, d, 2, dtype=np.float64) / d))
    ang = pos[:, None] * inv[None, :]
    cos = np.full((pos.shape[0], LANES), scale, np.float64)
    s_lo = np.zeros((pos.shape[0], LANES), np.float64)
    s_hi = np.zeros((pos.shape[0], LANES), np.float64)
    start = offset
    while start + d <= LANES:
        cos[:, start:start + half] = np.cos(ang) * scale
        cos[:, start + half:start + d] = np.cos(ang) * scale
        s_lo[:, start:start + half] = -np.sin(ang) * scale
        s_hi[:, start + half:start + d] = np.sin(ang) * scale
        start += group
    return cos, s_lo, s_hi


def _even_tables(pos):
    rq = _rope_tables(pos, RET_DK, LANES, 0, 1.0)
    rk = _rope_tables(pos, RET_DK, LANES, 0, RET_DK ** -0.5)
    c = (MLA_NOPE + MLA_ROPE) ** -0.5 * LOG2E
    mq = _rope_tables(pos, MLA_ROPE, LANES, MLA_NOPE, c)
    mk = _rope_tables(pos, MLA_ROPE, LANES, MLA_NOPE, 1.0)
    tabs = [rq[0], rq[1] + rq[2], rk[0], rk[1] + rk[2], mq[0], mq[1], mq[2], mk[0], mk[1], mk[2]]
    return jnp.asarray(np.stack(tabs).astype(np.float32))


def _odd_tables(pos):
    c = DIFF_HD ** -0.5 * LOG2E
    dq = _rope_tables(pos, DIFF_HD, DIFF_HD, 0, c)
    dk = _rope_tables(pos, DIFF_HD, DIFF_HD, 0, 1.0)
    return jnp.asarray(np.stack(list(dq) + list(dk)).astype(np.float32))


def _retention_tables(L):
    lg = np.asarray(RET_LOG_GAMMA, np.float64)
    idx = np.arange(L, dtype=np.float64)
    diff = idx[:, None] - idx[None, :]
    dmask = np.where(diff[None] >= 0, np.exp(np.maximum(diff, 0.0)[None] * lg[:, None, None]), 0.0)
    qd = np.exp((idx[None, :] + 1.0) * lg[:, None])
    kd = np.exp((L - 1.0 - idx)[None, :] * lg[:, None])
    gl = np.exp(L * lg)
    qd = np.broadcast_to(qd[:, :, None], (RET_HEADS, L, LANES))
    kd = np.broadcast_to(kd[:, :, None], (RET_HEADS, L, LANES))
    gl = np.broadcast_to(gl[:, None, None], (RET_HEADS, RET_DK, RET_DV))
    f = lambda a: jnp.asarray(np.ascontiguousarray(a).astype(np.float32))
    return f(dmask), f(qd), f(kd), f(gl)


def _dot(a, b):
    return jnp.dot(a, b, preferred_element_type=F32)


def _dot_nt(a, b):
    return lax.dot_general(a, b, (((1,), (1,)), ((), ())), preferred_element_type=F32)


def _dot_tn(a, b):
    return lax.dot_general(a, b, (((0,), (0,)), ((), ())), preferred_element_type=F32)


def _rope(x, cos, s_lo, s_hi, half):
    return x * cos + pltpu.roll(x, LANES - half, 1) * s_lo + pltpu.roll(x, half, 1) * s_hi


def _layer_norm(x, g, b):
    mu = jnp.mean(x, -1, keepdims=True)
    xc = x - mu
    var = jnp.mean(xc * xc, -1, keepdims=True)
    return xc * lax.rsqrt(var + LN_EPS) * g + b


def _rms_norm(x, g):
    return x * lax.rsqrt(jnp.mean(x * x, -1, keepdims=True) + NORM_EPS) * g


def _even_in_kernel(x_ref, w_ref, wq_ref, gq_ref, gkv_ref, tab_ref,
                    rq_ref, rk_ref, rv_ref, rg_ref, q_ref, lat_ref, kr_ref):
    xb = x_ref[...].astype(BF16)
    c_rq, s_rq, c_rk, s_rk = tab_ref[0], tab_ref[1], tab_ref[2], tab_ref[3]
    hq = _dot(xb, w_ref[:, 0:RET_W])
    hk = _dot(xb, w_ref[:, RET_W:2 * RET_W])
    for h in range(RET_HEADS):
        sl = slice(h * LANES, (h + 1) * LANES)
        xq = hq[:, sl]
        rq_ref[:, sl] = (xq * c_rq + pltpu.roll(xq, RET_DK // 2, 1) * s_rq).astype(BF16)
        xk = hk[:, sl]
        rk_ref[:, sl] = (xk * c_rk + pltpu.roll(xk, RET_DK // 2, 1) * s_rk).astype(BF16)
    rv_ref[...] = _dot(xb, w_ref[:, 2 * RET_W:3 * RET_W]).astype(BF16)
    rg_ref[...] = _dot(xb, w_ref[:, 3 * RET_W:4 * RET_W]).astype(BF16)
    o = 4 * RET_W
    cq = _dot(xb, w_ref[:, o:o + MLA_Q_RANK])
    qn = _rms_norm(cq, gq_ref[...]).astype(BF16)
    qf = _dot(qn, wq_ref[...])
    c_q, lo_q, hi_q = tab_ref[4], tab_ref[5], tab_ref[6]
    for h in range(MLA_HEADS):
        sl = slice(h * LANES, (h + 1) * LANES)
        q_ref[:, sl] = _rope(qf[:, sl], c_q, lo_q, hi_q, MLA_ROPE // 2).astype(BF16)
    o += MLA_Q_RANK
    ckv = _dot(xb, w_ref[:, o:o + MLA_KV_RANK])
    lat_ref[...] = _rms_norm(ckv, gkv_ref[...])
    o += MLA_KV_RANK
    krp = _dot(xb, w_ref[:, o:o + LANES])
    krp = _rope(krp, tab_ref[7], tab_ref[8], tab_ref[9], MLA_ROPE // 2)
    kr_ref[...] = krp[:, MLA_NOPE:MLA_NOPE + MLA_ROPE]


def _even_in(x, w_a, wq, gq, gkv, tabs, tm):
    T, D = x.shape
    P = tabs.shape[1]
    nt = P // tm
    row = lambda i: (i, 0)
    const = lambda i: (0, 0)
    outs = [
        jax.ShapeDtypeStruct((T, RET_W), BF16), jax.ShapeDtypeStruct((T, RET_W), BF16),
        jax.ShapeDtypeStruct((T, RET_W), BF16), jax.ShapeDtypeStruct((T, RET_W), BF16),
        jax.ShapeDtypeStruct((T, MLA_PAD), BF16),
        jax.ShapeDtypeStruct((T, MLA_KV_RANK), F32), jax.ShapeDtypeStruct((T, MLA_ROPE), F32),
    ]
    return pl.pallas_call(
        _even_in_kernel,
        grid=(T // tm,),
        in_specs=[
            pl.BlockSpec((tm, D), row),
            pl.BlockSpec(w_a.shape, const),
            pl.BlockSpec(wq.shape, const),
            pl.BlockSpec(gq.shape, const),
            pl.BlockSpec(gkv.shape, const),
            pl.BlockSpec((tabs.shape[0], tm, LANES), lambda i: (0, i % nt, 0)),
        ],
        out_specs=[
            pl.BlockSpec((tm, RET_W), row), pl.BlockSpec((tm, RET_W), row),
            pl.BlockSpec((tm, RET_W), row), pl.BlockSpec((tm, RET_W), row),
            pl.BlockSpec((tm, MLA_PAD), row),
            pl.BlockSpec((tm, MLA_KV_RANK), row), pl.BlockSpec((tm, MLA_ROPE), row),
        ],
        out_shape=outs,
        compiler_params=_cparams(("parallel",)),
        name="even_in",
    )(x, w_a, wq, gq, gkv, tabs)


def _kv_up_kernel(lat_ref, kr_ref, wk_ref, wvt_ref, e_ref, k_ref, vt_ref):
    lb = lat_ref[...].astype(BF16)
    krb = kr_ref[...].astype(BF16)
    k_ref[...] = (_dot(lb, wk_ref[...]) + _dot(krb, e_ref[...])).astype(BF16)
    vt_ref[0] = _dot_nt(wvt_ref[...], lb).astype(BF16)


def _vt_layout(T, S, tm, width, dtype=BF16):
    nb, cols = (T // S, S) if S % tm == 0 else (1, T)
    nt = cols // tm
    spec = pl.BlockSpec((1, width, tm), lambda i: (i // nt, 0, i % nt))
    return spec, jax.ShapeDtypeStruct((nb, width, cols), dtype)


def _kv_up(lat, kr, wk, wvt, e_mat, S, tm):
    T = lat.shape[0]
    row = lambda i: (i, 0)
    const = lambda i: (0, 0)
    vt_spec, vt_shape = _vt_layout(T, S, tm, MLA_HEADS * MLA_V)
    return pl.pallas_call(
        _kv_up_kernel,
        grid=(T // tm,),
        in_specs=[
            pl.BlockSpec((tm, MLA_KV_RANK), row), pl.BlockSpec((tm, MLA_ROPE), row),
            pl.BlockSpec(wk.shape, const), pl.BlockSpec(wvt.shape, const), pl.BlockSpec(e_mat.shape, const),
        ],
        out_specs=[pl.BlockSpec((tm, MLA_PAD), row), vt_spec],
        out_shape=[jax.ShapeDtypeStruct((T, MLA_PAD), BF16), vt_shape],
        compiler_params=_cparams(("parallel",)),
        name="kv_up",
    )(lat, kr, wk, wvt, e_mat)


def _retention_kernel(q_ref, k_ref, v_ref, g_ref, s0_ref, dm_ref, qd_ref, kd_ref, gl_ref,
                      o_ref, st_ref, *, L, nchunk):
    @pl.when(pl.program_id(1) == 0)
    def _():
        st_ref[...] = s0_ref[...]

    for c in range(nchunk):
        rows = slice(c * L, (c + 1) * L)
        for h in range(RET_HEADS):
            sl = slice(h * LANES, (h + 1) * LANES)
            q = q_ref[rows, sl]
            k = k_ref[rows, sl]
            v = v_ref[rows, sl]
            st = st_ref[0, h]
            a = (_dot_nt(q, k) * dm_ref[h]).astype(BF16)
            o = _dot(a, v) + _dot(q, st.astype(BF16)) * qd_ref[h]
            kdec = (k.astype(F32) * kd_ref[h]).astype(BF16)
            st_ref[0, h] = st * gl_ref[h] + _dot_tn(kdec, v)
            mu = jnp.mean(o, -1, keepdims=True)
            oc = o - mu
            var = jnp.mean(oc * oc, -1, keepdims=True)
            on = oc * lax.rsqrt(var + LN_EPS)
            g = g_ref[rows, sl].astype(F32)
            o_ref[rows, sl] = (g * jax.nn.sigmoid(g) * on).astype(BF16)


def _retention(rq, rk, rv, rg, state0, B, S):
    L = min(S, 256)
    lt = min(S, 512)
    nj = S // lt
    dm, qd, kd, gl = _retention_tables(L)
    row = lambda b, j: (b * nj + j, 0)
    c3 = lambda b, j: (0, 0, 0)
    st_spec = pl.BlockSpec((1, RET_HEADS, RET_DK, RET_DV), lambda b, j: (b, 0, 0, 0))
    return pl.pallas_call(
        functools.partial(_retention_kernel, L=L, nchunk=lt // L),
        grid=(B, nj),
        in_specs=[pl.BlockSpec((lt, RET_W), row)] * 4 + [
            st_spec,
            pl.BlockSpec(dm.shape, c3), pl.BlockSpec(qd.shape, c3),
            pl.BlockSpec(kd.shape, c3), pl.BlockSpec(gl.shape, c3),
        ],
        out_specs=[pl.BlockSpec((lt, RET_W), row), st_spec],
        out_shape=[jax.ShapeDtypeStruct((B * S, RET_W), BF16),
                   jax.ShapeDtypeStruct((B, RET_HEADS, RET_DK, RET_DV), F32)],
        compiler_params=_cparams(("parallel", "arbitrary")),
        name="retention",
    )(rq, rk, rv, rg, state0, dm, qd, kd, gl)


def _query_t(q, tq):
    q = q.astype(F32)
    if tq < LANES:
        q = jnp.concatenate([q, jnp.zeros((LANES - tq, LANES), F32)], 0)
    return q.T


def _flash_t(streams, qi, tq, tk, sk, sk_valid, causal, m_ref, l_ref, acc_ref):
    for s in range(len(streams)):
        m_ref[s] = jnp.full(m_ref.shape[1:], NEG_INF, F32)
        l_ref[s] = jnp.zeros(l_ref.shape[1:], F32)
        acc_ref[s] = jnp.zeros(acc_ref.shape[1:], F32)

    def step(start, size, diagonal=False, valid=None):
        for s, (q_t, k_at, vt_at) in enumerate(streams):
            st = _dot(k_at(start, size), q_t)
            if diagonal:
                kc = lax.broadcasted_iota(jnp.int32, st.shape, 0) // CHUNK
                qc = lax.broadcasted_iota(jnp.int32, st.shape, 1) // CHUNK
                st = jnp.where(kc <= qc, st, NEG_INF)
            if valid is not None:
                st = jnp.where(lax.broadcasted_iota(jnp.int32, st.shape, 0) < valid, st, NEG_INF)
            m_old = m_ref[s]
            m_new = jnp.maximum(m_old, jnp.max(st, 0, keepdims=True))
            p = jnp.exp2(st - m_new)
            alpha = jnp.exp2(m_old - m_new)
            l_ref[s] = alpha * l_ref[s] + jnp.sum(p, 0, keepdims=True)
            acc_ref[s] = acc_ref[s] * alpha + _dot(vt_at(start, size), p.astype(BF16))
            m_ref[s] = m_new

    if causal:
        n_full = qi * (tq // tk)

        def body(j, c):
            step(pl.multiple_of(2 * j * tk, tk), tk)
            step(pl.multiple_of((2 * j + 1) * tk, tk), tk)
            return c

        lax.fori_loop(0, n_full // 2, body, 0)

        @pl.when(n_full % 2 == 1)
        def _():
            step(pl.multiple_of((n_full - 1) * tk, tk), tk)

        step(pl.multiple_of(qi * tq, tq), tq, diagonal=True)
    else:
        for j in range(sk // tk):
            last_valid = sk_valid - j * tk
            step(j * tk, tk, valid=last_valid if last_valid < tk else None)
    return [acc_ref[s] / l_ref[s] for s in range(len(streams))]


def _attn_scratch(n_streams, dv, tq):
    tqp = max(tq, LANES)
    return [pltpu.VMEM((n_streams, 1, tqp), F32), pltpu.VMEM((n_streams, 1, tqp), F32),
            pltpu.VMEM((n_streams, dv, tqp), F32)]


def _mla_attn_kernel(q_ref, k_ref, vt_ref, o_ref, m_ref, l_ref, acc_ref, *, tq, tk, sk, sk_valid, causal):
    streams = []
    for hh in range(2):
        sl = slice(hh * LANES, (hh + 1) * LANES)
        vrows = slice(hh * MLA_V, (hh + 1) * MLA_V)
        k_at = lambda start, n, sl=sl: k_ref[pl.ds(start, n), sl]
        vt_at = lambda start, n, vrows=vrows: vt_ref[0, vrows, pl.ds(start, n)]
        streams.append((_query_t(q_ref[:, sl], tq).astype(BF16), k_at, vt_at))
    outs = _flash_t(streams, pl.program_id(2), tq, tk, sk, sk_valid, causal, m_ref, l_ref, acc_ref)
    o_ref[...] = jnp.concatenate(outs, 0).T[:tq].astype(BF16)


def _mla_attn(q, k, vt, B, sq, sk, sk_valid, tq, tk, causal):
    nq = sq // tq
    npair = MLA_HEADS // 2
    return pl.pallas_call(
        functools.partial(_mla_attn_kernel, tq=tq, tk=tk, sk=sk, sk_valid=sk_valid, causal=causal),
        grid=(B, npair, nq),
        in_specs=[
            pl.BlockSpec((tq, 2 * LANES), lambda b, p, i: (b * nq + i, p)),
            pl.BlockSpec((sk, 2 * LANES), lambda b, p, i: (b, p)),
            pl.BlockSpec((1, 2 * MLA_V, sk), lambda b, p, i: (b, p, 0)),
        ],
        out_specs=pl.BlockSpec((tq, LANES), lambda b, p, i: (b * nq + i, p)),
        out_shape=jax.ShapeDtypeStruct((B * sq, MLA_HEADS * MLA_V), BF16),
        scratch_shapes=_attn_scratch(2, MLA_V, tq),
        compiler_params=_cparams(("parallel", "parallel", "arbitrary")),
        name="mla_attn",
    )(q, k, vt)


def _diff_attn_kernel(lam_ref, gn_ref, q_ref, k_ref, vt_ref, o_ref, m_ref, l_ref, acc_ref,
                      *, tq, tk, sk, sk_valid, causal, lam_init):
    k_at = lambda start, n: k_ref[pl.ds(start, n), :]
    vt_at = lambda start, n: vt_ref[0, :, pl.ds(start, n)]
    q_t = _query_t(q_ref[...], tq)
    feat = lax.broadcasted_iota(jnp.int32, q_t.shape, 0)
    zero = jnp.zeros_like(q_t)
    q1 = jnp.where(feat < DIFF_HD, q_t, zero).astype(BF16)
    q2 = jnp.where(feat < DIFF_HD, zero, q_t).astype(BF16)
    o1, o2 = _flash_t([(q1, k_at, vt_at), (q2, k_at, vt_at)], pl.program_id(2), tq, tk, sk, sk_valid,
                      causal, m_ref, l_ref, acc_ref)
    lv = lam_ref[...]
    lam = (jnp.exp(jnp.sum(lv[0:1] * lv[1:2], -1, keepdims=True))
           - jnp.exp(jnp.sum(lv[2:3] * lv[3:4], -1, keepdims=True)) + lam_init)
    o = (o1 - lam * o2).T[:tq]
    o_ref[...] = (_rms_norm(o, gn_ref[...]) * (1.0 - lam_init)).astype(BF16)


def _diff_attn(lam_vecs, gn, q, k, vt, B, sq, sk, sk_valid, tq, tk, causal, lam_init):
    nq = sq // tq
    const = lambda b, h, i: (0, 0)
    return pl.pallas_call(
        functools.partial(_diff_attn_kernel, tq=tq, tk=tk, sk=sk, sk_valid=sk_valid, causal=causal,
                          lam_init=lam_init),
        grid=(B, DIFF_HEADS, nq),
        in_specs=[
            pl.BlockSpec(lam_vecs.shape, const), pl.BlockSpec(gn.shape, const),
            pl.BlockSpec((tq, LANES), lambda b, h, i: (b * nq + i, h)),
            pl.BlockSpec((sk, LANES), lambda b, h, i: (b, h)),
            pl.BlockSpec((1, DIFF_V, sk), lambda b, h, i: (b, h, 0)),
        ],
        out_specs=pl.BlockSpec((tq, LANES), lambda b, h, i: (b * nq + i, h)),
        out_shape=jax.ShapeDtypeStruct((B * sq, DIFF_HEADS * DIFF_V), BF16),
        scratch_shapes=_attn_scratch(2, DIFF_V, tq),
        compiler_params=_cparams(("parallel", "parallel", "arbitrary")),
        name="diff_attn",
    )(lam_vecs, gn, q, k, vt)


def _out_proj_kernel(*refs, n_in):
    a_refs = refs[:n_in]
    w_ref, x_ref, g_ref, b_ref, o_ref = refs[n_in:]
    y = None
    off = 0
    for a_ref in a_refs:
        width = a_ref.shape[1]
        part = _dot(a_ref[...], w_ref[off:off + width, :])
        y = part if y is None else y + part
        off += width
    o_ref[...] = _layer_norm(DN_ALPHA * x_ref[...] + y, g_ref[...], b_ref[...])


def _out_proj(acts, w, x, g, b, tm):
    T, D = x.shape
    row = lambda i: (i, 0)
    const = lambda i: (0, 0)
    return pl.pallas_call(
        functools.partial(_out_proj_kernel, n_in=len(acts)),
        grid=(T // tm,),
        in_specs=[pl.BlockSpec((tm, a.shape[1]), row) for a in acts] + [
            pl.BlockSpec(w.shape, const), pl.BlockSpec((tm, D), row),
            pl.BlockSpec(g.shape, const), pl.BlockSpec(b.shape, const),
        ],
        out_specs=pl.BlockSpec((tm, D), row),
        out_shape=jax.ShapeDtypeStruct((T, D), F32),
        compiler_params=_cparams(("parallel",)),
        name="out_proj",
    )(*acts, w, x, g, b)


def _odd_in_kernel(x_ref, w_ref, tab_ref, q_ref, kt_ref, kb_ref, vf_ref, vt_ref):
    xb = x_ref[...].astype(BF16)
    W = DIFF_HEADS * 2 * DIFF_HD
    hq = _dot(xb, w_ref[:, 0:W])
    hk = _dot(xb, w_ref[:, W:2 * W])
    for h in range(W // LANES):
        sl = slice(h * LANES, (h + 1) * LANES)
        q_ref[:, sl] = _rope(hq[:, sl], tab_ref[0], tab_ref[1], tab_ref[2], DIFF_HD // 2).astype(BF16)
        kk = _rope(hk[:, sl], tab_ref[3], tab_ref[4], tab_ref[5], DIFF_HD // 2)
        kt_ref[0, sl, :] = kk.T
        kb_ref[:, sl] = kk.astype(BF16)
    hv = _dot(xb, w_ref[:, 2 * W:])
    vf_ref[...] = hv
    vt_ref[0] = hv.T.astype(BF16)


def _odd_in(x, w, tabs, S, tm):
    T, D = x.shape
    W = DIFF_HEADS * 2 * DIFF_HD
    nt = tabs.shape[1] // tm
    row = lambda i: (i, 0)
    blk = pl.BlockSpec((tm, W), row)
    vt_spec, vt_shape = _vt_layout(T, S, tm, DIFF_HEADS * DIFF_V)
    kt_spec, kt_shape = _vt_layout(T, S, tm, W, F32)
    return pl.pallas_call(
        _odd_in_kernel,
        grid=(T // tm,),
        in_specs=[pl.BlockSpec((tm, D), row), pl.BlockSpec(w.shape, lambda i: (0, 0)),
                  pl.BlockSpec((tabs.shape[0], tm, LANES), lambda i: (0, i % nt, 0))],
        out_specs=[blk, kt_spec, blk, blk, vt_spec],
        out_shape=[jax.ShapeDtypeStruct((T, W), BF16), kt_shape,
                   jax.ShapeDtypeStruct((T, W), BF16), jax.ShapeDtypeStruct((T, W), F32), vt_shape],
        compiler_params=_cparams(("parallel",)),
        name="odd_in",
    )(x, w, tabs)


def _router_kernel(x_ref, wr_ref, bias_ref, tri_ref, g_ref, route_ref, cnt_out_ref, gt_ref, oh_ref, cnt_ref):
    tm = x_ref.shape[0]
    logits = _dot_nt(wr_ref[...], x_ref[...].astype(BF16))
    sc = jax.nn.sigmoid(logits)
    sel = sc + bias_ref[...]
    r = [sel[e:e + 1, :] for e in range(N_EXPERTS)]
    s = [sc[e:e + 1, :] for e in range(N_EXPERTS)]
    grp = []
    for g in range(N_GROUPS):
        a, b, c, d = r[4 * g:4 * g + 4]
        top2 = jnp.maximum(jnp.maximum(jnp.maximum(a + b, a + c), jnp.maximum(a + d, b + c)),
                           jnp.maximum(b + d, c + d))
        grp.append(top2)
    best = jnp.maximum(jnp.maximum(grp[0], grp[1]), jnp.maximum(grp[2], grp[3]))
    taken = jnp.zeros((1, tm), jnp.bool_)
    chosen = []
    for g in range(N_GROUPS):
        win = jnp.logical_and(grp[g] == best, jnp.logical_not(taken))
        chosen.append(win)
        taken = jnp.logical_or(taken, win)
    picked = []
    for e in range(N_EXPERTS):
        g = e // EXPERTS_PER_GROUP
        rank = jnp.zeros((1, tm), F32)
        for k in range(4 * g, 4 * g + 4):
            if k < e:
                rank = rank + (r[k] >= r[e]).astype(F32)
            elif k > e:
                rank = rank + (r[k] > r[e]).astype(F32)
        picked.append(jnp.logical_and(chosen[g], rank < 2.0))
    w = [jnp.where(picked[e], s[e], 0.0) for e in range(N_EXPERTS)]
    denom = w[0]
    for e in range(1, N_EXPERTS):
        denom = denom + w[e]
    gt_ref[...] = jnp.zeros_like(gt_ref)
    for e in range(N_EXPERTS):
        gt_ref[e:e + 1, :] = w[e] / denom
    g_ref[...] = gt_ref[...].T

    @pl.when(pl.program_id(0) == 0)
    def _():
        cnt_ref[...] = jnp.zeros_like(cnt_ref)

    oh_ref[...] = jnp.zeros_like(oh_ref)
    for g in range(N_GROUPS):
        oh_ref[g:g + 1, :] = chosen[g].astype(F32)
    oh = oh_ref[...]
    before = _dot(oh.astype(BF16), tri_ref[...])
    base = cnt_ref[:, 0:1]
    rank = jnp.sum(oh * (base + before), 0, keepdims=True)
    gid = jnp.sum(oh * lax.broadcasted_iota(jnp.int32, oh.shape, 0).astype(F32), 0, keepdims=True)
    row = lax.broadcasted_iota(jnp.int32, oh.shape, 0)
    route_ref[...] = jnp.where(row == 0, gid, jnp.where(row == 1, rank, 0.0)).astype(jnp.int32)
    cnt_ref[...] = cnt_ref[...] + jnp.sum(oh, 1, keepdims=True)
    cnt_out_ref[...] = cnt_ref[...]


def _router(x, wr_t, bias, tm):
    T, D = x.shape
    tri = jnp.asarray(np.triu(np.ones((tm, tm), np.float32), 1)).astype(BF16)
    const = lambda i: (0, 0)
    return pl.pallas_call(
        _router_kernel,
        grid=(T // tm,),
        in_specs=[pl.BlockSpec((tm, D), lambda i: (i, 0)), pl.BlockSpec(wr_t.shape, const),
                  pl.BlockSpec(bias.shape, const), pl.BlockSpec(tri.shape, const)],
        out_specs=[pl.BlockSpec((tm, LANES), lambda i: (i, 0)), pl.BlockSpec((8, tm), lambda i: (0, i)),
                   pl.BlockSpec((8, LANES), const)],
        out_shape=[jax.ShapeDtypeStruct((T, LANES), F32), jax.ShapeDtypeStruct((8, T), jnp.int32),
                   jax.ShapeDtypeStruct((8, LANES), F32)],
        scratch_shapes=[pltpu.VMEM((LANES, tm), F32), pltpu.VMEM((8, tm), F32), pltpu.VMEM((8, LANES), F32)],
        compiler_params=_cparams(("arbitrary",)),
        name="router",
    )(x, wr_t, bias, tri)


def _moe_dense_kernel(x_ref, gate_ref, wg_ref, wu_ref, wd_ref, g_ref, b_ref, o_ref, xb_ref, acc_ref):
    e = pl.program_id(1)

    @pl.when(e == 0)
    def _():
        xb_ref[...] = x_ref[...].astype(BF16)
        acc_ref[...] = jnp.zeros_like(acc_ref)

    xb = xb_ref[...]
    h = jax.nn.silu(_dot(xb, wg_ref[0, 0].astype(BF16))) * _dot(xb, wu_ref[0, 0].astype(BF16))
    y = _dot(h.astype(BF16), wd_ref[0, 0].astype(BF16))
    gates = gate_ref[...]
    lane = lax.broadcasted_iota(jnp.int32, gates.shape, 1)
    gcol = jnp.sum(jnp.where(lane == e, gates, 0.0), -1, keepdims=True)
    acc_ref[...] += gcol * y

    @pl.when(e == pl.num_programs(1) - 1)
    def _():
        o_ref[...] = _layer_norm(DN_ALPHA * x_ref[...] + acc_ref[...], g_ref[...], b_ref[...])


def _moe_dense(x, gates, wg, wu, wd, l, g, b, tm):
    T, D = x.shape
    _, E, _, H = wg.shape
    row = lambda i, e: (i, 0)
    const = lambda i, e: (0, 0)
    wsel = lambda i, e: (l, e, 0, 0)
    return pl.pallas_call(
        _moe_dense_kernel,
        grid=(T // tm, E),
        in_specs=[
            pl.BlockSpec((tm, D), row), pl.BlockSpec((tm, LANES), row),
            pl.BlockSpec((1, 1, D, H), wsel), pl.BlockSpec((1, 1, D, H), wsel),
            pl.BlockSpec((1, 1, H, D), wsel),
            pl.BlockSpec(g.shape, const), pl.BlockSpec(b.shape, const),
        ],
        out_specs=pl.BlockSpec((tm, D), row),
        out_shape=jax.ShapeDtypeStruct((T, D), F32),
        scratch_shapes=[pltpu.VMEM((tm, D), BF16), pltpu.VMEM((tm, D), F32)],
        compiler_params=_cparams(("parallel", "arbitrary")),
        name="moe_dense",
    )(x, gates, wg, wu, wd, g, b)


SC_CORES = 2
SC_SUBCORES = 16
SC_WORKERS = SC_CORES * SC_SUBCORES
SC_TILE_BYTES = 384 * 1024
SPARSE_ROW_TILE = 1024
SPARSE_MIN_TOKENS = 4096


def _sc_ring(per_w, row_bytes):
    for ch, nbuf in ((16, 4), (16, 2), (8, 2)):
        if per_w % (ch * nbuf) == 0 and ch * nbuf * row_bytes <= SC_TILE_BYTES:
            return ch, nbuf
    raise ValueError(f"no SparseCore gather ring for {per_w} rows of {row_bytes} bytes per subcore")


def _sc_gather(table, idx):
    R = idx.shape[0]
    D = table.shape[1]
    per_w = R // SC_WORKERS
    assert per_w * SC_WORKERS == R and per_w % 8 == 0
    ch, nbuf = _sc_ring(per_w, D * table.dtype.itemsize)
    nchunk = per_w // ch
    mesh = plsc.VectorSubcoreMesh(core_axis_name="c", subcore_axis_name="s")

    @functools.partial(
        pl.kernel, mesh=mesh,
        out_type=jax.ShapeDtypeStruct((R, D), table.dtype),
        scratch_types=[pltpu.VMEM((per_w,), jnp.int32), pltpu.VMEM((nbuf, ch, D), table.dtype),
                       pltpu.SemaphoreType.DMA((nbuf,)), pltpu.SemaphoreType.DMA((nbuf,))],
    )
    def gather_kernel(table_hbm, idx_hbm, out_hbm, idx_v, rows_v, gsem, wsem):
        base = (lax.axis_index("s") * SC_CORES + lax.axis_index("c")) * per_w
        pltpu.sync_copy(idx_hbm.at[pl.ds(base, per_w)], idx_v)

        def gather(c, b):
            return pltpu.make_async_copy(table_hbm.at[idx_v.at[pl.ds(c * ch, ch)]], rows_v.at[b], gsem.at[b])

        def write(c, b):
            return pltpu.make_async_copy(rows_v.at[b], out_hbm.at[pl.ds(base + c * ch, ch)], wsem.at[b])

        for b in range(nbuf - 1):
            gather(b, b).start()

        @pl.loop(0, nchunk, step=nbuf)
        def _(c):
            for b in range(nbuf):
                cc = c + b
                gather(cc, b).wait()
                write(cc, b).start()
                pb = (b - 1) % nbuf

                @pl.when(cc + nbuf - 1 < nchunk)
                def _():
                    @pl.when(cc >= 1)
                    def _():
                        write(cc - 1, pb).wait()

                    gather(cc + nbuf - 1, pb).start()

        for b in range(nbuf):
            write(nchunk - nbuf + b, b).wait()

    return gather_kernel(table, idx)


def _moe_group_kernel(tg_ref, nv_ref, x_ref, gate_ref, wg_ref, wu_ref, wd_ref, g_ref, b_ref, o_ref,
                      xb_ref, acc_ref):
    j = pl.program_id(0)
    e = pl.program_id(1)
    live = j < nv_ref[0]

    @pl.when(jnp.logical_and(live, e == 0))
    def _():
        xb_ref[...] = x_ref[...].astype(BF16)
        acc_ref[...] = jnp.zeros_like(acc_ref)

    @pl.when(live)
    def _():
        xb = xb_ref[...]
        h = jax.nn.silu(_dot(xb, wg_ref[0, 0].astype(BF16))) * _dot(xb, wu_ref[0, 0].astype(BF16))
        y = _dot(h.astype(BF16), wd_ref[0, 0].astype(BF16))
        gates = gate_ref[...]
        lane = lax.broadcasted_iota(jnp.int32, gates.shape, 1)
        expert = tg_ref[j] * EXPERTS_PER_GROUP + e
        gcol = jnp.sum(jnp.where(lane == expert, gates, 0.0), -1, keepdims=True)
        acc_ref[...] += gcol * y

    last = e == pl.num_programs(1) - 1

    @pl.when(jnp.logical_and(live, last))
    def _():
        o_ref[...] = _layer_norm(DN_ALPHA * x_ref[...] + acc_ref[...], g_ref[...], b_ref[...])

    @pl.when(jnp.logical_and(jnp.logical_not(live), last))
    def _():
        o_ref[...] = jnp.zeros_like(o_ref)


def _moe_group(tile_group, n_valid, xs, gs, wg, wu, wd, l, g, b, tm):
    R, D = xs.shape
    H = wg.shape[3]
    row = lambda j, e, tg, nv: (j, 0)
    const = lambda j, e, tg, nv: (0, 0)
    wsel = lambda j, e, tg, nv: (l, tg[j] * EXPERTS_PER_GROUP + e, 0, 0)
    grid_spec = pltpu.PrefetchScalarGridSpec(
        num_scalar_prefetch=2,
        grid=(R // tm, EXPERTS_PER_GROUP),
        in_specs=[
            pl.BlockSpec((tm, D), row), pl.BlockSpec((tm, LANES), row),
            pl.BlockSpec((1, 1, D, H), wsel), pl.BlockSpec((1, 1, D, H), wsel),
            pl.BlockSpec((1, 1, H, D), wsel),
            pl.BlockSpec(g.shape, const), pl.BlockSpec(b.shape, const),
        ],
        out_specs=pl.BlockSpec((tm, D), row),
        scratch_shapes=[pltpu.VMEM((tm, D), BF16), pltpu.VMEM((tm, D), F32)],
    )
    return pl.pallas_call(
        _moe_group_kernel,
        grid_spec=grid_spec,
        out_shape=jax.ShapeDtypeStruct((R, D), F32),
        compiler_params=_cparams(("parallel", "arbitrary")),
        name="moe_group",
    )(tile_group, n_valid, xs, gs, wg, wu, wd, g, b)


def _sc_scatter(x, dest, pad_dest, n_rows):
    T, D = x.shape
    P = pad_dest.shape[0]
    per_w, pad_w = T // SC_WORKERS, P // SC_WORKERS
    assert per_w * SC_WORKERS == T and pad_w * SC_WORKERS == P and T + P == n_rows
    ch, nbuf = _sc_ring(per_w, D * x.dtype.itemsize)
    assert pad_w % ch == 0
    nchunk, npad = per_w // ch, pad_w // ch
    mesh = plsc.VectorSubcoreMesh(core_axis_name="c", subcore_axis_name="s")

    @functools.partial(
        pl.kernel, mesh=mesh,
        out_type=jax.ShapeDtypeStruct((n_rows, D), x.dtype),
        scratch_types=[pltpu.VMEM((nchunk, ch), jnp.int32), pltpu.VMEM((npad, ch), jnp.int32),
                       pltpu.VMEM((nbuf, ch, D), x.dtype), pltpu.VMEM((ch, D), x.dtype),
                       pltpu.SemaphoreType.DMA((nbuf,)), pltpu.SemaphoreType.DMA((nbuf,)),
                       pltpu.SemaphoreType.DMA],
    )
    def scatter_kernel(x_hbm, dest_hbm, pad_hbm, zero_hbm, out_hbm, dest_v, pad_v, rows_v, zero_v,
                       rsem, wsem, zsem):
        wid = lax.axis_index("s") * SC_CORES + lax.axis_index("c")
        base = wid * per_w
        pltpu.sync_copy(dest_hbm.at[wid], dest_v)
        pltpu.sync_copy(pad_hbm.at[wid], pad_v)
        pltpu.sync_copy(zero_hbm, zero_v)

        def read(c, b):
            return pltpu.make_async_copy(x_hbm.at[pl.ds(base + c * ch, ch)], rows_v.at[b], rsem.at[b])

        def write(c, b):
            return pltpu.make_async_copy(rows_v.at[b], out_hbm.at[dest_v.at[c]], wsem.at[b])

        def write_zero(c):
            return pltpu.make_async_copy(zero_v, out_hbm.at[pad_v.at[c]], zsem)

        for c in range(npad):
            write_zero(c).start()
        for b in range(nbuf - 1):
            read(b, b).start()

        @pl.loop(0, nchunk, step=nbuf)
        def _(c):
            for b in range(nbuf):
                cc = c + b
                read(cc, b).wait()
                write(cc, b).start()
                pb = (b - 1) % nbuf

                @pl.when(cc + nbuf - 1 < nchunk)
                def _():
                    @pl.when(cc >= 1)
                    def _():
                        write(cc - 1, pb).wait()

                    read(cc + nbuf - 1, pb).start()

        for b in range(nbuf):
            write(nchunk - nbuf + b, b).wait()
        for c in range(npad):
            write_zero(c).wait()

    return scatter_kernel(x, dest.reshape(SC_WORKERS, nchunk, ch), pad_dest.reshape(SC_WORKERS, npad, ch),
                          jnp.zeros((ch, D), x.dtype))


def _moe_sparse(x, gates, route, counts, wg, wu, wd, l, g, b):
    T = x.shape[0]
    tm = SPARSE_ROW_TILE
    n_tiles = T // tm + N_GROUPS
    n_rows = n_tiles * tm
    gid, rank = route[0], route[1]
    cnt = counts[:N_GROUPS, 0].astype(jnp.int32)
    tiles = (cnt + tm - 1) // tm
    ends = jnp.cumsum(tiles)
    starts = (ends - tiles) * tm
    pick = lambda which, vals: sum(jnp.where(which == k, vals[k], 0) for k in range(N_GROUPS))
    dest = rank + pick(gid, starts)
    pad_cnt = tiles * tm - cnt
    pad_end = jnp.cumsum(pad_cnt)
    p = jnp.arange(n_rows - T, dtype=jnp.int32)
    seg = sum((p >= pad_end[k]).astype(jnp.int32) for k in range(N_GROUPS))
    pad_dest = jnp.where(seg < N_GROUPS,
                         pick(seg, starts + cnt) + p - pick(seg, pad_end - pad_cnt),
                         ends[N_GROUPS - 1] * tm + p - pad_end[N_GROUPS - 1])
    tile_group = jnp.minimum(jnp.searchsorted(ends, jnp.arange(n_tiles, dtype=jnp.int32), side="right"),
                             N_GROUPS - 1).astype(jnp.int32)
    xs = _sc_scatter(x, dest, pad_dest, n_rows)
    gs = _sc_scatter(gates, dest, pad_dest, n_rows)
    ys = _moe_group(tile_group, ends[N_GROUPS - 1:], xs, gs, wg, wu, wd, l, g, b, tm)
    return _sc_gather(ys, dest)


def _prep_weights(w_in_even, w_uq, w_ukv, w_out_even, w_in_odd, w_out_odd, w_router,
                  w_expert_gate, w_expert_up, w_expert_down):
    d = w_in_even.shape[1]
    n_main = 4 * RET_W + MLA_Q_RANK + MLA_KV_RANK
    w_in = w_in_even[0]
    kr_cols = jnp.pad(w_in[:, n_main:], ((0, 0), (MLA_NOPE, LANES - MLA_NOPE - MLA_ROPE)))
    w_a = jnp.concatenate([w_in[:, :n_main], kr_cols], 1).astype(BF16)
    qd = MLA_NOPE + MLA_ROPE
    wq = jnp.pad(w_uq[0].reshape(MLA_Q_RANK, MLA_HEADS, qd), ((0, 0), (0, 0), (0, LANES - qd)))
    wq = wq.reshape(MLA_Q_RANK, MLA_PAD).astype(BF16)
    wkv = w_ukv[0].reshape(MLA_KV_RANK, MLA_HEADS, MLA_NOPE + MLA_V)
    wk = jnp.pad(wkv[:, :, :MLA_NOPE], ((0, 0), (0, 0), (0, LANES - MLA_NOPE)))
    wk = wk.reshape(MLA_KV_RANK, MLA_PAD).astype(BF16)
    wvt = wkv[:, :, MLA_NOPE:].reshape(MLA_KV_RANK, MLA_HEADS * MLA_V).T.astype(BF16)
    e_np = np.zeros((MLA_ROPE, MLA_HEADS, LANES), np.float32)
    for j in range(MLA_ROPE):
        e_np[j, :, MLA_NOPE + j] = 1.0
    e_mat = jnp.asarray(e_np.reshape(MLA_ROPE, MLA_PAD)).astype(BF16)
    return dict(
        w_a=w_a, wq=wq, wk=wk, wvt=wvt, e_mat=e_mat,
        w_out_even=w_out_even[0].astype(BF16), w_in_odd=w_in_odd[0].astype(BF16),
        w_out_odd=w_out_odd[0].astype(BF16), wr_t=w_router.T.astype(BF16),
        wg=w_expert_gate, wu=w_expert_up, wd=w_expert_down,
    )


def _moe(x, wts, l, bias, ln_g, ln_b, tm):
    gates, route, counts = _router(x, wts["wr_t"], bias, tm)
    g, b = ln_g[l, 1][None], ln_b[l, 1][None]
    if x.shape[0] >= SPARSE_MIN_TOKENS:
        return _moe_sparse(x, gates, route, counts, wts["wg"], wts["wu"], wts["wd"], l, g, b)
    return _moe_dense(x, gates, wts["wg"], wts["wu"], wts["wd"], l, g, b, tm)


def _trunk(x3, pos0, past, wts, prm):
    B, S, D = x3.shape
    T = B * S
    x = x3.reshape(T, D)
    tm = min(T, 512)
    rep = max(tm // S, 1)
    pos = np.tile(pos0 + np.arange(S), rep)
    ln_g, ln_b = prm["ln_g"], prm["ln_b"]

    rq, rk, rv, rg, q, lat, kr = _even_in(x, wts["w_a"], wts["wq"], prm["gq"], prm["gkv"], _even_tables(pos), tm)
    causal = past is None
    if causal:
        state0 = jnp.zeros((B, RET_HEADS, RET_DK, RET_DV), F32)
        lat_all, kr_all, sk, skp = lat, kr, S, S
        tq = tk = min(S, 512)
    else:
        state0 = past["state"]
        sk = past["lat"].shape[1] + S
        skp = -(-sk // LANES) * LANES
        padk = lambda parts: jnp.concatenate(
            parts + [jnp.zeros((B, skp - sk, parts[0].shape[2]), parts[0].dtype)], 1).reshape(B * skp, -1)
        lat_all = padk([past["lat"], lat.reshape(B, S, -1)])
        kr_all = padk([past["kr"], kr.reshape(B, S, -1)])
        tq, tk = S, skp
    ret_out, ret_state = _retention(rq, rk, rv, rg, state0, B, S)
    tkv = 512 if (B * skp) % 512 == 0 else skp
    k_mla, vt_mla = _kv_up(lat_all, kr_all, wts["wk"], wts["wvt"], wts["e_mat"], skp, tkv)
    if vt_mla.shape[0] != B:
        vt_mla = vt_mla.reshape(-1, B, skp).transpose(1, 0, 2)
    mla_out = _mla_attn(q, k_mla, vt_mla, B, S, skp, sk, tq, tk, causal)
    x = _out_proj([ret_out, mla_out], wts["w_out_even"], x, ln_g[0, 0][None], ln_b[0, 0][None], tm)
    x = _moe(x, wts, 0, prm["bias"], ln_g, ln_b, tm)

    qd, kt, kb, vf, vt = _odd_in(x, wts["w_in_odd"], _odd_tables(pos), S, tm)
    if vt.shape[0] != B:
        vt = vt.reshape(-1, B, S).transpose(1, 0, 2)
        kt = kt.reshape(-1, B, S).transpose(1, 0, 2)
    kf = kt.reshape(1, B, 2 * DIFF_HEADS, DIFF_HD, S).transpose(0, 1, 4, 2, 3)
    if causal:
        k_all, vt_all = kb, vt
    else:
        k_all = padk([past["dk"], kb.reshape(B, S, -1)])
        vt_all = jnp.concatenate([past["dv"].transpose(0, 2, 1), vt,
                                  jnp.zeros((B, vt.shape[1], skp - sk), BF16)], 2)
    lam_init = 0.8 - 0.6 * math.exp(-0.3 * 1)
    d_out = _diff_attn(prm["lam"], prm["gn"], qd, k_all, vt_all, B, S, skp, sk, tq, tk, causal, lam_init)
    x = _out_proj([d_out], wts["w_out_odd"], x, ln_g[1, 0][None], ln_b[1, 0][None], tm)
    x = _moe(x, wts, 1, prm["bias"], ln_g, ln_b, tm)

    return (x.reshape(B, S, D), ret_state[None], lat.reshape(1, B, S, -1), kr.reshape(1, B, S, -1),
            kf, vf.reshape(1, B, S, DIFF_HEADS, DIFF_V))


def kernel(x_prompt, x_sample, state_ret, cache_mla_latent, cache_mla_krope, cache_diff_k, cache_diff_v,
           w_in_even, w_uq, w_ukv, g_qnorm, g_kvnorm, w_out_even,
           w_in_odd, lambda_q1, lambda_k1, lambda_q2, lambda_k2, g_diff_norm, w_out_odd,
           ln_g, ln_b, w_router, router_bias, w_expert_gate, w_expert_up, w_expert_down):
    wts = _prep_weights(w_in_even, w_uq, w_ukv, w_out_even, w_in_odd, w_out_odd, w_router,
                        w_expert_gate, w_expert_up, w_expert_down)
    prm = dict(
        gq=g_qnorm[0][None].astype(F32), gkv=g_kvnorm[0][None].astype(F32),
        lam=jnp.stack([lambda_q1[0], lambda_k1[0], lambda_q2[0], lambda_k2[0]]).astype(F32),
        gn=g_diff_norm[0][None].astype(F32), bias=router_bias.reshape(N_EXPERTS, 1).astype(F32),
        ln_g=ln_g.astype(F32), ln_b=ln_b.astype(F32),
    )
    past_len = cache_mla_latent.shape[2]
    db = x_sample.shape[0]
    past = dict(
        state=state_ret[0].astype(F32), lat=cache_mla_latent[0], kr=cache_mla_krope[0],
        dk=cache_diff_k[0].reshape(db, past_len, -1).astype(BF16),
        dv=cache_diff_v[0].reshape(db, past_len, -1).astype(BF16),
    )
    outs_p = _trunk(x_prompt, 0, None, wts, prm)
    outs_s = _trunk(x_sample, past_len, past, wts, prm)
    return (outs_p[0], outs_s[0]) + outs_p[1:] + outs_s[1:]
```

```python
import functools
import math

import numpy as np
import jax
import jax.numpy as jnp
from jax import lax
from jax.experimental import pallas as pl
from jax.experimental.pallas import tpu as pltpu
from jax.experimental.pallas import tpu_sc as plsc

F32 = jnp.float32
BF16 = jnp.bfloat16

CHUNK = 64
ROPE_THETA = 10000.0
NEG_INF = -1e30
LN_EPS = 1e-5
NORM_EPS = 1e-6
DEPTH = 2
DN_ALPHA = (2.0 * DEPTH) ** 0.25
RET_HEADS = 4
RET_DK = 128
RET_DV = 128
RET_LOG_GAMMA = tuple(math.log(1.0 - 2.0 ** (-5 - h)) for h in range(RET_HEADS))
MLA_HEADS = 8
MLA_Q_RANK = 384
MLA_KV_RANK = 256
MLA_NOPE = 64
MLA_ROPE = 32
MLA_V = 64
DIFF_HEADS = 8
DIFF_HD = 64
DIFF_V = 128
N_EXPERTS = 16
N_GROUPS = 4
EXPERTS_PER_GROUP = 4
TOP_K = 2
EXPERT_PAIRS = tuple((a, b) for a in range(EXPERTS_PER_GROUP) for b in range(a + 1, EXPERTS_PER_GROUP))
N_PAIRS = len(EXPERT_PAIRS)
N_SEG = N_GROUPS * N_PAIRS
SEG_ROWS = 32
LOG2E = math.log2(math.e)

LANES = 128
RET_W = RET_HEADS * RET_DK
MLA_PAD = MLA_HEADS * LANES
VMEM_LIMIT = 56 * 1024 * 1024


def _cparams(sem):
    return pltpu.CompilerParams(dimension_semantics=sem, vmem_limit_bytes=VMEM_LIMIT)


def _rope_tables(pos, d, group, offset, scale):
    pos = np.asarray(pos, np.float64)
    half = d // 2
    inv = 1.0 / (ROPE_THETA ** (np.arange(0, d, 2, dtype=np.float64) / d))
    ang = pos[:, None] * inv[None, :]
    cos = np.full((pos.shape[0], LANES), scale, np.float64)
    s_lo = np.zeros((pos.shape[0], LANES), np.float64)
    s_hi = np.zeros((pos.shape[0], LANES), np.float64)
    start = offset
    while start + d <= LANES:
        cos[:, start:start + half] = np.cos(ang) * scale
        cos[:, start + half:start + d] = np.cos(ang) * scale
        s_lo[:, start:start + half] = -np.sin(ang) * scale
        s_hi[:, start + half:start + d] = np.sin(ang) * scale
        start += group
    return cos, s_lo, s_hi


def _even_tables(pos):
    rq = _rope_tables(pos, RET_DK, LANES, 0, 1.0)
    rk = _rope_tables(pos, RET_DK, LANES, 0, RET_DK ** -0.5)
    c = (MLA_NOPE + MLA_ROPE) ** -0.5 * LOG2E
    mq = _rope_tables(pos, MLA_ROPE, LANES, MLA_NOPE, c)
    mk = _rope_tables(pos, MLA_ROPE, LANES, MLA_NOPE, 1.0)
    tabs = [rq[0], rq[1] + rq[2], rk[0], rk[1] + rk[2], mq[0], mq[1], mq[2], mk[0], mk[1], mk[2]]
    return jnp.asarray(np.stack(tabs).astype(np.float32))


def _odd_tables(pos):
    c = DIFF_HD ** -0.5 * LOG2E
    dq = _rope_tables(pos, DIFF_HD, DIFF_HD, 0, c)
    dk = _rope_tables(pos, DIFF_HD, DIFF_HD, 0, 1.0)
    return jnp.asarray(np.stack(list(dq) + list(dk)).astype(np.float32))


def _retention_tables(L):
    lg = np.asarray(RET_LOG_GAMMA, np.float64)
    idx = np.arange(L, dtype=np.float64)
    diff = idx[:, None] - idx[None, :]
    dmask = np.where(diff[None] >= 0, np.exp(np.maximum(diff, 0.0)[None] * lg[:, None, None]), 0.0)
    qd = np.exp((idx[None, :] + 1.0) * lg[:, None])
    kd = np.exp((L - 1.0 - idx)[None, :] * lg[:, None])
    gl = np.exp(L * lg)
    qd = np.broadcast_to(qd[:, :, None], (RET_HEADS, L, LANES))
    kd = np.broadcast_to(kd[:, :, None], (RET_HEADS, L, LANES))
    gl = np.broadcast_to(gl[:, None, None], (RET_HEADS, RET_DK, RET_DV))
    f = lambda a: jnp.asarray(np.ascontiguousarray(a).astype(np.float32))
    return f(dmask), f(qd), f(kd), f(gl)


def _dot(a, b):
    return jnp.dot(a, b, preferred_element_type=F32)


def _dot_nt(a, b):
    return lax.dot_general(a, b, (((1,), (1,)), ((), ())), preferred_element_type=F32)


def _dot_tn(a, b):
    return lax.dot_general(a, b, (((0,), (0,)), ((), ())), preferred_element_type=F32)


def _rope(x, cos, s_lo, s_hi, half):
    return x * cos + pltpu.roll(x, LANES - half, 1) * s_lo + pltpu.roll(x, half, 1) * s_hi


def _layer_norm(x, g, b):
    mu = jnp.mean(x, -1, keepdims=True)
    xc = x - mu
    var = jnp.mean(xc * xc, -1, keepdims=True)
    return xc * lax.rsqrt(var + LN_EPS) * g + b


def _rms_norm(x, g):
    return x * lax.rsqrt(jnp.mean(x * x, -1, keepdims=True) + NORM_EPS) * g


def _even_in_kernel(x_ref, w_ref, wq_ref, gq_ref, gkv_ref, tab_ref,
                    rq_ref, rk_ref, rv_ref, rg_ref, q_ref, lat_ref, kr_ref):
    xb = x_ref[...].astype(BF16)
    c_rq, s_rq, c_rk, s_rk = tab_ref[0], tab_ref[1], tab_ref[2], tab_ref[3]
    hq = _dot(xb, w_ref[:, 0:RET_W])
    hk = _dot(xb, w_ref[:, RET_W:2 * RET_W])
    for h in range(RET_HEADS):
        sl = slice(h * LANES, (h + 1) * LANES)
        xq = hq[:, sl]
        rq_ref[:, sl] = (xq * c_rq + pltpu.roll(xq, RET_DK // 2, 1) * s_rq).astype(BF16)
        xk = hk[:, sl]
        rk_ref[:, sl] = (xk * c_rk + pltpu.roll(xk, RET_DK // 2, 1) * s_rk).astype(BF16)
    rv_ref[...] = _dot(xb, w_ref[:, 2 * RET_W:3 * RET_W]).astype(BF16)
    rg_ref[...] = _dot(xb, w_ref[:, 3 * RET_W:4 * RET_W]).astype(BF16)
    o = 4 * RET_W
    cq = _dot(xb, w_ref[:, o:o + MLA_Q_RANK])
    qn = _rms_norm(cq, gq_ref[...]).astype(BF16)
    qf = _dot(qn, wq_ref[...])
    c_q, lo_q, hi_q = tab_ref[4], tab_ref[5], tab_ref[6]
    for h in range(MLA_HEADS):
        sl = slice(h * LANES, (h + 1) * LANES)
        q_ref[:, sl] = _rope(qf[:, sl], c_q, lo_q, hi_q, MLA_ROPE // 2).astype(BF16)
    o += MLA_Q_RANK
    ckv = _dot(xb, w_ref[:, o:o + MLA_KV_RANK])
    lat_ref[...] = _rms_norm(ckv, gkv_ref[...])
    o += MLA_KV_RANK
    krp = _dot(xb, w_ref[:, o:o + LANES])
    krp = _rope(krp, tab_ref[7], tab_ref[8], tab_ref[9], MLA_ROPE // 2)
    kr_ref[...] = krp[:, MLA_NOPE:MLA_NOPE + MLA_ROPE]


def _even_in(x, w_a, wq, gq, gkv, tabs, tm):
    T, D = x.shape
    P = tabs.shape[1]
    nt = P // tm
    row = lambda i: (i, 0)
    const = lambda i: (0, 0)
    outs = [
        jax.ShapeDtypeStruct((T, RET_W), BF16), jax.ShapeDtypeStruct((T, RET_W), BF16),
        jax.ShapeDtypeStruct((T, RET_W), BF16), jax.ShapeDtypeStruct((T, RET_W), BF16),
        jax.ShapeDtypeStruct((T, MLA_PAD), BF16),
        jax.ShapeDtypeStruct((T, MLA_KV_RANK), F32), jax.ShapeDtypeStruct((T, MLA_ROPE), F32),
    ]
    return pl.pallas_call(
        _even_in_kernel,
        grid=(T // tm,),
        in_specs=[
            pl.BlockSpec((tm, D), row),
            pl.BlockSpec(w_a.shape, const),
            pl.BlockSpec(wq.shape, const),
            pl.BlockSpec(gq.shape, const),
            pl.BlockSpec(gkv.shape, const),
            pl.BlockSpec((tabs.shape[0], tm, LANES), lambda i: (0, i % nt, 0)),
        ],
        out_specs=[
            pl.BlockSpec((tm, RET_W), row), pl.BlockSpec((tm, RET_W), row),
            pl.BlockSpec((tm, RET_W), row), pl.BlockSpec((tm, RET_W), row),
            pl.BlockSpec((tm, MLA_PAD), row),
            pl.BlockSpec((tm, MLA_KV_RANK), row), pl.BlockSpec((tm, MLA_ROPE), row),
        ],
        out_shape=outs,
        compiler_params=_cparams(("parallel",)),
        name="even_in",
    )(x, w_a, wq, gq, gkv, tabs)


def _kv_up_kernel(lat_ref, kr_ref, wk_ref, wvt_ref, e_ref, k_ref, vt_ref):
    lb = lat_ref[...].astype(BF16)
    krb = kr_ref[...].astype(BF16)
    k_ref[...] = (_dot(lb, wk_ref[...]) + _dot(krb, e_ref[...])).astype(BF16)
    vt_ref[0] = _dot_nt(wvt_ref[...], lb).astype(BF16)


def _vt_layout(T, S, tm, width, dtype=BF16):
    nb, cols = (T // S, S) if S % tm == 0 else (1, T)
    nt = cols // tm
    spec = pl.BlockSpec((1, width, tm), lambda i: (i // nt, 0, i % nt))
    return spec, jax.ShapeDtypeStruct((nb, width, cols), dtype)


def _kv_up(lat, kr, wk, wvt, e_mat, S, tm):
    T = lat.shape[0]
    row = lambda i: (i, 0)
    const = lambda i: (0, 0)
    vt_spec, vt_shape = _vt_layout(T, S, tm, MLA_HEADS * MLA_V)
    return pl.pallas_call(
        _kv_up_kernel,
        grid=(T // tm,),
        in_specs=[
            pl.BlockSpec((tm, MLA_KV_RANK), row), pl.BlockSpec((tm, MLA_ROPE), row),
            pl.BlockSpec(wk.shape, const), pl.BlockSpec(wvt.shape, const), pl.BlockSpec(e_mat.shape, const),
        ],
        out_specs=[pl.BlockSpec((tm, MLA_PAD), row), vt_spec],
        out_shape=[jax.ShapeDtypeStruct((T, MLA_PAD), BF16), vt_shape],
        compiler_params=_cparams(("parallel",)),
        name="kv_up",
    )(lat, kr, wk, wvt, e_mat)


def _retention_kernel(q_ref, k_ref, v_ref, g_ref, s0_ref, dm_ref, qd_ref, kd_ref, gl_ref,
                      o_ref, st_ref, *, L, nchunk):
    @pl.when(pl.program_id(1) == 0)
    def _():
        st_ref[...] = s0_ref[...]

    for c in range(nchunk):
        rows = slice(c * L, (c + 1) * L)
        for h in range(RET_HEADS):
            sl = slice(h * LANES, (h + 1) * LANES)
            q = q_ref[rows, sl]
            k = k_ref[rows, sl]
            v = v_ref[rows, sl]
            st = st_ref[0, h]
            a = (_dot_nt(q, k) * dm_ref[h]).astype(BF16)
            o = _dot(a, v) + _dot(q, st.astype(BF16)) * qd_ref[h]
            kdec = (k.astype(F32) * kd_ref[h]).astype(BF16)
            st_ref[0, h] = st * gl_ref[h] + _dot_tn(kdec, v)
            mu = jnp.mean(o, -1, keepdims=True)
            oc = o - mu
            var = jnp.mean(oc * oc, -1, keepdims=True)
            on = oc * lax.rsqrt(var + LN_EPS)
            g = g_ref[rows, sl].astype(F32)
            o_ref[rows, sl] = (g * jax.nn.sigmoid(g) * on).astype(BF16)


def _retention(rq, rk, rv, rg, state0, B, S):
    L = min(S, 256)
    lt = min(S, 512)
    nj = S // lt
    dm, qd, kd, gl = _retention_tables(L)
    row = lambda b, j: (b * nj + j, 0)
    c3 = lambda b, j: (0, 0, 0)
    st_spec = pl.BlockSpec((1, RET_HEADS, RET_DK, RET_DV), lambda b, j: (b, 0, 0, 0))
    return pl.pallas_call(
        functools.partial(_retention_kernel, L=L, nchunk=lt // L),
        grid=(B, nj),
        in_specs=[pl.BlockSpec((lt, RET_W), row)] * 4 + [
            st_spec,
            pl.BlockSpec(dm.shape, c3), pl.BlockSpec(qd.shape, c3),
            pl.BlockSpec(kd.shape, c3), pl.BlockSpec(gl.shape, c3),
        ],
        out_specs=[pl.BlockSpec((lt, RET_W), row), st_spec],
        out_shape=[jax.ShapeDtypeStruct((B * S, RET_W), BF16),
                   jax.ShapeDtypeStruct((B, RET_HEADS, RET_DK, RET_DV), F32)],
        compiler_params=_cparams(("parallel", "arbitrary")),
        name="retention",
    )(rq, rk, rv, rg, state0, dm, qd, kd, gl)


def _query_t(q, tq):
    q = q.astype(F32)
    if tq < LANES:
        q = jnp.concatenate([q, jnp.zeros((LANES - tq, LANES), F32)], 0)
    return q.T


def _flash_t(streams, qi, tq, tk, sk, sk_valid, causal, m_ref, l_ref, acc_ref):
    for s in range(len(streams)):
        m_ref[s] = jnp.full(m_ref.shape[1:], NEG_INF, F32)
        l_ref[s] = jnp.zeros(l_ref.shape[1:], F32)
        acc_ref[s] = jnp.zeros(acc_ref.shape[1:], F32)

    def step(start, size, diagonal=False, valid=None):
        for s, (q_t, k_at, vt_at) in enumerate(streams):
            st = _dot(k_at(start, size), q_t)
            if diagonal:
                kc = lax.broadcasted_iota(jnp.int32, st.shape, 0) // CHUNK
                qc = lax.broadcasted_iota(jnp.int32, st.shape, 1) // CHUNK
                st = jnp.where(kc <= qc, st, NEG_INF)
            if valid is not None:
                st = jnp.where(lax.broadcasted_iota(jnp.int32, st.shape, 0) < valid, st, NEG_INF)
            m_old = m_ref[s]
            m_new = jnp.maximum(m_old, jnp.max(st, 0, keepdims=True))
            p = jnp.exp2(st - m_new)
            alpha = jnp.exp2(m_old - m_new)
            l_ref[s] = alpha * l_ref[s] + jnp.sum(p, 0, keepdims=True)
            acc_ref[s] = acc_ref[s] * alpha + _dot(vt_at(start, size), p.astype(BF16))
            m_ref[s] = m_new

    if causal:
        n_full = qi * (tq // tk)

        def body(j, c):
            step(pl.multiple_of(2 * j * tk, tk), tk)
            step(pl.multiple_of((2 * j + 1) * tk, tk), tk)
            return c

        lax.fori_loop(0, n_full // 2, body, 0)

        @pl.when(n_full % 2 == 1)
        def _():
            step(pl.multiple_of((n_full - 1) * tk, tk), tk)

        step(pl.multiple_of(qi * tq, tq), tq, diagonal=True)
    else:
        for j in range(sk // tk):
            last_valid = sk_valid - j * tk
            step(j * tk, tk, valid=last_valid if last_valid < tk else None)
    return [acc_ref[s] / l_ref[s] for s in range(len(streams))]


def _attn_scratch(n_streams, dv, tq):
    tqp = max(tq, LANES)
    return [pltpu.VMEM((n_streams, 1, tqp), F32), pltpu.VMEM((n_streams, 1, tqp), F32),
            pltpu.VMEM((n_streams, dv, tqp), F32)]


def _mla_attn_kernel(q_ref, k_ref, vt_ref, o_ref, m_ref, l_ref, acc_ref, *, tq, tk, sk, sk_valid, causal):
    streams = []
    for hh in range(2):
        sl = slice(hh * LANES, (hh + 1) * LANES)
        vrows = slice(hh * MLA_V, (hh + 1) * MLA_V)
        k_at = lambda start, n, sl=sl: k_ref[pl.ds(start, n), sl]
        vt_at = lambda start, n, vrows=vrows: vt_ref[0, vrows, pl.ds(start, n)]
        streams.append((_query_t(q_ref[:, sl], tq).astype(BF16), k_at, vt_at))
    outs = _flash_t(streams, pl.program_id(2), tq, tk, sk, sk_valid, causal, m_ref, l_ref, acc_ref)
    o_ref[...] = jnp.concatenate(outs, 0).T[:tq].astype(BF16)


def _mla_attn(q, k, vt, B, sq, sk, sk_valid, tq, tk, causal):
    nq = sq // tq
    npair = MLA_HEADS // 2
    return pl.pallas_call(
        functools.partial(_mla_attn_kernel, tq=tq, tk=tk, sk=sk, sk_valid=sk_valid, causal=causal),
        grid=(B, npair, nq),
        in_specs=[
            pl.BlockSpec((tq, 2 * LANES), lambda b, p, i: (b * nq + i, p)),
            pl.BlockSpec((sk, 2 * LANES), lambda b, p, i: (b, p)),
            pl.BlockSpec((1, 2 * MLA_V, sk), lambda b, p, i: (b, p, 0)),
        ],
        out_specs=pl.BlockSpec((tq, LANES), lambda b, p, i: (b * nq + i, p)),
        out_shape=jax.ShapeDtypeStruct((B * sq, MLA_HEADS * MLA_V), BF16),
        scratch_shapes=_attn_scratch(2, MLA_V, tq),
        compiler_params=_cparams(("parallel", "parallel", "arbitrary")),
        name="mla_attn",
    )(q, k, vt)


def _diff_attn_kernel(lam_ref, gn_ref, q_ref, k_ref, vt_ref, o_ref, m_ref, l_ref, acc_ref,
                      *, tq, tk, sk, sk_valid, causal, lam_init):
    k_at = lambda start, n: k_ref[pl.ds(start, n), :]
    vt_at = lambda start, n: vt_ref[0, :, pl.ds(start, n)]
    q_t = _query_t(q_ref[...], tq)
    feat = lax.broadcasted_iota(jnp.int32, q_t.shape, 0)
    zero = jnp.zeros_like(q_t)
    q1 = jnp.where(feat < DIFF_HD, q_t, zero).astype(BF16)
    q2 = jnp.where(feat < DIFF_HD, zero, q_t).astype(BF16)
    o1, o2 = _flash_t([(q1, k_at, vt_at), (q2, k_at, vt_at)], pl.program_id(2), tq, tk, sk, sk_valid,
                      causal, m_ref, l_ref, acc_ref)
    lv = lam_ref[...]
    lam = (jnp.exp(jnp.sum(lv[0:1] * lv[1:2], -1, keepdims=True))
           - jnp.exp(jnp.sum(lv[2:3] * lv[3:4], -1, keepdims=True)) + lam_init)
    o = (o1 - lam * o2).T[:tq]
    o_ref[...] = (_rms_norm(o, gn_ref[...]) * (1.0 - lam_init)).astype(BF16)


def _diff_attn(lam_vecs, gn, q, k, vt, B, sq, sk, sk_valid, tq, tk, causal, lam_init):
    nq = sq // tq
    const = lambda b, h, i: (0, 0)
    return pl.pallas_call(
        functools.partial(_diff_attn_kernel, tq=tq, tk=tk, sk=sk, sk_valid=sk_valid, causal=causal,
                          lam_init=lam_init),
        grid=(B, DIFF_HEADS, nq),
        in_specs=[
            pl.BlockSpec(lam_vecs.shape, const), pl.BlockSpec(gn.shape, const),
            pl.BlockSpec((tq, LANES), lambda b, h, i: (b * nq + i, h)),
            pl.BlockSpec((sk, LANES), lambda b, h, i: (b, h)),
            pl.BlockSpec((1, DIFF_V, sk), lambda b, h, i: (b, h, 0)),
        ],
        out_specs=pl.BlockSpec((tq, LANES), lambda b, h, i: (b * nq + i, h)),
        out_shape=jax.ShapeDtypeStruct((B * sq, DIFF_HEADS * DIFF_V), BF16),
        scratch_shapes=_attn_scratch(2, DIFF_V, tq),
        compiler_params=_cparams(("parallel", "parallel", "arbitrary")),
        name="diff_attn",
    )(lam_vecs, gn, q, k, vt)


def _out_proj_kernel(*refs, n_in):
    a_refs = refs[:n_in]
    (w_ref, x_ref, g_ref, b_ref, wr_ref, bias_ref, tri_ref,
     o_ref, gate_ref, route_ref, cnt_out_ref, gt_ref, oh_ref, cnt_ref) = refs[n_in:]
    y = None
    off = 0
    for a_ref in a_refs:
        width = a_ref.shape[1]
        part = _dot(a_ref[...], w_ref[off:off + width, :])
        y = part if y is None else y + part
        off += width
    x1 = _layer_norm(DN_ALPHA * x_ref[...] + y, g_ref[...], b_ref[...])
    o_ref[...] = x1
    _route(x1, wr_ref, bias_ref, tri_ref, gate_ref, route_ref, cnt_out_ref, gt_ref, oh_ref, cnt_ref)


def _out_proj(acts, w, x, g, b, wr_t, bias, tm):
    T, D = x.shape
    tri = jnp.asarray(np.triu(np.ones((tm, tm), np.float32), 1)).astype(BF16)
    row = lambda i: (i, 0)
    const = lambda i: (0, 0)
    return pl.pallas_call(
        functools.partial(_out_proj_kernel, n_in=len(acts)),
        grid=(T // tm,),
        in_specs=[pl.BlockSpec((tm, a.shape[1]), row) for a in acts] + [
            pl.BlockSpec(w.shape, const), pl.BlockSpec((tm, D), row),
            pl.BlockSpec(g.shape, const), pl.BlockSpec(b.shape, const),
            pl.BlockSpec(wr_t.shape, const), pl.BlockSpec(bias.shape, const), pl.BlockSpec(tri.shape, const),
        ],
        out_specs=[pl.BlockSpec((tm, D), row), pl.BlockSpec((tm, LANES), row),
                   pl.BlockSpec((8, tm), lambda i: (0, i)), pl.BlockSpec((SEG_ROWS, LANES), const)],
        out_shape=[jax.ShapeDtypeStruct((T, D), F32), jax.ShapeDtypeStruct((T, LANES), F32),
                   jax.ShapeDtypeStruct((8, T), jnp.int32), jax.ShapeDtypeStruct((SEG_ROWS, LANES), F32)],
        scratch_shapes=[pltpu.VMEM((LANES, tm), F32), pltpu.VMEM((SEG_ROWS, tm), F32),
                        pltpu.VMEM((SEG_ROWS, LANES), F32)],
        compiler_params=_cparams(("arbitrary",)),
        name="out_proj",
    )(*acts, w, x, g, b, wr_t, bias, tri)


def _odd_in_kernel(x_ref, w_ref, tab_ref, q_ref, kt_ref, kb_ref, vf_ref, vt_ref):
    xb = x_ref[...].astype(BF16)
    W = DIFF_HEADS * 2 * DIFF_HD
    hq = _dot(xb, w_ref[:, 0:W])
    hk = _dot(xb, w_ref[:, W:2 * W])
    for h in range(W // LANES):
        sl = slice(h * LANES, (h + 1) * LANES)
        q_ref[:, sl] = _rope(hq[:, sl], tab_ref[0], tab_ref[1], tab_ref[2], DIFF_HD // 2).astype(BF16)
        kk = _rope(hk[:, sl], tab_ref[3], tab_ref[4], tab_ref[5], DIFF_HD // 2)
        kt_ref[0, sl, :] = kk.T
        kb_ref[:, sl] = kk.astype(BF16)
    hv = _dot(xb, w_ref[:, 2 * W:])
    vf_ref[...] = hv
    vt_ref[0] = hv.T.astype(BF16)


def _odd_in(x, w, tabs, S, tm):
    T, D = x.shape
    W = DIFF_HEADS * 2 * DIFF_HD
    nt = tabs.shape[1] // tm
    row = lambda i: (i, 0)
    blk = pl.BlockSpec((tm, W), row)
    vt_spec, vt_shape = _vt_layout(T, S, tm, DIFF_HEADS * DIFF_V)
    kt_spec, kt_shape = _vt_layout(T, S, tm, W, F32)
    return pl.pallas_call(
        _odd_in_kernel,
        grid=(T // tm,),
        in_specs=[pl.BlockSpec((tm, D), row), pl.BlockSpec(w.shape, lambda i: (0, 0)),
                  pl.BlockSpec((tabs.shape[0], tm, LANES), lambda i: (0, i % nt, 0))],
        out_specs=[blk, kt_spec, blk, blk, vt_spec],
        out_shape=[jax.ShapeDtypeStruct((T, W), BF16), kt_shape,
                   jax.ShapeDtypeStruct((T, W), BF16), jax.ShapeDtypeStruct((T, W), F32), vt_shape],
        compiler_params=_cparams(("parallel",)),
        name="odd_in",
    )(x, w, tabs)


def _route(x, wr_ref, bias_ref, tri_ref, g_ref, route_ref, cnt_out_ref, gt_ref, oh_ref, cnt_ref):
    tm = x.shape[0]
    logits = _dot_nt(wr_ref[...], x.astype(BF16))
    sc = jax.nn.sigmoid(logits)
    sel = sc + bias_ref[...]
    r = [sel[e:e + 1, :] for e in range(N_EXPERTS)]
    s = [sc[e:e + 1, :] for e in range(N_EXPERTS)]
    grp = []
    for g in range(N_GROUPS):
        a, b, c, d = r[4 * g:4 * g + 4]
        top2 = jnp.maximum(jnp.maximum(jnp.maximum(a + b, a + c), jnp.maximum(a + d, b + c)),
                           jnp.maximum(b + d, c + d))
        grp.append(top2)
    best = jnp.maximum(jnp.maximum(grp[0], grp[1]), jnp.maximum(grp[2], grp[3]))
    taken = jnp.zeros((1, tm), jnp.bool_)
    chosen = []
    for g in range(N_GROUPS):
        win = jnp.logical_and(grp[g] == best, jnp.logical_not(taken))
        chosen.append(win)
        taken = jnp.logical_or(taken, win)
    picked = []
    for e in range(N_EXPERTS):
        g = e // EXPERTS_PER_GROUP
        rank = jnp.zeros((1, tm), F32)
        for k in range(4 * g, 4 * g + 4):
            if k < e:
                rank = rank + (r[k] >= r[e]).astype(F32)
            elif k > e:
                rank = rank + (r[k] > r[e]).astype(F32)
        picked.append(jnp.logical_and(chosen[g], rank < 2.0))
    w = [jnp.where(picked[e], s[e], 0.0) for e in range(N_EXPERTS)]
    denom = w[0]
    for e in range(1, N_EXPERTS):
        denom = denom + w[e]
    gt_ref[...] = jnp.zeros_like(gt_ref)
    for e in range(N_EXPERTS):
        gt_ref[e:e + 1, :] = w[e] / denom
    g_ref[...] = gt_ref[...].T

    @pl.when(pl.program_id(0) == 0)
    def _():
        cnt_ref[...] = jnp.zeros_like(cnt_ref)

    oh_ref[...] = jnp.zeros_like(oh_ref)
    for g in range(N_GROUPS):
        for p, (a, b) in enumerate(EXPERT_PAIRS):
            both = jnp.logical_and(picked[4 * g + a], picked[4 * g + b])
            oh_ref[g * N_PAIRS + p:g * N_PAIRS + p + 1, :] = both.astype(F32)
    oh = oh_ref[...]
    before = _dot(oh.astype(BF16), tri_ref[...])
    base = cnt_ref[:, 0:1]
    rank = jnp.sum(oh * (base + before), 0, keepdims=True)
    sid = jnp.sum(oh * lax.broadcasted_iota(jnp.int32, oh.shape, 0).astype(F32), 0, keepdims=True)
    row = lax.broadcasted_iota(jnp.int32, route_ref.shape, 0)
    route_ref[...] = jnp.where(row == 0, sid, jnp.where(row == 1, rank, 0.0)).astype(jnp.int32)
    cnt_ref[...] = cnt_ref[...] + jnp.sum(oh, 1, keepdims=True)
    cnt_out_ref[...] = cnt_ref[...]


def _moe_dense_kernel(x_ref, gate_ref, wg_ref, wu_ref, wd_ref, g_ref, b_ref, o_ref, xb_ref, acc_ref):
    e = pl.program_id(1)

    @pl.when(e == 0)
    def _():
        xb_ref[...] = x_ref[...].astype(BF16)
        acc_ref[...] = jnp.zeros_like(acc_ref)

    xb = xb_ref[...]
    h = jax.nn.silu(_dot(xb, wg_ref[0, 0].astype(BF16))) * _dot(xb, wu_ref[0, 0].astype(BF16))
    y = _dot(h.astype(BF16), wd_ref[0, 0].astype(BF16))
    gates = gate_ref[...]
    lane = lax.broadcasted_iota(jnp.int32, gates.shape, 1)
    gcol = jnp.sum(jnp.where(lane == e, gates, 0.0), -1, keepdims=True)
    acc_ref[...] += gcol * y

    @pl.when(e == pl.num_programs(1) - 1)
    def _():
        o_ref[...] = _layer_norm(DN_ALPHA * x_ref[...] + acc_ref[...], g_ref[...], b_ref[...])


def _moe_dense(x, gates, wg, wu, wd, l, g, b, tm):
    T, D = x.shape
    _, E, _, H = wg.shape
    row = lambda i, e: (i, 0)
    const = lambda i, e: (0, 0)
    wsel = lambda i, e: (l, e, 0, 0)
    return pl.pallas_call(
        _moe_dense_kernel,
        grid=(T // tm, E),
        in_specs=[
            pl.BlockSpec((tm, D), row), pl.BlockSpec((tm, LANES), row),
            pl.BlockSpec((1, 1, D, H), wsel), pl.BlockSpec((1, 1, D, H), wsel),
            pl.BlockSpec((1, 1, H, D), wsel),
            pl.BlockSpec(g.shape, const), pl.BlockSpec(b.shape, const),
        ],
        out_specs=pl.BlockSpec((tm, D), row),
        out_shape=jax.ShapeDtypeStruct((T, D), F32),
        scratch_shapes=[pltpu.VMEM((tm, D), BF16), pltpu.VMEM((tm, D), F32)],
        compiler_params=_cparams(("parallel", "arbitrary")),
        name="moe_dense",
    )(x, gates, wg, wu, wd, g, b)


SC_CORES = 2
SC_SUBCORES = 16
SC_WORKERS = SC_CORES * SC_SUBCORES
SC_TILE_BYTES = 384 * 1024
SPARSE_ROW_TILE = 512
SPARSE_MIN_TOKENS = 4096


def _sc_ring(per_w, row_bytes):
    for ch, nbuf in ((16, 4), (16, 2), (8, 2)):
        if per_w % (ch * nbuf) == 0 and ch * nbuf * row_bytes <= SC_TILE_BYTES:
            return ch, nbuf
    raise ValueError(f"no SparseCore gather ring for {per_w} rows of {row_bytes} bytes per subcore")


def _sc_gather(table, idx):
    R = idx.shape[0]
    D = table.shape[1]
    per_w = R // SC_WORKERS
    assert per_w * SC_WORKERS == R and per_w % 8 == 0
    ch, nbuf = _sc_ring(per_w, D * table.dtype.itemsize)
    nchunk = per_w // ch
    mesh = plsc.VectorSubcoreMesh(core_axis_name="c", subcore_axis_name="s")

    @functools.partial(
        pl.kernel, mesh=mesh,
        out_type=jax.ShapeDtypeStruct((R, D), table.dtype),
        scratch_types=[pltpu.VMEM((per_w,), jnp.int32), pltpu.VMEM((nbuf, ch, D), table.dtype),
                       pltpu.SemaphoreType.DMA((nbuf,)), pltpu.SemaphoreType.DMA((nbuf,))],
    )
    def gather_kernel(table_hbm, idx_hbm, out_hbm, idx_v, rows_v, gsem, wsem):
        base = (lax.axis_index("s") * SC_CORES + lax.axis_index("c")) * per_w
        pltpu.sync_copy(idx_hbm.at[pl.ds(base, per_w)], idx_v)

        def gather(c, b):
            return pltpu.make_async_copy(table_hbm.at[idx_v.at[pl.ds(c * ch, ch)]], rows_v.at[b], gsem.at[b])

        def write(c, b):
            return pltpu.make_async_copy(rows_v.at[b], out_hbm.at[pl.ds(base + c * ch, ch)], wsem.at[b])

        for b in range(nbuf - 1):
            gather(b, b).start()

        @pl.loop(0, nchunk, step=nbuf)
        def _(c):
            for b in range(nbuf):
                cc = c + b
                gather(cc, b).wait()
                write(cc, b).start()
                pb = (b - 1) % nbuf

                @pl.when(cc + nbuf - 1 < nchunk)
                def _():
                    @pl.when(cc >= 1)
                    def _():
                        write(cc - 1, pb).wait()

                    gather(cc + nbuf - 1, pb).start()

        for b in range(nbuf):
            write(nchunk - nbuf + b, b).wait()

    return gather_kernel(table, idx)


def _moe_group_kernel(te_ref, nv_ref, x_ref, gate_ref, *refs):
    w_refs, (g_ref, b_ref, o_ref) = refs[:3 * TOP_K], refs[3 * TOP_K:]
    j = pl.program_id(0)
    live = j < nv_ref[0]

    @pl.when(live)
    def _():
        x = x_ref[...]
        xb = x.astype(BF16)
        gates = gate_ref[...]
        lane = lax.broadcasted_iota(jnp.int32, gates.shape, 1)
        acc = None
        for k in range(TOP_K):
            wg_ref, wu_ref, wd_ref = w_refs[3 * k:3 * k + 3]
            h = jax.nn.silu(_dot(xb, wg_ref[0, 0].astype(BF16))) * _dot(xb, wu_ref[0, 0].astype(BF16))
            y = _dot(h.astype(BF16), wd_ref[0, 0].astype(BF16))
            gcol = jnp.sum(jnp.where(lane == te_ref[TOP_K * j + k], gates, 0.0), -1, keepdims=True)
            acc = gcol * y if acc is None else acc + gcol * y
        o_ref[...] = _layer_norm(DN_ALPHA * x + acc, g_ref[...], b_ref[...])

    @pl.when(jnp.logical_not(live))
    def _():
        o_ref[...] = jnp.zeros_like(o_ref)


def _moe_group(tile_experts, n_valid, xs, gs, wg, wu, wd, l, g, b, tm):
    R, D = xs.shape
    H = wg.shape[3]
    row = lambda j, te, nv: (j, 0)
    const = lambda j, te, nv: (0, 0)
    w_specs, w_args = [], []
    for k in range(TOP_K):
        wsel = lambda j, te, nv, k=k: (l, te[TOP_K * j + k], 0, 0)
        w_specs += [pl.BlockSpec((1, 1, D, H), wsel), pl.BlockSpec((1, 1, D, H), wsel),
                    pl.BlockSpec((1, 1, H, D), wsel)]
        w_args += [wg, wu, wd]
    grid_spec = pltpu.PrefetchScalarGridSpec(
        num_scalar_prefetch=2,
        grid=(R // tm,),
        in_specs=[pl.BlockSpec((tm, D), row), pl.BlockSpec((tm, LANES), row)] + w_specs + [
            pl.BlockSpec(g.shape, const), pl.BlockSpec(b.shape, const)],
        out_specs=pl.BlockSpec((tm, D), row),
    )
    return pl.pallas_call(
        _moe_group_kernel,
        grid_spec=grid_spec,
        out_shape=jax.ShapeDtypeStruct((R, D), F32),
        compiler_params=_cparams(("arbitrary",)),
        name="moe_group",
    )(tile_experts, n_valid, xs, gs, *w_args, g, b)


def _sc_scatter(x, dest, pad_dest, n_rows):
    T, D = x.shape
    P = pad_dest.shape[0]
    per_w, pad_w = T // SC_WORKERS, P // SC_WORKERS
    assert per_w * SC_WORKERS == T and pad_w * SC_WORKERS == P and T + P == n_rows
    ch, nbuf = _sc_ring(per_w, D * x.dtype.itemsize)
    assert pad_w % ch == 0
    nchunk, npad = per_w // ch, pad_w // ch
    mesh = plsc.VectorSubcoreMesh(core_axis_name="c", subcore_axis_name="s")

    @functools.partial(
        pl.kernel, mesh=mesh,
        out_type=jax.ShapeDtypeStruct((n_rows, D), x.dtype),
        scratch_types=[pltpu.VMEM((nchunk, ch), jnp.int32), pltpu.VMEM((npad, ch), jnp.int32),
                       pltpu.VMEM((nbuf, ch, D), x.dtype), pltpu.VMEM((ch, D), x.dtype),
                       pltpu.SemaphoreType.DMA((nbuf,)), pltpu.SemaphoreType.DMA((nbuf,)),
                       pltpu.SemaphoreType.DMA],
    )
    def scatter_kernel(x_hbm, dest_hbm, pad_hbm, zero_hbm, out_hbm, dest_v, pad_v, rows_v, zero_v,
                       rsem, wsem, zsem):
        wid = lax.axis_index("s") * SC_CORES + lax.axis_index("c")
        base = wid * per_w
        pltpu.sync_copy(dest_hbm.at[wid], dest_v)
        pltpu.sync_copy(pad_hbm.at[wid], pad_v)
        pltpu.sync_copy(zero_hbm, zero_v)

        def read(c, b):
            return pltpu.make_async_copy(x_hbm.at[pl.ds(base + c * ch, ch)], rows_v.at[b], rsem.at[b])

        def write(c, b):
            return pltpu.make_async_copy(rows_v.at[b], out_hbm.at[dest_v.at[c]], wsem.at[b])

        def write_zero(c):
            return pltpu.make_async_copy(zero_v, out_hbm.at[pad_v.at[c]], zsem)

        for c in range(npad):
            write_zero(c).start()
        for b in range(nbuf - 1):
            read(b, b).start()

        @pl.loop(0, nchunk, step=nbuf)
        def _(c):
            for b in range(nbuf):
                cc = c + b
                read(cc, b).wait()
                write(cc, b).start()
                pb = (b - 1) % nbuf

                @pl.when(cc + nbuf - 1 < nchunk)
                def _():
                    @pl.when(cc >= 1)
                    def _():
                        write(cc - 1, pb).wait()

                    read(cc + nbuf - 1, pb).start()

        for b in range(nbuf):
            write(nchunk - nbuf + b, b).wait()
        for c in range(npad):
            write_zero(c).wait()

    return scatter_kernel(x, dest.reshape(SC_WORKERS, nchunk, ch), pad_dest.reshape(SC_WORKERS, npad, ch),
                          jnp.zeros((ch, D), x.dtype))


def _moe_sparse(x, gates, route, counts, wg, wu, wd, l, g, b):
    T = x.shape[0]
    tm = SPARSE_ROW_TILE
    n_tiles = T // tm + N_SEG
    n_rows = n_tiles * tm
    sid, rank = route[0], route[1]
    cnt = counts[:N_SEG, 0].astype(jnp.int32)
    tiles = (cnt + tm - 1) // tm
    ends = jnp.cumsum(tiles)
    starts = (ends - tiles) * tm
    pick = lambda which, vals: sum(jnp.where(which == k, vals[k], 0) for k in range(N_SEG))
    dest = rank + pick(sid, starts)
    pad_cnt = tiles * tm - cnt
    pad_end = jnp.cumsum(pad_cnt)
    p = jnp.arange(n_rows - T, dtype=jnp.int32)
    seg = sum((p >= pad_end[k]).astype(jnp.int32) for k in range(N_SEG))
    pad_dest = jnp.where(seg < N_SEG,
                         pick(seg, starts + cnt) + p - pick(seg, pad_end - pad_cnt),
                         ends[N_SEG - 1] * tm + p - pad_end[N_SEG - 1])
    tile_seg = jnp.minimum(jnp.searchsorted(ends, jnp.arange(n_tiles, dtype=jnp.int32), side="right"),
                           N_SEG - 1).astype(jnp.int32)
    pair = jnp.asarray(np.asarray(EXPERT_PAIRS, np.int32))
    tile_experts = (tile_seg // N_PAIRS * EXPERTS_PER_GROUP)[:, None] + pair[tile_seg % N_PAIRS]
    xs = _sc_scatter(x, dest, pad_dest, n_rows)
    gs = _sc_scatter(gates, dest, pad_dest, n_rows)
    ys = _moe_group(tile_experts.reshape(-1), ends[N_SEG - 1:], xs, gs, wg, wu, wd, l, g, b, tm)
    return _sc_gather(ys, dest)


def _prep_weights(w_in_even, w_uq, w_ukv, w_out_even, w_in_odd, w_out_odd, w_router,
                  w_expert_gate, w_expert_up, w_expert_down):
    d = w_in_even.shape[1]
    n_main = 4 * RET_W + MLA_Q_RANK + MLA_KV_RANK
    w_in = w_in_even[0]
    kr_cols = jnp.pad(w_in[:, n_main:], ((0, 0), (MLA_NOPE, LANES - MLA_NOPE - MLA_ROPE)))
    w_a = jnp.concatenate([w_in[:, :n_main], kr_cols], 1).astype(BF16)
    qd = MLA_NOPE + MLA_ROPE
    wq = jnp.pad(w_uq[0].reshape(MLA_Q_RANK, MLA_HEADS, qd), ((0, 0), (0, 0), (0, LANES - qd)))
    wq = wq.reshape(MLA_Q_RANK, MLA_PAD).astype(BF16)
    wkv = w_ukv[0].reshape(MLA_KV_RANK, MLA_HEADS, MLA_NOPE + MLA_V)
    wk = jnp.pad(wkv[:, :, :MLA_NOPE], ((0, 0), (0, 0), (0, LANES - MLA_NOPE)))
    wk = wk.reshape(MLA_KV_RANK, MLA_PAD).astype(BF16)
    wvt = wkv[:, :, MLA_NOPE:].reshape(MLA_KV_RANK, MLA_HEADS * MLA_V).T.astype(BF16)
    e_np = np.zeros((MLA_ROPE, MLA_HEADS, LANES), np.float32)
    for j in range(MLA_ROPE):
        e_np[j, :, MLA_NOPE + j] = 1.0
    e_mat = jnp.asarray(e_np.reshape(MLA_ROPE, MLA_PAD)).astype(BF16)
    return dict(
        w_a=w_a, wq=wq, wk=wk, wvt=wvt, e_mat=e_mat,
        w_out_even=w_out_even[0].astype(BF16), w_in_odd=w_in_odd[0].astype(BF16),
        w_out_odd=w_out_odd[0].astype(BF16), wr_t=w_router.T.astype(BF16),
        wg=w_expert_gate, wu=w_expert_up, wd=w_expert_down,
    )


def _moe(routed, wts, l, ln_g, ln_b, tm):
    x, gates, route, counts = routed
    g, b = ln_g[l, 1][None], ln_b[l, 1][None]
    if x.shape[0] >= SPARSE_MIN_TOKENS:
        return _moe_sparse(x, gates, route, counts, wts["wg"], wts["wu"], wts["wd"], l, g, b)
    return _moe_dense(x, gates, wts["wg"], wts["wu"], wts["wd"], l, g, b, tm)


def _trunk(x3, pos0, past, wts, prm):
    B, S, D = x3.shape
    T = B * S
    x = x3.reshape(T, D)
    tm = min(T, 512)
    rep = max(tm // S, 1)
    pos = np.tile(pos0 + np.arange(S), rep)
    ln_g, ln_b = prm["ln_g"], prm["ln_b"]

    rq, rk, rv, rg, q, lat, kr = _even_in(x, wts["w_a"], wts["wq"], prm["gq"], prm["gkv"], _even_tables(pos), tm)
    causal = past is None
    if causal:
        state0 = jnp.zeros((B, RET_HEADS, RET_DK, RET_DV), F32)
        lat_all, kr_all, sk, skp = lat, kr, S, S
        tq = tk = min(S, 512)
    else:
        state0 = past["state"]
        sk = past["lat"].shape[1] + S
        skp = -(-sk // LANES) * LANES
        padk = lambda parts: jnp.concatenate(
            parts + [jnp.zeros((B, skp - sk, parts[0].shape[2]), parts[0].dtype)], 1).reshape(B * skp, -1)
        lat_all = padk([past["lat"], lat.reshape(B, S, -1)])
        kr_all = padk([past["kr"], kr.reshape(B, S, -1)])
        tq, tk = S, skp
    ret_out, ret_state = _retention(rq, rk, rv, rg, state0, B, S)
    tkv = 512 if (B * skp) % 512 == 0 else skp
    k_mla, vt_mla = _kv_up(lat_all, kr_all, wts["wk"], wts["wvt"], wts["e_mat"], skp, tkv)
    if vt_mla.shape[0] != B:
        vt_mla = vt_mla.reshape(-1, B, skp).transpose(1, 0, 2)
    mla_out = _mla_attn(q, k_mla, vt_mla, B, S, skp, sk, tq, tk, causal)
    routed = _out_proj([ret_out, mla_out], wts["w_out_even"], x, ln_g[0, 0][None], ln_b[0, 0][None],
                       wts["wr_t"], prm["bias"], tm)
    x = _moe(routed, wts, 0, ln_g, ln_b, tm)

    qd, kt, kb, vf, vt = _odd_in(x, wts["w_in_odd"], _odd_tables(pos), S, tm)
    if vt.shape[0] != B:
        vt = vt.reshape(-1, B, S).transpose(1, 0, 2)
        kt = kt.reshape(-1, B, S).transpose(1, 0, 2)
    kf = kt.reshape(1, B, 2 * DIFF_HEADS, DIFF_HD, S).transpose(0, 1, 4, 2, 3)
    if causal:
        k_all, vt_all = kb, vt
    else:
        k_all = padk([past["dk"], kb.reshape(B, S, -1)])
        vt_all = jnp.concatenate([past["dv"].transpose(0, 2, 1), vt,
                                  jnp.zeros((B, vt.shape[1], skp - sk), BF16)], 2)
    lam_init = 0.8 - 0.6 * math.exp(-0.3 * 1)
    d_out = _diff_attn(prm["lam"], prm["gn"], qd, k_all, vt_all, B, S, skp, sk, tq, tk, causal, lam_init)
    routed = _out_proj([d_out], wts["w_out_odd"], x, ln_g[1, 0][None], ln_b[1, 0][None],
                       wts["wr_t"], prm["bias"], tm)
    x = _moe(routed, wts, 1, ln_g, ln_b, tm)

    return (x.reshape(B, S, D), ret_state[None], lat.reshape(1, B, S, -1), kr.reshape(1, B, S, -1),
            kf, vf.reshape(1, B, S, DIFF_HEADS, DIFF_V))


def kernel(x_prompt, x_sample, state_ret, cache_mla_latent, cache_mla_krope, cache_diff_k, cache_diff_v,
           w_in_even, w_uq, w_ukv, g_qnorm, g_kvnorm, w_out_even,
           w_in_odd, lambda_q1, lambda_k1, lambda_q2, lambda_k2, g_diff_norm, w_out_odd,
           ln_g, ln_b, w_router, router_bias, w_expert_gate, w_expert_up, w_expert_down):
    wts = _prep_weights(w_in_even, w_uq, w_ukv, w_out_even, w_in_odd, w_out_odd, w_router,
                        w_expert_gate, w_expert_up, w_expert_down)
    prm = dict(
        gq=g_qnorm[0][None].astype(F32), gkv=g_kvnorm[0][None].astype(F32),
        lam=jnp.stack([lambda_q1[0], lambda_k1[0], lambda_q2[0], lambda_k2[0]]).astype(F32),
        gn=g_diff_norm[0][None].astype(F32), bias=router_bias.reshape(N_EXPERTS, 1).astype(F32),
        ln_g=ln_g.astype(F32), ln_b=ln_b.astype(F32),
    )
    past_len = cache_mla_latent.shape[2]
    db = x_sample.shape[0]
    past = dict(
        state=state_ret[0].astype(F32), lat=cache_mla_latent[0], kr=cache_mla_krope[0],
        dk=cache_diff_k[0].reshape(db, past_len, -1).astype(BF16),
        dv=cache_diff_v[0].reshape(db, past_len, -1).astype(BF16),
    )
    outs_p = _trunk(x_prompt, 0, None, wts, prm)
    outs_s = _trunk(x_sample, past_len, past, wts, prm)
    return (outs_p[0], outs_s[0]) + outs_p[1:] + outs_s[1:]
```

```python
import functools
import math

import numpy as np
import jax
import jax.numpy as jnp
from jax import lax
from jax.experimental import pallas as pl
from jax.experimental.pallas import tpu as pltpu
from jax.experimental.pallas import tpu_sc as plsc

F32 = jnp.float32
BF16 = jnp.bfloat16

CHUNK = 64
ROPE_THETA = 10000.0
NEG_INF = -1e30
LN_EPS = 1e-5
NORM_EPS = 1e-6
DEPTH = 2
DN_ALPHA = (2.0 * DEPTH) ** 0.25
RET_HEADS = 4
RET_DK = 128
RET_DV = 128
RET_LOG_GAMMA = tuple(math.log(1.0 - 2.0 ** (-5 - h)) for h in range(RET_HEADS))
MLA_HEADS = 8
MLA_Q_RANK = 384
MLA_KV_RANK = 256
MLA_NOPE = 64
MLA_ROPE = 32
MLA_V = 64
DIFF_HEADS = 8
DIFF_HD = 64
DIFF_V = 128
N_EXPERTS = 16
N_GROUPS = 4
EXPERTS_PER_GROUP = 4
TOP_K = 2
EXPERT_PAIRS = tuple((a, b) for a in range(EXPERTS_PER_GROUP) for b in range(a + 1, EXPERTS_PER_GROUP))
N_PAIRS = len(EXPERT_PAIRS)
N_SEG = N_GROUPS * N_PAIRS
SEG_ROWS = 32
LOG2E = math.log2(math.e)

LANES = 128
RET_W = RET_HEADS * RET_DK
MLA_PAD = MLA_HEADS * LANES
VMEM_LIMIT = 56 * 1024 * 1024


def _cparams(sem):
    return pltpu.CompilerParams(dimension_semantics=sem, vmem_limit_bytes=VMEM_LIMIT)


def _rope_tables(pos, d, group, offset, scale):
    pos = np.asarray(pos, np.float64)
    half = d // 2
    inv = 1.0 / (ROPE_THETA ** (np.arange(0, d, 2, dtype=np.float64) / d))
    ang = pos[:, None] * inv[None, :]
    cos = np.full((pos.shape[0], LANES), scale, np.float64)
    s_lo = np.zeros((pos.shape[0], LANES), np.float64)
    s_hi = np.zeros((pos.shape[0], LANES), np.float64)
    start = offset
    while start + d <= LANES:
        cos[:, start:start + half] = np.cos(ang) * scale
        cos[:, start + half:start + d] = np.cos(ang) * scale
        s_lo[:, start:start + half] = -np.sin(ang) * scale
        s_hi[:, start + half:start + d] = np.sin(ang) * scale
        start += group
    return cos, s_lo, s_hi


def _even_tables(pos):
    rq = _rope_tables(pos, RET_DK, LANES, 0, 1.0)
    rk = _rope_tables(pos, RET_DK, LANES, 0, RET_DK ** -0.5)
    c = (MLA_NOPE + MLA_ROPE) ** -0.5 * LOG2E
    mq = _rope_tables(pos, MLA_ROPE, LANES, MLA_NOPE, c)
    mk = _rope_tables(pos, MLA_ROPE, LANES, MLA_NOPE, 1.0)
    tabs = [rq[0], rq[1] + rq[2], rk[0], rk[1] + rk[2], mq[0], mq[1], mq[2], mk[0], mk[1], mk[2]]
    return jnp.asarray(np.stack(tabs).astype(np.float32))


def _odd_tables(pos):
    c = DIFF_HD ** -0.5 * LOG2E
    dq = _rope_tables(pos, DIFF_HD, DIFF_HD, 0, c)
    dk = _rope_tables(pos, DIFF_HD, DIFF_HD, 0, 1.0)
    return jnp.asarray(np.stack(list(dq) + list(dk)).astype(np.float32))


def _retention_tables(L):
    lg = np.asarray(RET_LOG_GAMMA, np.float64)
    idx = np.arange(L, dtype=np.float64)
    diff = idx[:, None] - idx[None, :]
    dmask = np.where(diff[None] >= 0, np.exp(np.maximum(diff, 0.0)[None] * lg[:, None, None]), 0.0)
    qd = np.exp((idx[None, :] + 1.0) * lg[:, None])
    kd = np.exp((L - 1.0 - idx)[None, :] * lg[:, None])
    gl = np.exp(L * lg)
    qd = np.broadcast_to(qd[:, :, None], (RET_HEADS, L, LANES))
    kd = np.broadcast_to(kd[:, :, None], (RET_HEADS, L, LANES))
    gl = np.broadcast_to(gl[:, None, None], (RET_HEADS, RET_DK, RET_DV))
    f = lambda a: jnp.asarray(np.ascontiguousarray(a).astype(np.float32))
    return f(dmask), f(qd), f(kd), f(gl)


def _dot(a, b):
    return jnp.dot(a, b, preferred_element_type=F32)


def _dot_nt(a, b):
    return lax.dot_general(a, b, (((1,), (1,)), ((), ())), preferred_element_type=F32)


def _dot_tn(a, b):
    return lax.dot_general(a, b, (((0,), (0,)), ((), ())), preferred_element_type=F32)


def _rope(x, cos, s_lo, s_hi, half):
    return x * cos + pltpu.roll(x, LANES - half, 1) * s_lo + pltpu.roll(x, half, 1) * s_hi


def _layer_norm(x, g, b):
    mu = jnp.mean(x, -1, keepdims=True)
    xc = x - mu
    var = jnp.mean(xc * xc, -1, keepdims=True)
    return xc * lax.rsqrt(var + LN_EPS) * g + b


def _rms_norm(x, g):
    return x * lax.rsqrt(jnp.mean(x * x, -1, keepdims=True) + NORM_EPS) * g


def _even_in_kernel(x_ref, w_ref, wq_ref, gq_ref, gkv_ref, tab_ref,
                    rq_ref, rk_ref, rv_ref, rg_ref, q_ref, lat_ref, kr_ref):
    xb = x_ref[...].astype(BF16)
    c_rq, s_rq, c_rk, s_rk = tab_ref[0], tab_ref[1], tab_ref[2], tab_ref[3]
    hq = _dot(xb, w_ref[:, 0:RET_W])
    hk = _dot(xb, w_ref[:, RET_W:2 * RET_W])
    for h in range(RET_HEADS):
        sl = slice(h * LANES, (h + 1) * LANES)
        xq = hq[:, sl]
        rq_ref[:, sl] = (xq * c_rq + pltpu.roll(xq, RET_DK // 2, 1) * s_rq).astype(BF16)
        xk = hk[:, sl]
        rk_ref[:, sl] = (xk * c_rk + pltpu.roll(xk, RET_DK // 2, 1) * s_rk).astype(BF16)
    rv_ref[...] = _dot(xb, w_ref[:, 2 * RET_W:3 * RET_W]).astype(BF16)
    rg_ref[...] = _dot(xb, w_ref[:, 3 * RET_W:4 * RET_W]).astype(BF16)
    o = 4 * RET_W
    cq = _dot(xb, w_ref[:, o:o + MLA_Q_RANK])
    qn = _rms_norm(cq, gq_ref[...]).astype(BF16)
    qf = _dot(qn, wq_ref[...])
    c_q, lo_q, hi_q = tab_ref[4], tab_ref[5], tab_ref[6]
    for h in range(MLA_HEADS):
        sl = slice(h * LANES, (h + 1) * LANES)
        q_ref[:, sl] = _rope(qf[:, sl], c_q, lo_q, hi_q, MLA_ROPE // 2).astype(BF16)
    o += MLA_Q_RANK
    ckv = _dot(xb, w_ref[:, o:o + MLA_KV_RANK])
    lat_ref[...] = _rms_norm(ckv, gkv_ref[...])
    o += MLA_KV_RANK
    krp = _dot(xb, w_ref[:, o:o + LANES])
    krp = _rope(krp, tab_ref[7], tab_ref[8], tab_ref[9], MLA_ROPE // 2)
    kr_ref[...] = krp[:, MLA_NOPE:MLA_NOPE + MLA_ROPE]


def _even_in(x, w_a, wq, gq, gkv, tabs, tm):
    T, D = x.shape
    P = tabs.shape[1]
    nt = P // tm
    row = lambda i: (i, 0)
    const = lambda i: (0, 0)
    outs = [
        jax.ShapeDtypeStruct((T, RET_W), BF16), jax.ShapeDtypeStruct((T, RET_W), BF16),
        jax.ShapeDtypeStruct((T, RET_W), BF16), jax.ShapeDtypeStruct((T, RET_W), BF16),
        jax.ShapeDtypeStruct((T, MLA_PAD), BF16),
        jax.ShapeDtypeStruct((T, MLA_KV_RANK), F32), jax.ShapeDtypeStruct((T, MLA_ROPE), F32),
    ]
    return pl.pallas_call(
        _even_in_kernel,
        grid=(T // tm,),
        in_specs=[
            pl.BlockSpec((tm, D), row),
            pl.BlockSpec(w_a.shape, const),
            pl.BlockSpec(wq.shape, const),
            pl.BlockSpec(gq.shape, const),
            pl.BlockSpec(gkv.shape, const),
            pl.BlockSpec((tabs.shape[0], tm, LANES), lambda i: (0, i % nt, 0)),
        ],
        out_specs=[
            pl.BlockSpec((tm, RET_W), row), pl.BlockSpec((tm, RET_W), row),
            pl.BlockSpec((tm, RET_W), row), pl.BlockSpec((tm, RET_W), row),
            pl.BlockSpec((tm, MLA_PAD), row),
            pl.BlockSpec((tm, MLA_KV_RANK), row), pl.BlockSpec((tm, MLA_ROPE), row),
        ],
        out_shape=outs,
        compiler_params=_cparams(("parallel",)),
        name="even_in",
    )(x, w_a, wq, gq, gkv, tabs)


def _kv_up_kernel(lat_ref, kr_ref, wk_ref, wvt_ref, e_ref, k_ref, vt_ref):
    lb = lat_ref[...].astype(BF16)
    krb = kr_ref[...].astype(BF16)
    k_ref[...] = (_dot(lb, wk_ref[...]) + _dot(krb, e_ref[...])).astype(BF16)
    vt_ref[0] = _dot_nt(wvt_ref[...], lb).astype(BF16)


def _vt_layout(T, S, tm, width, dtype=BF16):
    nb, cols = (T // S, S) if S % tm == 0 else (1, T)
    nt = cols // tm
    spec = pl.BlockSpec((1, width, tm), lambda i: (i // nt, 0, i % nt))
    return spec, jax.ShapeDtypeStruct((nb, width, cols), dtype)


def _kv_up(lat, kr, wk, wvt, e_mat, S, tm):
    T = lat.shape[0]
    row = lambda i: (i, 0)
    const = lambda i: (0, 0)
    vt_spec, vt_shape = _vt_layout(T, S, tm, MLA_HEADS * MLA_V)
    return pl.pallas_call(
        _kv_up_kernel,
        grid=(T // tm,),
        in_specs=[
            pl.BlockSpec((tm, MLA_KV_RANK), row), pl.BlockSpec((tm, MLA_ROPE), row),
            pl.BlockSpec(wk.shape, const), pl.BlockSpec(wvt.shape, const), pl.BlockSpec(e_mat.shape, const),
        ],
        out_specs=[pl.BlockSpec((tm, MLA_PAD), row), vt_spec],
        out_shape=[jax.ShapeDtypeStruct((T, MLA_PAD), BF16), vt_shape],
        compiler_params=_cparams(("parallel",)),
        name="kv_up",
    )(lat, kr, wk, wvt, e_mat)


def _retention_kernel(q_ref, k_ref, v_ref, g_ref, s0_ref, dm_ref, qd_ref, kd_ref, gl_ref,
                      o_ref, st_ref, *, L, nchunk):
    @pl.when(pl.program_id(1) == 0)
    def _():
        st_ref[...] = s0_ref[...]

    for c in range(nchunk):
        rows = slice(c * L, (c + 1) * L)
        for h in range(RET_HEADS):
            sl = slice(h * LANES, (h + 1) * LANES)
            q = q_ref[rows, sl]
            k = k_ref[rows, sl]
            v = v_ref[rows, sl]
            st = st_ref[0, h]
            a = (_dot_nt(q, k) * dm_ref[h]).astype(BF16)
            o = _dot(a, v) + _dot(q, st.astype(BF16)) * qd_ref[h]
            kdec = (k.astype(F32) * kd_ref[h]).astype(BF16)
            st_ref[0, h] = st * gl_ref[h] + _dot_tn(kdec, v)
            mu = jnp.mean(o, -1, keepdims=True)
            oc = o - mu
            var = jnp.mean(oc * oc, -1, keepdims=True)
            on = oc * lax.rsqrt(var + LN_EPS)
            g = g_ref[rows, sl].astype(F32)
            o_ref[rows, sl] = (g * jax.nn.sigmoid(g) * on).astype(BF16)


def _retention(rq, rk, rv, rg, state0, B, S):
    L = min(S, 256)
    lt = min(S, 512)
    nj = S // lt
    dm, qd, kd, gl = _retention_tables(L)
    row = lambda b, j: (b * nj + j, 0)
    c3 = lambda b, j: (0, 0, 0)
    st_spec = pl.BlockSpec((1, RET_HEADS, RET_DK, RET_DV), lambda b, j: (b, 0, 0, 0))
    return pl.pallas_call(
        functools.partial(_retention_kernel, L=L, nchunk=lt // L),
        grid=(B, nj),
        in_specs=[pl.BlockSpec((lt, RET_W), row)] * 4 + [
            st_spec,
            pl.BlockSpec(dm.shape, c3), pl.BlockSpec(qd.shape, c3),
            pl.BlockSpec(kd.shape, c3), pl.BlockSpec(gl.shape, c3),
        ],
        out_specs=[pl.BlockSpec((lt, RET_W), row), st_spec],
        out_shape=[jax.ShapeDtypeStruct((B * S, RET_W), BF16),
                   jax.ShapeDtypeStruct((B, RET_HEADS, RET_DK, RET_DV), F32)],
        compiler_params=_cparams(("parallel", "arbitrary")),
        name="retention",
    )(rq, rk, rv, rg, state0, dm, qd, kd, gl)


def _query_t(q, tq):
    q = q.astype(F32)
    if tq < LANES:
        q = jnp.concatenate([q, jnp.zeros((LANES - tq, LANES), F32)], 0)
    return q.T


def _flash_t(streams, qi, tq, tk, sk, sk_valid, causal, m_ref, l_ref, acc_ref):
    for s in range(len(streams)):
        m_ref[s] = jnp.full(m_ref.shape[1:], NEG_INF, F32)
        l_ref[s] = jnp.zeros(l_ref.shape[1:], F32)
        acc_ref[s] = jnp.zeros(acc_ref.shape[1:], F32)

    def step(start, size, rel=None, valid=None):
        for s, (q_t, k_at, vt_at) in enumerate(streams):
            st = _dot(k_at(start, size), q_t)
            if rel is not None:
                kc = (lax.broadcasted_iota(jnp.int32, st.shape, 0) + rel) // CHUNK
                qc = lax.broadcasted_iota(jnp.int32, st.shape, 1) // CHUNK
                st = jnp.where(kc <= qc, st, NEG_INF)
            if valid is not None:
                st = jnp.where(lax.broadcasted_iota(jnp.int32, st.shape, 0) < valid, st, NEG_INF)
            m_old = m_ref[s]
            m_new = jnp.maximum(m_old, jnp.max(st, 0, keepdims=True))
            p = jnp.exp2(st - m_new)
            alpha = jnp.exp2(m_old - m_new)
            l_ref[s] = alpha * l_ref[s] + jnp.sum(p, 0, keepdims=True)
            acc_ref[s] = acc_ref[s] * alpha + _dot(vt_at(start, size), p.astype(BF16))
            m_ref[s] = m_new

    if causal:
        n_full = qi * (tq // tk)

        def body(j, c):
            step(pl.multiple_of(2 * j * tk, tk), tk)
            step(pl.multiple_of((2 * j + 1) * tk, tk), tk)
            return c

        lax.fori_loop(0, n_full // 2, body, 0)

        if (tq // tk) % 2 == 1:
            @pl.when(n_full % 2 == 1)
            def _():
                step(pl.multiple_of((n_full - 1) * tk, tk), tk)

        for d in range(tq // tk):
            step(pl.multiple_of(qi * tq + d * tk, tk), tk, rel=d * tk)
    else:
        for j in range(sk // tk):
            last_valid = sk_valid - j * tk
            step(j * tk, tk, valid=last_valid if last_valid < tk else None)
    return [acc_ref[s] / l_ref[s] for s in range(len(streams))]


def _attn_scratch(n_streams, dv, tq):
    tqp = max(tq, LANES)
    return [pltpu.VMEM((n_streams, 1, tqp), F32), pltpu.VMEM((n_streams, 1, tqp), F32),
            pltpu.VMEM((n_streams, dv, tqp), F32)]


def _mla_attn_kernel(q_ref, k_ref, vt_ref, o_ref, m_ref, l_ref, acc_ref, *, tq, tk, sk, sk_valid, causal):
    streams = []
    for hh in range(2):
        sl = slice(hh * LANES, (hh + 1) * LANES)
        vrows = slice(hh * MLA_V, (hh + 1) * MLA_V)
        k_at = lambda start, n, sl=sl: k_ref[pl.ds(start, n), sl]
        vt_at = lambda start, n, vrows=vrows: vt_ref[0, vrows, pl.ds(start, n)]
        streams.append((_query_t(q_ref[:, sl], tq).astype(BF16), k_at, vt_at))
    outs = _flash_t(streams, pl.program_id(2), tq, tk, sk, sk_valid, causal, m_ref, l_ref, acc_ref)
    o_ref[...] = jnp.concatenate(outs, 0).T[:tq].astype(BF16)


def _mla_attn(q, k, vt, B, sq, sk, sk_valid, tq, tk, causal):
    nq = sq // tq
    npair = MLA_HEADS // 2
    return pl.pallas_call(
        functools.partial(_mla_attn_kernel, tq=tq, tk=tk, sk=sk, sk_valid=sk_valid, causal=causal),
        grid=(B, npair, nq),
        in_specs=[
            pl.BlockSpec((tq, 2 * LANES), lambda b, p, i: (b * nq + i, p)),
            pl.BlockSpec((sk, 2 * LANES), lambda b, p, i: (b, p)),
            pl.BlockSpec((1, 2 * MLA_V, sk), lambda b, p, i: (b, p, 0)),
        ],
        out_specs=pl.BlockSpec((tq, LANES), lambda b, p, i: (b * nq + i, p)),
        out_shape=jax.ShapeDtypeStruct((B * sq, MLA_HEADS * MLA_V), BF16),
        scratch_shapes=_attn_scratch(2, MLA_V, tq),
        compiler_params=_cparams(("parallel", "parallel", "arbitrary")),
        name="mla_attn",
    )(q, k, vt)


def _diff_attn_kernel(lam_ref, gn_ref, q_ref, k_ref, vt_ref, o_ref, m_ref, l_ref, acc_ref,
                      *, tq, tk, sk, sk_valid, causal, lam_init):
    k_at = lambda start, n: k_ref[pl.ds(start, n), :]
    vt_at = lambda start, n: vt_ref[0, :, pl.ds(start, n)]
    q_t = _query_t(q_ref[...], tq)
    feat = lax.broadcasted_iota(jnp.int32, q_t.shape, 0)
    zero = jnp.zeros_like(q_t)
    q1 = jnp.where(feat < DIFF_HD, q_t, zero).astype(BF16)
    q2 = jnp.where(feat < DIFF_HD, zero, q_t).astype(BF16)
    o1, o2 = _flash_t([(q1, k_at, vt_at), (q2, k_at, vt_at)], pl.program_id(2), tq, tk, sk, sk_valid,
                      causal, m_ref, l_ref, acc_ref)
    lv = lam_ref[...]
    lam = (jnp.exp(jnp.sum(lv[0:1] * lv[1:2], -1, keepdims=True))
           - jnp.exp(jnp.sum(lv[2:3] * lv[3:4], -1, keepdims=True)) + lam_init)
    o = (o1 - lam * o2).T[:tq]
    o_ref[...] = (_rms_norm(o, gn_ref[...]) * (1.0 - lam_init)).astype(BF16)


def _diff_attn(lam_vecs, gn, q, k, vt, B, sq, sk, sk_valid, tq, tk, causal, lam_init):
    nq = sq // tq
    const = lambda b, h, i: (0, 0)
    return pl.pallas_call(
        functools.partial(_diff_attn_kernel, tq=tq, tk=tk, sk=sk, sk_valid=sk_valid, causal=causal,
                          lam_init=lam_init),
        grid=(B, DIFF_HEADS, nq),
        in_specs=[
            pl.BlockSpec(lam_vecs.shape, const), pl.BlockSpec(gn.shape, const),
            pl.BlockSpec((tq, LANES), lambda b, h, i: (b * nq + i, h)),
            pl.BlockSpec((sk, LANES), lambda b, h, i: (b, h)),
            pl.BlockSpec((1, DIFF_V, sk), lambda b, h, i: (b, h, 0)),
        ],
        out_specs=pl.BlockSpec((tq, LANES), lambda b, h, i: (b * nq + i, h)),
        out_shape=jax.ShapeDtypeStruct((B * sq, DIFF_HEADS * DIFF_V), BF16),
        scratch_shapes=_attn_scratch(2, DIFF_V, tq),
        compiler_params=_cparams(("parallel", "parallel", "arbitrary")),
        name="diff_attn",
    )(lam_vecs, gn, q, k, vt)


def _out_proj_kernel(*refs, n_in):
    a_refs = refs[:n_in]
    (w_ref, x_ref, g_ref, b_ref, wr_ref, bias_ref, tri_ref,
     o_ref, gate_ref, route_ref, cnt_out_ref, gt_ref, oh_ref, cnt_ref) = refs[n_in:]
    y = None
    off = 0
    for a_ref in a_refs:
        width = a_ref.shape[1]
        part = _dot(a_ref[...], w_ref[off:off + width, :])
        y = part if y is None else y + part
        off += width
    x1 = _layer_norm(DN_ALPHA * x_ref[...] + y, g_ref[...], b_ref[...])
    o_ref[...] = x1
    _route(x1, wr_ref, bias_ref, tri_ref, gate_ref, route_ref, cnt_out_ref, gt_ref, oh_ref, cnt_ref)


def _out_proj(acts, w, x, g, b, wr_t, bias, tm):
    T, D = x.shape
    tri = jnp.asarray(np.triu(np.ones((tm, tm), np.float32), 1)).astype(BF16)
    row = lambda i: (i, 0)
    const = lambda i: (0, 0)
    return pl.pallas_call(
        functools.partial(_out_proj_kernel, n_in=len(acts)),
        grid=(T // tm,),
        in_specs=[pl.BlockSpec((tm, a.shape[1]), row) for a in acts] + [
            pl.BlockSpec(w.shape, const), pl.BlockSpec((tm, D), row),
            pl.BlockSpec(g.shape, const), pl.BlockSpec(b.shape, const),
            pl.BlockSpec(wr_t.shape, const), pl.BlockSpec(bias.shape, const), pl.BlockSpec(tri.shape, const),
        ],
        out_specs=[pl.BlockSpec((tm, D), row), pl.BlockSpec((tm, LANES), row),
                   pl.BlockSpec((8, tm), lambda i: (0, i)), pl.BlockSpec((SEG_ROWS, LANES), const)],
        out_shape=[jax.ShapeDtypeStruct((T, D), F32), jax.ShapeDtypeStruct((T, LANES), F32),
                   jax.ShapeDtypeStruct((8, T), jnp.int32), jax.ShapeDtypeStruct((SEG_ROWS, LANES), F32)],
        scratch_shapes=[pltpu.VMEM((LANES, tm), F32), pltpu.VMEM((SEG_ROWS, tm), F32),
                        pltpu.VMEM((SEG_ROWS, LANES), F32)],
        compiler_params=_cparams(("arbitrary",)),
        name="out_proj",
    )(*acts, w, x, g, b, wr_t, bias, tri)


def _odd_in_kernel(x_ref, w_ref, tab_ref, q_ref, kt_ref, kb_ref, vf_ref, vt_ref):
    xb = x_ref[...].astype(BF16)
    W = DIFF_HEADS * 2 * DIFF_HD
    hq = _dot(xb, w_ref[:, 0:W])
    hk = _dot(xb, w_ref[:, W:2 * W])
    for h in range(W // LANES):
        sl = slice(h * LANES, (h + 1) * LANES)
        q_ref[:, sl] = _rope(hq[:, sl], tab_ref[0], tab_ref[1], tab_ref[2], DIFF_HD // 2).astype(BF16)
        kk = _rope(hk[:, sl], tab_ref[3], tab_ref[4], tab_ref[5], DIFF_HD // 2)
        kt_ref[0, sl, :] = kk.T
        kb_ref[:, sl] = kk.astype(BF16)
    hv = _dot(xb, w_ref[:, 2 * W:])
    vf_ref[...] = hv
    vt_ref[0] = hv.T.astype(BF16)


def _odd_in(x, w, tabs, S, tm):
    T, D = x.shape
    W = DIFF_HEADS * 2 * DIFF_HD
    nt = tabs.shape[1] // tm
    row = lambda i: (i, 0)
    blk = pl.BlockSpec((tm, W), row)
    vt_spec, vt_shape = _vt_layout(T, S, tm, DIFF_HEADS * DIFF_V)
    kt_spec, kt_shape = _vt_layout(T, S, tm, W, F32)
    return pl.pallas_call(
        _odd_in_kernel,
        grid=(T // tm,),
        in_specs=[pl.BlockSpec((tm, D), row), pl.BlockSpec(w.shape, lambda i: (0, 0)),
                  pl.BlockSpec((tabs.shape[0], tm, LANES), lambda i: (0, i % nt, 0))],
        out_specs=[blk, kt_spec, blk, blk, vt_spec],
        out_shape=[jax.ShapeDtypeStruct((T, W), BF16), kt_shape,
                   jax.ShapeDtypeStruct((T, W), BF16), jax.ShapeDtypeStruct((T, W), F32), vt_shape],
        compiler_params=_cparams(("parallel",)),
        name="odd_in",
    )(x, w, tabs)


def _route(x, wr_ref, bias_ref, tri_ref, g_ref, route_ref, cnt_out_ref, gt_ref, oh_ref, cnt_ref):
    tm = x.shape[0]
    logits = _dot_nt(wr_ref[...], x.astype(BF16))
    sc = jax.nn.sigmoid(logits)
    sel = sc + bias_ref[...]
    r = [sel[e:e + 1, :] for e in range(N_EXPERTS)]
    s = [sc[e:e + 1, :] for e in range(N_EXPERTS)]
    grp = []
    for g in range(N_GROUPS):
        a, b, c, d = r[4 * g:4 * g + 4]
        top2 = jnp.maximum(jnp.maximum(jnp.maximum(a + b, a + c), jnp.maximum(a + d, b + c)),
                           jnp.maximum(b + d, c + d))
        grp.append(top2)
    best = jnp.maximum(jnp.maximum(grp[0], grp[1]), jnp.maximum(grp[2], grp[3]))
    taken = jnp.zeros((1, tm), jnp.bool_)
    chosen = []
    for g in range(N_GROUPS):
        win = jnp.logical_and(grp[g] == best, jnp.logical_not(taken))
        chosen.append(win)
        taken = jnp.logical_or(taken, win)
    picked = []
    for e in range(N_EXPERTS):
        g = e // EXPERTS_PER_GROUP
        rank = jnp.zeros((1, tm), F32)
        for k in range(4 * g, 4 * g + 4):
            if k < e:
                rank = rank + (r[k] >= r[e]).astype(F32)
            elif k > e:
                rank = rank + (r[k] > r[e]).astype(F32)
        picked.append(jnp.logical_and(chosen[g], rank < 2.0))
    w = [jnp.where(picked[e], s[e], 0.0) for e in range(N_EXPERTS)]
    denom = w[0]
    for e in range(1, N_EXPERTS):
        denom = denom + w[e]
    gt_ref[...] = jnp.zeros_like(gt_ref)
    for e in range(N_EXPERTS):
        gt_ref[e:e + 1, :] = w[e] / denom
    g_ref[...] = gt_ref[...].T

    @pl.when(pl.program_id(0) == 0)
    def _():
        cnt_ref[...] = jnp.zeros_like(cnt_ref)

    oh_ref[...] = jnp.zeros_like(oh_ref)
    for g in range(N_GROUPS):
        for p, (a, b) in enumerate(EXPERT_PAIRS):
            both = jnp.logical_and(picked[4 * g + a], picked[4 * g + b])
            oh_ref[g * N_PAIRS + p:g * N_PAIRS + p + 1, :] = both.astype(F32)
    oh = oh_ref[...]
    before = _dot(oh.astype(BF16), tri_ref[...])
    base = cnt_ref[:, 0:1]
    rank = jnp.sum(oh * (base + before), 0, keepdims=True)
    sid = jnp.sum(oh * lax.broadcasted_iota(jnp.int32, oh.shape, 0).astype(F32), 0, keepdims=True)
    row = lax.broadcasted_iota(jnp.int32, route_ref.shape, 0)
    route_ref[...] = jnp.where(row == 0, sid, jnp.where(row == 1, rank, 0.0)).astype(jnp.int32)
    cnt_ref[...] = cnt_ref[...] + jnp.sum(oh, 1, keepdims=True)
    cnt_out_ref[...] = cnt_ref[...]


def _moe_dense_kernel(x_ref, gate_ref, wg_ref, wu_ref, wd_ref, g_ref, b_ref, o_ref, xb_ref, acc_ref):
    e = pl.program_id(1)

    @pl.when(e == 0)
    def _():
        xb_ref[...] = x_ref[...].astype(BF16)
        acc_ref[...] = jnp.zeros_like(acc_ref)

    xb = xb_ref[...]
    h = jax.nn.silu(_dot(xb, wg_ref[0, 0].astype(BF16))) * _dot(xb, wu_ref[0, 0].astype(BF16))
    y = _dot(h.astype(BF16), wd_ref[0, 0].astype(BF16))
    gates = gate_ref[...]
    lane = lax.broadcasted_iota(jnp.int32, gates.shape, 1)
    gcol = jnp.sum(jnp.where(lane == e, gates, 0.0), -1, keepdims=True)
    acc_ref[...] += gcol * y

    @pl.when(e == pl.num_programs(1) - 1)
    def _():
        o_ref[...] = _layer_norm(DN_ALPHA * x_ref[...] + acc_ref[...], g_ref[...], b_ref[...])


def _moe_dense(x, gates, wg, wu, wd, l, g, b, tm):
    T, D = x.shape
    _, E, _, H = wg.shape
    row = lambda i, e: (i, 0)
    const = lambda i, e: (0, 0)
    wsel = lambda i, e: (l, e, 0, 0)
    return pl.pallas_call(
        _moe_dense_kernel,
        grid=(T // tm, E),
        in_specs=[
            pl.BlockSpec((tm, D), row), pl.BlockSpec((tm, LANES), row),
            pl.BlockSpec((1, 1, D, H), wsel), pl.BlockSpec((1, 1, D, H), wsel),
            pl.BlockSpec((1, 1, H, D), wsel),
            pl.BlockSpec(g.shape, const), pl.BlockSpec(b.shape, const),
        ],
        out_specs=pl.BlockSpec((tm, D), row),
        out_shape=jax.ShapeDtypeStruct((T, D), F32),
        scratch_shapes=[pltpu.VMEM((tm, D), BF16), pltpu.VMEM((tm, D), F32)],
        compiler_params=_cparams(("parallel", "arbitrary")),
        name="moe_dense",
    )(x, gates, wg, wu, wd, g, b)


SC_CORES = 2
SC_SUBCORES = 16
SC_WORKERS = SC_CORES * SC_SUBCORES
SC_TILE_BYTES = 384 * 1024
SPARSE_ROW_TILE = 512
SPARSE_MIN_TOKENS = 4096


def _sc_ring(per_w, row_bytes):
    for ch, nbuf in ((16, 4), (16, 2), (8, 2)):
        if per_w % (ch * nbuf) == 0 and ch * nbuf * row_bytes <= SC_TILE_BYTES:
            return ch, nbuf
    raise ValueError(f"no SparseCore gather ring for {per_w} rows of {row_bytes} bytes per subcore")


def _sc_gather(table, idx):
    R = idx.shape[0]
    D = table.shape[1]
    per_w = R // SC_WORKERS
    assert per_w * SC_WORKERS == R and per_w % 8 == 0
    ch, nbuf = _sc_ring(per_w, D * table.dtype.itemsize)
    nchunk = per_w // ch
    mesh = plsc.VectorSubcoreMesh(core_axis_name="c", subcore_axis_name="s")

    @functools.partial(
        pl.kernel, mesh=mesh,
        out_type=jax.ShapeDtypeStruct((R, D), table.dtype),
        scratch_types=[pltpu.VMEM((per_w,), jnp.int32), pltpu.VMEM((nbuf, ch, D), table.dtype),
                       pltpu.SemaphoreType.DMA((nbuf,)), pltpu.SemaphoreType.DMA((nbuf,))],
    )
    def gather_kernel(table_hbm, idx_hbm, out_hbm, idx_v, rows_v, gsem, wsem):
        base = (lax.axis_index("s") * SC_CORES + lax.axis_index("c")) * per_w
        pltpu.sync_copy(idx_hbm.at[pl.ds(base, per_w)], idx_v)

        def gather(c, b):
            return pltpu.make_async_copy(table_hbm.at[idx_v.at[pl.ds(c * ch, ch)]], rows_v.at[b], gsem.at[b])

        def write(c, b):
            return pltpu.make_async_copy(rows_v.at[b], out_hbm.at[pl.ds(base + c * ch, ch)], wsem.at[b])

        for b in range(nbuf - 1):
            gather(b, b).start()

        @pl.loop(0, nchunk, step=nbuf)
        def _(c):
            for b in range(nbuf):
                cc = c + b
                gather(cc, b).wait()
                write(cc, b).start()
                pb = (b - 1) % nbuf

                @pl.when(cc + nbuf - 1 < nchunk)
                def _():
                    @pl.when(cc >= 1)
                    def _():
                        write(cc - 1, pb).wait()

                    gather(cc + nbuf - 1, pb).start()

        for b in range(nbuf):
            write(nchunk - nbuf + b, b).wait()

    return gather_kernel(table, idx)


def _moe_group_kernel(te_ref, nv_ref, x_ref, gate_ref, *refs):
    w_refs, (g_ref, b_ref, o_ref) = refs[:3 * TOP_K], refs[3 * TOP_K:]
    j = pl.program_id(0)
    live = j < nv_ref[0]

    @pl.when(live)
    def _():
        x = x_ref[...]
        xb = x.astype(BF16)
        gates = gate_ref[...]
        lane = lax.broadcasted_iota(jnp.int32, gates.shape, 1)
        acc = None
        for k in range(TOP_K):
            wg_ref, wu_ref, wd_ref = w_refs[3 * k:3 * k + 3]
            h = jax.nn.silu(_dot(xb, wg_ref[0, 0].astype(BF16))) * _dot(xb, wu_ref[0, 0].astype(BF16))
            y = _dot(h.astype(BF16), wd_ref[0, 0].astype(BF16))
            gcol = jnp.sum(jnp.where(lane == te_ref[TOP_K * j + k], gates, 0.0), -1, keepdims=True)
            acc = gcol * y if acc is None else acc + gcol * y
        o_ref[...] = _layer_norm(DN_ALPHA * x + acc, g_ref[...], b_ref[...])

    @pl.when(jnp.logical_not(live))
    def _():
        o_ref[...] = jnp.zeros_like(o_ref)


def _moe_group(tile_experts, n_valid, xs, gs, wg, wu, wd, l, g, b, tm):
    R, D = xs.shape
    H = wg.shape[3]
    row = lambda j, te, nv: (j, 0)
    const = lambda j, te, nv: (0, 0)
    w_specs, w_args = [], []
    for k in range(TOP_K):
        wsel = lambda j, te, nv, k=k: (l, te[TOP_K * j + k], 0, 0)
        w_specs += [pl.BlockSpec((1, 1, D, H), wsel), pl.BlockSpec((1, 1, D, H), wsel),
                    pl.BlockSpec((1, 1, H, D), wsel)]
        w_args += [wg, wu, wd]
    grid_spec = pltpu.PrefetchScalarGridSpec(
        num_scalar_prefetch=2,
        grid=(R // tm,),
        in_specs=[pl.BlockSpec((tm, D), row), pl.BlockSpec((tm, LANES), row)] + w_specs + [
            pl.BlockSpec(g.shape, const), pl.BlockSpec(b.shape, const)],
        out_specs=pl.BlockSpec((tm, D), row),
    )
    return pl.pallas_call(
        _moe_group_kernel,
        grid_spec=grid_spec,
        out_shape=jax.ShapeDtypeStruct((R, D), F32),
        compiler_params=_cparams(("arbitrary",)),
        name="moe_group",
    )(tile_experts, n_valid, xs, gs, *w_args, g, b)


def _sc_scatter(x, dest, pad_dest, n_rows):
    T, D = x.shape
    P = pad_dest.shape[0]
    per_w, pad_w = T // SC_WORKERS, P // SC_WORKERS
    assert per_w * SC_WORKERS == T and pad_w * SC_WORKERS == P and T + P == n_rows
    ch, nbuf = _sc_ring(per_w, D * x.dtype.itemsize)
    assert pad_w % ch == 0
    nchunk, npad = per_w // ch, pad_w // ch
    mesh = plsc.VectorSubcoreMesh(core_axis_name="c", subcore_axis_name="s")

    @functools.partial(
        pl.kernel, mesh=mesh,
        out_type=jax.ShapeDtypeStruct((n_rows, D), x.dtype),
        scratch_types=[pltpu.VMEM((nchunk, ch), jnp.int32), pltpu.VMEM((npad, ch), jnp.int32),
                       pltpu.VMEM((nbuf, ch, D), x.dtype), pltpu.VMEM((ch, D), x.dtype),
                       pltpu.SemaphoreType.DMA((nbuf,)), pltpu.SemaphoreType.DMA((nbuf,)),
                       pltpu.SemaphoreType.DMA],
    )
    def scatter_kernel(x_hbm, dest_hbm, pad_hbm, zero_hbm, out_hbm, dest_v, pad_v, rows_v, zero_v,
                       rsem, wsem, zsem):
        wid = lax.axis_index("s") * SC_CORES + lax.axis_index("c")
        base = wid * per_w
        pltpu.sync_copy(dest_hbm.at[wid], dest_v)
        pltpu.sync_copy(pad_hbm.at[wid], pad_v)
        pltpu.sync_copy(zero_hbm, zero_v)

        def read(c, b):
            return pltpu.make_async_copy(x_hbm.at[pl.ds(base + c * ch, ch)], rows_v.at[b], rsem.at[b])

        def write(c, b):
            return pltpu.make_async_copy(rows_v.at[b], out_hbm.at[dest_v.at[c]], wsem.at[b])

        def write_zero(c):
            return pltpu.make_async_copy(zero_v, out_hbm.at[pad_v.at[c]], zsem)

        for c in range(npad):
            write_zero(c).start()
        for b in range(nbuf - 1):
            read(b, b).start()

        @pl.loop(0, nchunk, step=nbuf)
        def _(c):
            for b in range(nbuf):
                cc = c + b
                read(cc, b).wait()
                write(cc, b).start()
                pb = (b - 1) % nbuf

                @pl.when(cc + nbuf - 1 < nchunk)
                def _():
                    @pl.when(cc >= 1)
                    def _():
                        write(cc - 1, pb).wait()

                    read(cc + nbuf - 1, pb).start()

        for b in range(nbuf):
            write(nchunk - nbuf + b, b).wait()
        for c in range(npad):
            write_zero(c).wait()

    return scatter_kernel(x, dest.reshape(SC_WORKERS, nchunk, ch), pad_dest.reshape(SC_WORKERS, npad, ch),
                          jnp.zeros((ch, D), x.dtype))


def _moe_sparse(x, gates, route, counts, wg, wu, wd, l, g, b):
    T = x.shape[0]
    tm = SPARSE_ROW_TILE
    n_tiles = T // tm + N_SEG
    n_rows = n_tiles * tm
    sid, rank = route[0], route[1]
    cnt = counts[:N_SEG, 0].astype(jnp.int32)
    tiles = (cnt + tm - 1) // tm
    ends = jnp.cumsum(tiles)
    starts = (ends - tiles) * tm
    seg_ids = jnp.arange(N_SEG, dtype=jnp.int32)
    pick = lambda which, vals: jnp.sum(jnp.where(which[:, None] == seg_ids[None, :], vals[None, :], 0), 1)
    passed = lambda pos, bounds: jnp.sum((pos[:, None] >= bounds[None, :]).astype(jnp.int32), 1)
    dest = rank + pick(sid, starts)
    pad_cnt = tiles * tm - cnt
    pad_end = jnp.cumsum(pad_cnt)
    p = jnp.arange(n_rows - T, dtype=jnp.int32)
    seg = passed(p, pad_end)
    pad_dest = jnp.where(seg < N_SEG,
                         pick(seg, starts + cnt) + p - pick(seg, pad_end - pad_cnt),
                         ends[N_SEG - 1] * tm + p - pad_end[N_SEG - 1])
    tile_seg = jnp.minimum(passed(jnp.arange(n_tiles, dtype=jnp.int32), ends), N_SEG - 1)
    pair = jnp.asarray(np.asarray(EXPERT_PAIRS, np.int32))
    tile_experts = (tile_seg // N_PAIRS * EXPERTS_PER_GROUP)[:, None] + pair[tile_seg % N_PAIRS]
    xs = _sc_scatter(x, dest, pad_dest, n_rows)
    gs = _sc_scatter(gates, dest, pad_dest, n_rows)
    ys = _moe_group(tile_experts.reshape(-1), ends[N_SEG - 1:], xs, gs, wg, wu, wd, l, g, b, tm)
    return _sc_gather(ys, dest)


def _prep_weights(w_in_even, w_uq, w_ukv, w_out_even, w_in_odd, w_out_odd, w_router,
                  w_expert_gate, w_expert_up, w_expert_down):
    d = w_in_even.shape[1]
    n_main = 4 * RET_W + MLA_Q_RANK + MLA_KV_RANK
    w_in = w_in_even[0]
    kr_cols = jnp.pad(w_in[:, n_main:], ((0, 0), (MLA_NOPE, LANES - MLA_NOPE - MLA_ROPE)))
    w_a = jnp.concatenate([w_in[:, :n_main], kr_cols], 1).astype(BF16)
    qd = MLA_NOPE + MLA_ROPE
    wq = jnp.pad(w_uq[0].reshape(MLA_Q_RANK, MLA_HEADS, qd), ((0, 0), (0, 0), (0, LANES - qd)))
    wq = wq.reshape(MLA_Q_RANK, MLA_PAD).astype(BF16)
    wkv = w_ukv[0].reshape(MLA_KV_RANK, MLA_HEADS, MLA_NOPE + MLA_V)
    wk = jnp.pad(wkv[:, :, :MLA_NOPE], ((0, 0), (0, 0), (0, LANES - MLA_NOPE)))
    wk = wk.reshape(MLA_KV_RANK, MLA_PAD).astype(BF16)
    wvt = wkv[:, :, MLA_NOPE:].reshape(MLA_KV_RANK, MLA_HEADS * MLA_V).T.astype(BF16)
    e_np = np.zeros((MLA_ROPE, MLA_HEADS, LANES), np.float32)
    for j in range(MLA_ROPE):
        e_np[j, :, MLA_NOPE + j] = 1.0
    e_mat = jnp.asarray(e_np.reshape(MLA_ROPE, MLA_PAD)).astype(BF16)
    return dict(
        w_a=w_a, wq=wq, wk=wk, wvt=wvt, e_mat=e_mat,
        w_out_even=w_out_even[0].astype(BF16), w_in_odd=w_in_odd[0].astype(BF16),
        w_out_odd=w_out_odd[0].astype(BF16), wr_t=w_router.T.astype(BF16),
        wg=w_expert_gate, wu=w_expert_up, wd=w_expert_down,
    )


def _moe(routed, wts, l, ln_g, ln_b, tm):
    x, gates, route, counts = routed
    g, b = ln_g[l, 1][None], ln_b[l, 1][None]
    if x.shape[0] >= SPARSE_MIN_TOKENS:
        return _moe_sparse(x, gates, route, counts, wts["wg"], wts["wu"], wts["wd"], l, g, b)
    return _moe_dense(x, gates, wts["wg"], wts["wu"], wts["wd"], l, g, b, tm)


def _trunk(x3, pos0, past, wts, prm):
    B, S, D = x3.shape
    T = B * S
    x = x3.reshape(T, D)
    tm = min(T, 512)
    rep = max(tm // S, 1)
    pos = np.tile(pos0 + np.arange(S), rep)
    ln_g, ln_b = prm["ln_g"], prm["ln_b"]

    rq, rk, rv, rg, q, lat, kr = _even_in(x, wts["w_a"], wts["wq"], prm["gq"], prm["gkv"], _even_tables(pos), tm)
    causal = past is None
    if causal:
        state0 = jnp.zeros((B, RET_HEADS, RET_DK, RET_DV), F32)
        lat_all, kr_all, sk, skp = lat, kr, S, S
        tq, tk = min(S, 1024), min(S, 512)
    else:
        state0 = past["state"]
        sk = past["lat"].shape[1] + S
        skp = -(-sk // LANES) * LANES
        padk = lambda parts: jnp.concatenate(
            parts + [jnp.zeros((B, skp - sk, parts[0].shape[2]), parts[0].dtype)], 1).reshape(B * skp, -1)
        lat_all = padk([past["lat"], lat.reshape(B, S, -1)])
        kr_all = padk([past["kr"], kr.reshape(B, S, -1)])
        tq, tk = S, skp
    ret_out, ret_state = _retention(rq, rk, rv, rg, state0, B, S)
    tkv = 512 if (B * skp) % 512 == 0 else skp
    k_mla, vt_mla = _kv_up(lat_all, kr_all, wts["wk"], wts["wvt"], wts["e_mat"], skp, tkv)
    if vt_mla.shape[0] != B:
        vt_mla = vt_mla.reshape(-1, B, skp).transpose(1, 0, 2)
    mla_out = _mla_attn(q, k_mla, vt_mla, B, S, skp, sk, tq, tk, causal)
    routed = _out_proj([ret_out, mla_out], wts["w_out_even"], x, ln_g[0, 0][None], ln_b[0, 0][None],
                       wts["wr_t"], prm["bias"], tm)
    x = _moe(routed, wts, 0, ln_g, ln_b, tm)

    qd, kt, kb, vf, vt = _odd_in(x, wts["w_in_odd"], _odd_tables(pos), S, tm)
    if vt.shape[0] != B:
        vt = vt.reshape(-1, B, S).transpose(1, 0, 2)
        kt = kt.reshape(-1, B, S).transpose(1, 0, 2)
    kf = kt.reshape(1, B, 2 * DIFF_HEADS, DIFF_HD, S).transpose(0, 1, 4, 2, 3)
    if causal:
        k_all, vt_all = kb, vt
    else:
        k_all = padk([past["dk"], kb.reshape(B, S, -1)])
        vt_all = jnp.concatenate([past["dv"].transpose(0, 2, 1), vt,
                                  jnp.zeros((B, vt.shape[1], skp - sk), BF16)], 2)
    lam_init = 0.8 - 0.6 * math.exp(-0.3 * 1)
    d_out = _diff_attn(prm["lam"], prm["gn"], qd, k_all, vt_all, B, S, skp, sk, tq, tk, causal, lam_init)
    routed = _out_proj([d_out], wts["w_out_odd"], x, ln_g[1, 0][None], ln_b[1, 0][None],
                       wts["wr_t"], prm["bias"], tm)
    x = _moe(routed, wts, 1, ln_g, ln_b, tm)

    return (x.reshape(B, S, D), ret_state[None], lat.reshape(1, B, S, -1), kr.reshape(1, B, S, -1),
            kf, vf.reshape(1, B, S, DIFF_HEADS, DIFF_V))


def kernel(x_prompt, x_sample, state_ret, cache_mla_latent, cache_mla_krope, cache_diff_k, cache_diff_v,
           w_in_even, w_uq, w_ukv, g_qnorm, g_kvnorm, w_out_even,
           w_in_odd, lambda_q1, lambda_k1, lambda_q2, lambda_k2, g_diff_norm, w_out_odd,
           ln_g, ln_b, w_router, router_bias, w_expert_gate, w_expert_up, w_expert_down):
    wts = _prep_weights(w_in_even, w_uq, w_ukv, w_out_even, w_in_odd, w_out_odd, w_router,
                        w_expert_gate, w_expert_up, w_expert_down)
    prm = dict(
        gq=g_qnorm[0][None].astype(F32), gkv=g_kvnorm[0][None].astype(F32),
        lam=jnp.stack([lambda_q1[0], lambda_k1[0], lambda_q2[0], lambda_k2[0]]).astype(F32),
        gn=g_diff_norm[0][None].astype(F32), bias=router_bias.reshape(N_EXPERTS, 1).astype(F32),
        ln_g=ln_g.astype(F32), ln_b=ln_b.astype(F32),
    )
    past_len = cache_mla_latent.shape[2]
    db = x_sample.shape[0]
    past = dict(
        state=state_ret[0].astype(F32), lat=cache_mla_latent[0], kr=cache_mla_krope[0],
        dk=cache_diff_k[0].reshape(db, past_len, -1).astype(BF16),
        dv=cache_diff_v[0].reshape(db, past_len, -1).astype(BF16),
    )
    outs_p = _trunk(x_prompt, 0, None, wts, prm)
    outs_s = _trunk(x_sample, past_len, past, wts, prm)
    return (outs_p[0], outs_s[0]) + outs_p[1:] + outs_s[1:]
```

```python
import functools
import math

import numpy as np
import jax
import jax.numpy as jnp
from jax import lax
from jax.experimental import pallas as pl
from jax.experimental.pallas import tpu as pltpu
from jax.experimental.pallas import tpu_sc as plsc

F32 = jnp.float32
BF16 = jnp.bfloat16

CHUNK = 64
ROPE_THETA = 10000.0
NEG_INF = -1e30
LN_EPS = 1e-5
NORM_EPS = 1e-6
DEPTH = 2
DN_ALPHA = (2.0 * DEPTH) ** 0.25
RET_HEADS = 4
RET_DK = 128
RET_DV = 128
RET_LOG_GAMMA = tuple(math.log(1.0 - 2.0 ** (-5 - h)) for h in range(RET_HEADS))
MLA_HEADS = 8
MLA_Q_RANK = 384
MLA_KV_RANK = 256
MLA_NOPE = 64
MLA_ROPE = 32
MLA_V = 64
DIFF_HEADS = 8
DIFF_HD = 64
DIFF_V = 128
DIFF_GROUP = 1
N_EXPERTS = 16
N_GROUPS = 4
EXPERTS_PER_GROUP = 4
TOP_K = 2
EXPERT_PAIRS = tuple((a, b) for a in range(EXPERTS_PER_GROUP) for b in range(a + 1, EXPERTS_PER_GROUP))
N_PAIRS = len(EXPERT_PAIRS)
N_SEG = N_GROUPS * N_PAIRS
SEG_ROWS = 32
LOG2E = math.log2(math.e)

LANES = 128
RET_W = RET_HEADS * RET_DK
MLA_PAD = MLA_HEADS * LANES
VMEM_LIMIT = 56 * 1024 * 1024


def _cparams(sem):
    return pltpu.CompilerParams(dimension_semantics=sem, vmem_limit_bytes=VMEM_LIMIT)


def _rope_tables(pos, d, group, offset, scale):
    pos = np.asarray(pos, np.float64)
    half = d // 2
    inv = 1.0 / (ROPE_THETA ** (np.arange(0, d, 2, dtype=np.float64) / d))
    ang = pos[:, None] * inv[None, :]
    cos = np.full((pos.shape[0], LANES), scale, np.float64)
    s_lo = np.zeros((pos.shape[0], LANES), np.float64)
    s_hi = np.zeros((pos.shape[0], LANES), np.float64)
    start = offset
    while start + d <= LANES:
        cos[:, start:start + half] = np.cos(ang) * scale
        cos[:, start + half:start + d] = np.cos(ang) * scale
        s_lo[:, start:start + half] = -np.sin(ang) * scale
        s_hi[:, start + half:start + d] = np.sin(ang) * scale
        start += group
    return cos, s_lo, s_hi


def _even_tables(pos):
    rq = _rope_tables(pos, RET_DK, LANES, 0, 1.0)
    rk = _rope_tables(pos, RET_DK, LANES, 0, RET_DK ** -0.5)
    c = (MLA_NOPE + MLA_ROPE) ** -0.5 * LOG2E
    mq = _rope_tables(pos, MLA_ROPE, LANES, MLA_NOPE, c)
    mk = _rope_tables(pos, MLA_ROPE, LANES, MLA_NOPE, 1.0)
    tabs = [rq[0], rq[1] + rq[2], rk[0], rk[1] + rk[2], mq[0], mq[1], mq[2], mk[0], mk[1], mk[2]]
    return jnp.asarray(np.stack(tabs).astype(np.float32))


def _odd_tables(pos):
    c = DIFF_HD ** -0.5 * LOG2E
    dq = _rope_tables(pos, DIFF_HD, DIFF_HD, 0, c)
    dk = _rope_tables(pos, DIFF_HD, DIFF_HD, 0, 1.0)
    return jnp.asarray(np.stack(list(dq) + list(dk)).astype(np.float32))


def _retention_tables(L):
    lg = np.asarray(RET_LOG_GAMMA, np.float64)
    idx = np.arange(L, dtype=np.float64)
    diff = idx[:, None] - idx[None, :]
    dmask = np.where(diff[None] >= 0, np.exp(np.maximum(diff, 0.0)[None] * lg[:, None, None]), 0.0)
    qd = np.exp((idx[None, :] + 1.0) * lg[:, None])
    kd = np.exp((L - 1.0 - idx)[None, :] * lg[:, None])
    gl = np.exp(L * lg)
    qd = np.broadcast_to(qd[:, :, None], (RET_HEADS, L, LANES))
    kd = np.broadcast_to(kd[:, :, None], (RET_HEADS, L, LANES))
    gl = np.broadcast_to(gl[:, None, None], (RET_HEADS, RET_DK, RET_DV))
    f = lambda a: jnp.asarray(np.ascontiguousarray(a).astype(np.float32))
    return f(dmask), f(qd), f(kd), f(gl)


def _dot(a, b):
    return jnp.dot(a, b, preferred_element_type=F32)


def _dot_nt(a, b):
    return lax.dot_general(a, b, (((1,), (1,)), ((), ())), preferred_element_type=F32)


def _dot_tn(a, b):
    return lax.dot_general(a, b, (((0,), (0,)), ((), ())), preferred_element_type=F32)


def _rope(x, cos, s_lo, s_hi, half):
    return x * cos + pltpu.roll(x, LANES - half, 1) * s_lo + pltpu.roll(x, half, 1) * s_hi


def _layer_norm(x, g, b):
    mu = jnp.mean(x, -1, keepdims=True)
    xc = x - mu
    var = jnp.mean(xc * xc, -1, keepdims=True)
    return xc * lax.rsqrt(var + LN_EPS) * g + b


def _rms_norm(x, g):
    return x * lax.rsqrt(jnp.mean(x * x, -1, keepdims=True) + NORM_EPS) * g


def _even_in_kernel(x_ref, w_ref, wq_ref, gq_ref, gkv_ref, tab_ref,
                    rq_ref, rk_ref, rv_ref, rg_ref, q_ref, lat_ref, kr_ref):
    xb = x_ref[...].astype(BF16)
    c_rq, s_rq, c_rk, s_rk = tab_ref[0], tab_ref[1], tab_ref[2], tab_ref[3]
    hq = _dot(xb, w_ref[:, 0:RET_W])
    hk = _dot(xb, w_ref[:, RET_W:2 * RET_W])
    for h in range(RET_HEADS):
        sl = slice(h * LANES, (h + 1) * LANES)
        xq = hq[:, sl]
        rq_ref[:, sl] = (xq * c_rq + pltpu.roll(xq, RET_DK // 2, 1) * s_rq).astype(BF16)
        xk = hk[:, sl]
        rk_ref[:, sl] = (xk * c_rk + pltpu.roll(xk, RET_DK // 2, 1) * s_rk).astype(BF16)
    rv_ref[...] = _dot(xb, w_ref[:, 2 * RET_W:3 * RET_W]).astype(BF16)
    rg_ref[...] = _dot(xb, w_ref[:, 3 * RET_W:4 * RET_W]).astype(BF16)
    o = 4 * RET_W
    cq = _dot(xb, w_ref[:, o:o + MLA_Q_RANK])
    qn = _rms_norm(cq, gq_ref[...]).astype(BF16)
    qf = _dot(qn, wq_ref[...])
    c_q, lo_q, hi_q = tab_ref[4], tab_ref[5], tab_ref[6]
    for h in range(MLA_HEADS):
        sl = slice(h * LANES, (h + 1) * LANES)
        q_ref[:, sl] = _rope(qf[:, sl], c_q, lo_q, hi_q, MLA_ROPE // 2).astype(BF16)
    o += MLA_Q_RANK
    ckv = _dot(xb, w_ref[:, o:o + MLA_KV_RANK])
    lat_ref[...] = _rms_norm(ckv, gkv_ref[...])
    o += MLA_KV_RANK
    krp = _dot(xb, w_ref[:, o:o + LANES])
    krp = _rope(krp, tab_ref[7], tab_ref[8], tab_ref[9], MLA_ROPE // 2)
    kr_ref[...] = krp[:, MLA_NOPE:MLA_NOPE + MLA_ROPE]


def _even_in(x, w_a, wq, gq, gkv, tabs, tm):
    T, D = x.shape
    P = tabs.shape[1]
    nt = P // tm
    row = lambda i: (i, 0)
    const = lambda i: (0, 0)
    outs = [
        jax.ShapeDtypeStruct((T, RET_W), BF16), jax.ShapeDtypeStruct((T, RET_W), BF16),
        jax.ShapeDtypeStruct((T, RET_W), BF16), jax.ShapeDtypeStruct((T, RET_W), BF16),
        jax.ShapeDtypeStruct((T, MLA_PAD), BF16),
        jax.ShapeDtypeStruct((T, MLA_KV_RANK), F32), jax.ShapeDtypeStruct((T, MLA_ROPE), F32),
    ]
    return pl.pallas_call(
        _even_in_kernel,
        grid=(T // tm,),
        in_specs=[
            pl.BlockSpec((tm, D), row),
            pl.BlockSpec(w_a.shape, const),
            pl.BlockSpec(wq.shape, const),
            pl.BlockSpec(gq.shape, const),
            pl.BlockSpec(gkv.shape, const),
            pl.BlockSpec((tabs.shape[0], tm, LANES), lambda i: (0, i % nt, 0)),
        ],
        out_specs=[
            pl.BlockSpec((tm, RET_W), row), pl.BlockSpec((tm, RET_W), row),
            pl.BlockSpec((tm, RET_W), row), pl.BlockSpec((tm, RET_W), row),
            pl.BlockSpec((tm, MLA_PAD), row),
            pl.BlockSpec((tm, MLA_KV_RANK), row), pl.BlockSpec((tm, MLA_ROPE), row),
        ],
        out_shape=outs,
        compiler_params=_cparams(("parallel",)),
        name="even_in",
    )(x, w_a, wq, gq, gkv, tabs)


def _kv_up_kernel(lat_ref, kr_ref, wk_ref, wvt_ref, e_ref, k_ref, vt_ref):
    lb = lat_ref[...].astype(BF16)
    krb = kr_ref[...].astype(BF16)
    k_ref[...] = (_dot(lb, wk_ref[...]) + _dot(krb, e_ref[...])).astype(BF16)
    vt_ref[0] = _dot_nt(wvt_ref[...], lb).astype(BF16)


def _vt_layout(T, S, tm, width, dtype=BF16):
    nb, cols = (T // S, S) if S % tm == 0 else (1, T)
    nt = cols // tm
    spec = pl.BlockSpec((1, width, tm), lambda i: (i // nt, 0, i % nt))
    return spec, jax.ShapeDtypeStruct((nb, width, cols), dtype)


def _kv_up(lat, kr, wk, wvt, e_mat, S, tm):
    T = lat.shape[0]
    row = lambda i: (i, 0)
    const = lambda i: (0, 0)
    vt_spec, vt_shape = _vt_layout(T, S, tm, MLA_HEADS * MLA_V)
    return pl.pallas_call(
        _kv_up_kernel,
        grid=(T // tm,),
        in_specs=[
            pl.BlockSpec((tm, MLA_KV_RANK), row), pl.BlockSpec((tm, MLA_ROPE), row),
            pl.BlockSpec(wk.shape, const), pl.BlockSpec(wvt.shape, const), pl.BlockSpec(e_mat.shape, const),
        ],
        out_specs=[pl.BlockSpec((tm, MLA_PAD), row), vt_spec],
        out_shape=[jax.ShapeDtypeStruct((T, MLA_PAD), BF16), vt_shape],
        compiler_params=_cparams(("parallel",)),
        name="kv_up",
    )(lat, kr, wk, wvt, e_mat)


def _retention_kernel(q_ref, k_ref, v_ref, g_ref, s0_ref, dm_ref, qd_ref, kd_ref, gl_ref,
                      o_ref, st_ref, *, L, nchunk):
    @pl.when(pl.program_id(1) == 0)
    def _():
        st_ref[...] = s0_ref[...]

    for c in range(nchunk):
        rows = slice(c * L, (c + 1) * L)
        for h in range(RET_HEADS):
            sl = slice(h * LANES, (h + 1) * LANES)
            q = q_ref[rows, sl]
            k = k_ref[rows, sl]
            v = v_ref[rows, sl]
            st = st_ref[0, h]
            a = (_dot_nt(q, k) * dm_ref[h]).astype(BF16)
            o = _dot(a, v) + _dot(q, st.astype(BF16)) * qd_ref[h]
            kdec = (k.astype(F32) * kd_ref[h]).astype(BF16)
            st_ref[0, h] = st * gl_ref[h] + _dot_tn(kdec, v)
            mu = jnp.mean(o, -1, keepdims=True)
            oc = o - mu
            var = jnp.mean(oc * oc, -1, keepdims=True)
            on = oc * lax.rsqrt(var + LN_EPS)
            g = g_ref[rows, sl].astype(F32)
            o_ref[rows, sl] = (g * jax.nn.sigmoid(g) * on).astype(BF16)


def _retention(rq, rk, rv, rg, state0, B, S):
    L = min(S, 256)
    lt = min(S, 512)
    nj = S // lt
    dm, qd, kd, gl = _retention_tables(L)
    row = lambda b, j: (b * nj + j, 0)
    c3 = lambda b, j: (0, 0, 0)
    st_spec = pl.BlockSpec((1, RET_HEADS, RET_DK, RET_DV), lambda b, j: (b, 0, 0, 0))
    return pl.pallas_call(
        functools.partial(_retention_kernel, L=L, nchunk=lt // L),
        grid=(B, nj),
        in_specs=[pl.BlockSpec((lt, RET_W), row)] * 4 + [
            st_spec,
            pl.BlockSpec(dm.shape, c3), pl.BlockSpec(qd.shape, c3),
            pl.BlockSpec(kd.shape, c3), pl.BlockSpec(gl.shape, c3),
        ],
        out_specs=[pl.BlockSpec((lt, RET_W), row), st_spec],
        out_shape=[jax.ShapeDtypeStruct((B * S, RET_W), BF16),
                   jax.ShapeDtypeStruct((B, RET_HEADS, RET_DK, RET_DV), F32)],
        compiler_params=_cparams(("parallel", "arbitrary")),
        name="retention",
    )(rq, rk, rv, rg, state0, dm, qd, kd, gl)


def _query_t(q, tq):
    q = q.astype(F32)
    if tq < LANES:
        q = jnp.concatenate([q, jnp.zeros((LANES - tq, LANES), F32)], 0)
    return q.T


def _flash_t(streams, qi, tq, tk, sk, sk_valid, causal, m_ref, l_ref, acc_ref):
    for s in range(len(streams)):
        m_ref[s] = jnp.full(m_ref.shape[1:], NEG_INF, F32)
        l_ref[s] = jnp.zeros(l_ref.shape[1:], F32)
        acc_ref[s] = jnp.zeros(acc_ref.shape[1:], F32)

    def step(start, size, rel=None, valid=None):
        for s, (q_t, k_at, vt_at) in enumerate(streams):
            st = _dot(k_at(start, size), q_t)
            if rel is not None:
                kc = (lax.broadcasted_iota(jnp.int32, st.shape, 0) + rel) // CHUNK
                qc = lax.broadcasted_iota(jnp.int32, st.shape, 1) // CHUNK
                st = jnp.where(kc <= qc, st, NEG_INF)
            if valid is not None:
                st = jnp.where(lax.broadcasted_iota(jnp.int32, st.shape, 0) < valid, st, NEG_INF)
            m_old = m_ref[s]
            m_new = jnp.maximum(m_old, jnp.max(st, 0, keepdims=True))
            p = jnp.exp2(st - m_new)
            alpha = jnp.exp2(m_old - m_new)
            l_ref[s] = alpha * l_ref[s] + jnp.sum(p, 0, keepdims=True)
            acc_ref[s] = acc_ref[s] * alpha + _dot(vt_at(start, size), p.astype(BF16))
            m_ref[s] = m_new

    if causal:
        n_full = qi * (tq // tk)

        def body(j, c):
            step(pl.multiple_of(2 * j * tk, tk), tk)
            step(pl.multiple_of((2 * j + 1) * tk, tk), tk)
            return c

        lax.fori_loop(0, n_full // 2, body, 0)

        if (tq // tk) % 2 == 1:
            @pl.when(n_full % 2 == 1)
            def _():
                step(pl.multiple_of((n_full - 1) * tk, tk), tk)

        for d in range(tq // tk):
            step(pl.multiple_of(qi * tq + d * tk, tk), tk, rel=d * tk)
    else:
        for j in range(sk // tk):
            last_valid = sk_valid - j * tk
            step(j * tk, tk, valid=last_valid if last_valid < tk else None)
    return [acc_ref[s] / l_ref[s] for s in range(len(streams))]


def _attn_scratch(n_streams, dv, tq):
    tqp = max(tq, LANES)
    return [pltpu.VMEM((n_streams, 1, tqp), F32), pltpu.VMEM((n_streams, 1, tqp), F32),
            pltpu.VMEM((n_streams, dv, tqp), F32)]


def _mla_attn_kernel(q_ref, k_ref, vt_ref, o_ref, m_ref, l_ref, acc_ref, *, tq, tk, sk, sk_valid, causal):
    streams = []
    for hh in range(2):
        sl = slice(hh * LANES, (hh + 1) * LANES)
        vrows = slice(hh * MLA_V, (hh + 1) * MLA_V)
        k_at = lambda start, n, sl=sl: k_ref[pl.ds(start, n), sl]
        vt_at = lambda start, n, vrows=vrows: vt_ref[0, vrows, pl.ds(start, n)]
        streams.append((_query_t(q_ref[:, sl], tq).astype(BF16), k_at, vt_at))
    outs = _flash_t(streams, pl.program_id(2), tq, tk, sk, sk_valid, causal, m_ref, l_ref, acc_ref)
    o_ref[...] = jnp.concatenate(outs, 0).T[:tq].astype(BF16)


def _mla_attn(q, k, vt, B, sq, sk, sk_valid, tq, tk, causal):
    nq = sq // tq
    npair = MLA_HEADS // 2
    return pl.pallas_call(
        functools.partial(_mla_attn_kernel, tq=tq, tk=tk, sk=sk, sk_valid=sk_valid, causal=causal),
        grid=(B, npair, nq),
        in_specs=[
            pl.BlockSpec((tq, 2 * LANES), lambda b, p, i: (b * nq + i, p)),
            pl.BlockSpec((sk, 2 * LANES), lambda b, p, i: (b, p)),
            pl.BlockSpec((1, 2 * MLA_V, sk), lambda b, p, i: (b, p, 0)),
        ],
        out_specs=pl.BlockSpec((tq, LANES), lambda b, p, i: (b * nq + i, p)),
        out_shape=jax.ShapeDtypeStruct((B * sq, MLA_HEADS * MLA_V), BF16),
        scratch_shapes=_attn_scratch(2, MLA_V, tq),
        compiler_params=_cparams(("parallel", "parallel", "arbitrary")),
        name="mla_attn",
    )(q, k, vt)


def _diff_attn_kernel(lam_ref, gn_ref, q_ref, k_ref, vt_ref, o_ref, m_ref, l_ref, acc_ref,
                      *, tq, tk, sk, sk_valid, causal, lam_init):
    streams = []
    for g in range(DIFF_GROUP):
        sl = slice(g * LANES, (g + 1) * LANES)
        k_at = lambda start, n, sl=sl: k_ref[pl.ds(start, n), sl]
        vt_at = lambda start, n, sl=sl: vt_ref[0, sl, pl.ds(start, n)]
        q_t = _query_t(q_ref[:, sl], tq)
        feat = lax.broadcasted_iota(jnp.int32, q_t.shape, 0)
        zero = jnp.zeros_like(q_t)
        streams.append((jnp.where(feat < DIFF_HD, q_t, zero).astype(BF16), k_at, vt_at))
        streams.append((jnp.where(feat < DIFF_HD, zero, q_t).astype(BF16), k_at, vt_at))
    outs = _flash_t(streams, pl.program_id(2), tq, tk, sk, sk_valid, causal, m_ref, l_ref, acc_ref)
    lv = lam_ref[...]
    lam = (jnp.exp(jnp.sum(lv[0:1] * lv[1:2], -1, keepdims=True))
           - jnp.exp(jnp.sum(lv[2:3] * lv[3:4], -1, keepdims=True)) + lam_init)
    for g in range(DIFF_GROUP):
        o = (outs[2 * g] - lam * outs[2 * g + 1]).T[:tq]
        o_ref[:, g * LANES:(g + 1) * LANES] = (_rms_norm(o, gn_ref[...]) * (1.0 - lam_init)).astype(BF16)


def _diff_attn(lam_vecs, gn, q, k, vt, B, sq, sk, sk_valid, tq, tk, causal, lam_init):
    nq = sq // tq
    const = lambda b, h, i: (0, 0)
    return pl.pallas_call(
        functools.partial(_diff_attn_kernel, tq=tq, tk=tk, sk=sk, sk_valid=sk_valid, causal=causal,
                          lam_init=lam_init),
        grid=(B, DIFF_HEADS // DIFF_GROUP, nq),
        in_specs=[
            pl.BlockSpec(lam_vecs.shape, const), pl.BlockSpec(gn.shape, const),
            pl.BlockSpec((tq, DIFF_GROUP * LANES), lambda b, h, i: (b * nq + i, h)),
            pl.BlockSpec((sk, DIFF_GROUP * LANES), lambda b, h, i: (b, h)),
            pl.BlockSpec((1, DIFF_GROUP * DIFF_V, sk), lambda b, h, i: (b, h, 0)),
        ],
        out_specs=pl.BlockSpec((tq, DIFF_GROUP * LANES), lambda b, h, i: (b * nq + i, h)),
        out_shape=jax.ShapeDtypeStruct((B * sq, DIFF_HEADS * DIFF_V), BF16),
        scratch_shapes=_attn_scratch(2 * DIFF_GROUP, DIFF_V, tq),
        compiler_params=_cparams(("parallel", "parallel", "arbitrary")),
        name="diff_attn",
    )(lam_vecs, gn, q, k, vt)


def _out_proj_kernel(*refs, n_in):
    a_refs = refs[:n_in]
    (w_ref, x_ref, g_ref, b_ref, wr_ref, bias_ref, tri_ref,
     o_ref, gate_ref, route_ref, cnt_out_ref, gt_ref, oh_ref, cnt_ref) = refs[n_in:]
    y = None
    off = 0
    for a_ref in a_refs:
        width = a_ref.shape[1]
        part = _dot(a_ref[...], w_ref[off:off + width, :])
        y = part if y is None else y + part
        off += width
    x1 = _layer_norm(DN_ALPHA * x_ref[...] + y, g_ref[...], b_ref[...])
    o_ref[...] = x1
    _route(x1, wr_ref, bias_ref, tri_ref, gate_ref, route_ref, cnt_out_ref, gt_ref, oh_ref, cnt_ref)


def _out_proj(acts, w, x, g, b, wr_t, bias, tm):
    T, D = x.shape
    tri = jnp.asarray(np.triu(np.ones((tm, tm), np.float32), 1)).astype(BF16)
    row = lambda i: (i, 0)
    const = lambda i: (0, 0)
    return pl.pallas_call(
        functools.partial(_out_proj_kernel, n_in=len(acts)),
        grid=(T // tm,),
        in_specs=[pl.BlockSpec((tm, a.shape[1]), row) for a in acts] + [
            pl.BlockSpec(w.shape, const), pl.BlockSpec((tm, D), row),
            pl.BlockSpec(g.shape, const), pl.BlockSpec(b.shape, const),
            pl.BlockSpec(wr_t.shape, const), pl.BlockSpec(bias.shape, const), pl.BlockSpec(tri.shape, const),
        ],
        out_specs=[pl.BlockSpec((tm, D), row), pl.BlockSpec((tm, LANES), row),
                   pl.BlockSpec((8, tm), lambda i: (0, i)), pl.BlockSpec((SEG_ROWS, LANES), const)],
        out_shape=[jax.ShapeDtypeStruct((T, D), F32), jax.ShapeDtypeStruct((T, LANES), F32),
                   jax.ShapeDtypeStruct((8, T), jnp.int32), jax.ShapeDtypeStruct((SEG_ROWS, LANES), F32)],
        scratch_shapes=[pltpu.VMEM((LANES, tm), F32), pltpu.VMEM((SEG_ROWS, tm), F32),
                        pltpu.VMEM((SEG_ROWS, LANES), F32)],
        compiler_params=_cparams(("arbitrary",)),
        name="out_proj",
    )(*acts, w, x, g, b, wr_t, bias, tri)


def _odd_in_kernel(x_ref, w_ref, tab_ref, q_ref, kt_ref, kb_ref, vf_ref, vt_ref):
    xb = x_ref[...].astype(BF16)
    W = DIFF_HEADS * 2 * DIFF_HD
    hq = _dot(xb, w_ref[:, 0:W])
    hk = _dot(xb, w_ref[:, W:2 * W])
    for h in range(W // LANES):
        sl = slice(h * LANES, (h + 1) * LANES)
        q_ref[:, sl] = _rope(hq[:, sl], tab_ref[0], tab_ref[1], tab_ref[2], DIFF_HD // 2).astype(BF16)
        kk = _rope(hk[:, sl], tab_ref[3], tab_ref[4], tab_ref[5], DIFF_HD // 2)
        kt_ref[0, sl, :] = kk.T
        kb_ref[:, sl] = kk.astype(BF16)
    hv = _dot(xb, w_ref[:, 2 * W:])
    vf_ref[...] = hv
    vt_ref[0] = hv.T.astype(BF16)


def _odd_in(x, w, tabs, S, tm):
    T, D = x.shape
    W = DIFF_HEADS * 2 * DIFF_HD
    nt = tabs.shape[1] // tm
    row = lambda i: (i, 0)
    blk = pl.BlockSpec((tm, W), row)
    vt_spec, vt_shape = _vt_layout(T, S, tm, DIFF_HEADS * DIFF_V)
    kt_spec, kt_shape = _vt_layout(T, S, tm, W, F32)
    return pl.pallas_call(
        _odd_in_kernel,
        grid=(T // tm,),
        in_specs=[pl.BlockSpec((tm, D), row), pl.BlockSpec(w.shape, lambda i: (0, 0)),
                  pl.BlockSpec((tabs.shape[0], tm, LANES), lambda i: (0, i % nt, 0))],
        out_specs=[blk, kt_spec, blk, blk, vt_spec],
        out_shape=[jax.ShapeDtypeStruct((T, W), BF16), kt_shape,
                   jax.ShapeDtypeStruct((T, W), BF16), jax.ShapeDtypeStruct((T, W), F32), vt_shape],
        compiler_params=_cparams(("parallel",)),
        name="odd_in",
    )(x, w, tabs)


def _route(x, wr_ref, bias_ref, tri_ref, g_ref, route_ref, cnt_out_ref, gt_ref, oh_ref, cnt_ref):
    tm = x.shape[0]
    logits = _dot_nt(wr_ref[...], x.astype(BF16))
    sc = jax.nn.sigmoid(logits)
    sel = sc + bias_ref[...]
    r = [sel[e:e + 1, :] for e in range(N_EXPERTS)]
    s = [sc[e:e + 1, :] for e in range(N_EXPERTS)]
    grp = []
    for g in range(N_GROUPS):
        a, b, c, d = r[4 * g:4 * g + 4]
        top2 = jnp.maximum(jnp.maximum(jnp.maximum(a + b, a + c), jnp.maximum(a + d, b + c)),
                           jnp.maximum(b + d, c + d))
        grp.append(top2)
    best = jnp.maximum(jnp.maximum(grp[0], grp[1]), jnp.maximum(grp[2], grp[3]))
    taken = jnp.zeros((1, tm), jnp.bool_)
    chosen = []
    for g in range(N_GROUPS):
        win = jnp.logical_and(grp[g] == best, jnp.logical_not(taken))
        chosen.append(win)
        taken = jnp.logical_or(taken, win)
    picked = []
    for e in range(N_EXPERTS):
        g = e // EXPERTS_PER_GROUP
        rank = jnp.zeros((1, tm), F32)
        for k in range(4 * g, 4 * g + 4):
            if k < e:
                rank = rank + (r[k] >= r[e]).astype(F32)
            elif k > e:
                rank = rank + (r[k] > r[e]).astype(F32)
        picked.append(jnp.logical_and(chosen[g], rank < 2.0))
    w = [jnp.where(picked[e], s[e], 0.0) for e in range(N_EXPERTS)]
    denom = w[0]
    for e in range(1, N_EXPERTS):
        denom = denom + w[e]
    gt_ref[...] = jnp.zeros_like(gt_ref)
    for e in range(N_EXPERTS):
        gt_ref[e:e + 1, :] = w[e] / denom
    g_ref[...] = gt_ref[...].T

    @pl.when(pl.program_id(0) == 0)
    def _():
        cnt_ref[...] = jnp.zeros_like(cnt_ref)

    oh_ref[...] = jnp.zeros_like(oh_ref)
    for g in range(N_GROUPS):
        for p, (a, b) in enumerate(EXPERT_PAIRS):
            both = jnp.logical_and(picked[4 * g + a], picked[4 * g + b])
            oh_ref[g * N_PAIRS + p:g * N_PAIRS + p + 1, :] = both.astype(F32)
    oh = oh_ref[...]
    before = _dot(oh.astype(BF16), tri_ref[...])
    base = cnt_ref[:, 0:1]
    rank = jnp.sum(oh * (base + before), 0, keepdims=True)
    sid = jnp.sum(oh * lax.broadcasted_iota(jnp.int32, oh.shape, 0).astype(F32), 0, keepdims=True)
    row = lax.broadcasted_iota(jnp.int32, route_ref.shape, 0)
    route_ref[...] = jnp.where(row == 0, sid, jnp.where(row == 1, rank, 0.0)).astype(jnp.int32)
    cnt_ref[...] = cnt_ref[...] + jnp.sum(oh, 1, keepdims=True)
    cnt_out_ref[...] = cnt_ref[...]


def _moe_dense_kernel(x_ref, gate_ref, wg_ref, wu_ref, wd_ref, g_ref, b_ref, o_ref, xb_ref, acc_ref):
    e = pl.program_id(1)

    @pl.when(e == 0)
    def _():
        xb_ref[...] = x_ref[...].astype(BF16)
        acc_ref[...] = jnp.zeros_like(acc_ref)

    xb = xb_ref[...]
    h = jax.nn.silu(_dot(xb, wg_ref[0, 0].astype(BF16))) * _dot(xb, wu_ref[0, 0].astype(BF16))
    y = _dot(h.astype(BF16), wd_ref[0, 0].astype(BF16))
    gates = gate_ref[...]
    lane = lax.broadcasted_iota(jnp.int32, gates.shape, 1)
    gcol = jnp.sum(jnp.where(lane == e, gates, 0.0), -1, keepdims=True)
    acc_ref[...] += gcol * y

    @pl.when(e == pl.num_programs(1) - 1)
    def _():
        o_ref[...] = _layer_norm(DN_ALPHA * x_ref[...] + acc_ref[...], g_ref[...], b_ref[...])


def _moe_dense(x, gates, wg, wu, wd, l, g, b, tm):
    T, D = x.shape
    _, E, _, H = wg.shape
    row = lambda i, e: (i, 0)
    const = lambda i, e: (0, 0)
    wsel = lambda i, e: (l, e, 0, 0)
    return pl.pallas_call(
        _moe_dense_kernel,
        grid=(T // tm, E),
        in_specs=[
            pl.BlockSpec((tm, D), row), pl.BlockSpec((tm, LANES), row),
            pl.BlockSpec((1, 1, D, H), wsel), pl.BlockSpec((1, 1, D, H), wsel),
            pl.BlockSpec((1, 1, H, D), wsel),
            pl.BlockSpec(g.shape, const), pl.BlockSpec(b.shape, const),
        ],
        out_specs=pl.BlockSpec((tm, D), row),
        out_shape=jax.ShapeDtypeStruct((T, D), F32),
        scratch_shapes=[pltpu.VMEM((tm, D), BF16), pltpu.VMEM((tm, D), F32)],
        compiler_params=_cparams(("parallel", "arbitrary")),
        name="moe_dense",
    )(x, gates, wg, wu, wd, g, b)


SC_CORES = 2
SC_SUBCORES = 16
SC_WORKERS = SC_CORES * SC_SUBCORES
SC_TILE_BYTES = 384 * 1024
SPARSE_ROW_TILE = 512
SPARSE_MIN_TOKENS = 4096


def _sc_ring(per_w, row_bytes):
    for ch, nbuf in ((16, 4), (16, 2), (8, 2)):
        if per_w % (ch * nbuf) == 0 and ch * nbuf * row_bytes <= SC_TILE_BYTES:
            return ch, nbuf
    raise ValueError(f"no SparseCore gather ring for {per_w} rows of {row_bytes} bytes per subcore")


def _sc_gather(table, idx):
    R = idx.shape[0]
    D = table.shape[1]
    per_w = R // SC_WORKERS
    assert per_w * SC_WORKERS == R and per_w % 8 == 0
    ch, nbuf = _sc_ring(per_w, D * table.dtype.itemsize)
    nchunk = per_w // ch
    mesh = plsc.VectorSubcoreMesh(core_axis_name="c", subcore_axis_name="s")

    @functools.partial(
        pl.kernel, mesh=mesh,
        out_type=jax.ShapeDtypeStruct((R, D), table.dtype),
        scratch_types=[pltpu.VMEM((per_w,), jnp.int32), pltpu.VMEM((nbuf, ch, D), table.dtype),
                       pltpu.SemaphoreType.DMA((nbuf,)), pltpu.SemaphoreType.DMA((nbuf,))],
    )
    def gather_kernel(table_hbm, idx_hbm, out_hbm, idx_v, rows_v, gsem, wsem):
        base = (lax.axis_index("s") * SC_CORES + lax.axis_index("c")) * per_w
        pltpu.sync_copy(idx_hbm.at[pl.ds(base, per_w)], idx_v)

        def gather(c, b):
            return pltpu.make_async_copy(table_hbm.at[idx_v.at[pl.ds(c * ch, ch)]], rows_v.at[b], gsem.at[b])

        def write(c, b):
            return pltpu.make_async_copy(rows_v.at[b], out_hbm.at[pl.ds(base + c * ch, ch)], wsem.at[b])

        for b in range(nbuf - 1):
            gather(b, b).start()

        @pl.loop(0, nchunk, step=nbuf)
        def _(c):
            for b in range(nbuf):
                cc = c + b
                gather(cc, b).wait()
                write(cc, b).start()
                pb = (b - 1) % nbuf

                @pl.when(cc + nbuf - 1 < nchunk)
                def _():
                    @pl.when(cc >= 1)
                    def _():
                        write(cc - 1, pb).wait()

                    gather(cc + nbuf - 1, pb).start()

        for b in range(nbuf):
            write(nchunk - nbuf + b, b).wait()

    return gather_kernel(table, idx)


def _moe_group_kernel(te_ref, nv_ref, x_ref, gate_ref, *refs):
    w_refs, (g_ref, b_ref, o_ref) = refs[:3 * TOP_K], refs[3 * TOP_K:]
    j = pl.program_id(0)
    live = j < nv_ref[0]

    @pl.when(live)
    def _():
        x = x_ref[...]
        xb = x.astype(BF16)
        gates = gate_ref[...]
        lane = lax.broadcasted_iota(jnp.int32, gates.shape, 1)
        acc = None
        for k in range(TOP_K):
            wg_ref, wu_ref, wd_ref = w_refs[3 * k:3 * k + 3]
            h = jax.nn.silu(_dot(xb, wg_ref[0, 0].astype(BF16))) * _dot(xb, wu_ref[0, 0].astype(BF16))
            y = _dot(h.astype(BF16), wd_ref[0, 0].astype(BF16))
            gcol = jnp.sum(jnp.where(lane == te_ref[TOP_K * j + k], gates, 0.0), -1, keepdims=True)
            acc = gcol * y if acc is None else acc + gcol * y
        o_ref[...] = _layer_norm(DN_ALPHA * x + acc, g_ref[...], b_ref[...])

    @pl.when(jnp.logical_not(live))
    def _():
        o_ref[...] = jnp.zeros_like(o_ref)


def _moe_group(tile_experts, n_valid, xs, gs, wg, wu, wd, l, g, b, tm):
    R, D = xs.shape
    H = wg.shape[3]
    row = lambda j, te, nv: (j, 0)
    const = lambda j, te, nv: (0, 0)
    w_specs, w_args = [], []
    for k in range(TOP_K):
        wsel = lambda j, te, nv, k=k: (l, te[TOP_K * j + k], 0, 0)
        w_specs += [pl.BlockSpec((1, 1, D, H), wsel), pl.BlockSpec((1, 1, D, H), wsel),
                    pl.BlockSpec((1, 1, H, D), wsel)]
        w_args += [wg, wu, wd]
    grid_spec = pltpu.PrefetchScalarGridSpec(
        num_scalar_prefetch=2,
        grid=(R // tm,),
        in_specs=[pl.BlockSpec((tm, D), row), pl.BlockSpec((tm, LANES), row)] + w_specs + [
            pl.BlockSpec(g.shape, const), pl.BlockSpec(b.shape, const)],
        out_specs=pl.BlockSpec((tm, D), row),
    )
    return pl.pallas_call(
        _moe_group_kernel,
        grid_spec=grid_spec,
        out_shape=jax.ShapeDtypeStruct((R, D), F32),
        compiler_params=_cparams(("arbitrary",)),
        name="moe_group",
    )(tile_experts, n_valid, xs, gs, *w_args, g, b)


def _sc_scatter(x, dest, pad_dest, n_rows):
    T, D = x.shape
    P = pad_dest.shape[0]
    per_w, pad_w = T // SC_WORKERS, P // SC_WORKERS
    assert per_w * SC_WORKERS == T and pad_w * SC_WORKERS == P and T + P == n_rows
    ch, nbuf = _sc_ring(per_w, D * x.dtype.itemsize)
    assert pad_w % ch == 0
    nchunk, npad = per_w // ch, pad_w // ch
    mesh = plsc.VectorSubcoreMesh(core_axis_name="c", subcore_axis_name="s")

    @functools.partial(
        pl.kernel, mesh=mesh,
        out_type=jax.ShapeDtypeStruct((n_rows, D), x.dtype),
        scratch_types=[pltpu.VMEM((nchunk, ch), jnp.int32), pltpu.VMEM((npad, ch), jnp.int32),
                       pltpu.VMEM((nbuf, ch, D), x.dtype), pltpu.VMEM((ch, D), x.dtype),
                       pltpu.SemaphoreType.DMA((nbuf,)), pltpu.SemaphoreType.DMA((nbuf,)),
                       pltpu.SemaphoreType.DMA],
    )
    def scatter_kernel(x_hbm, dest_hbm, pad_hbm, zero_hbm, out_hbm, dest_v, pad_v, rows_v, zero_v,
                       rsem, wsem, zsem):
        wid = lax.axis_index("s") * SC_CORES + lax.axis_index("c")
        base = wid * per_w
        pltpu.sync_copy(dest_hbm.at[wid], dest_v)
        pltpu.sync_copy(pad_hbm.at[wid], pad_v)
        pltpu.sync_copy(zero_hbm, zero_v)

        def read(c, b):
            return pltpu.make_async_copy(x_hbm.at[pl.ds(base + c * ch, ch)], rows_v.at[b], rsem.at[b])

        def write(c, b):
            return pltpu.make_async_copy(rows_v.at[b], out_hbm.at[dest_v.at[c]], wsem.at[b])

        def write_zero(c):
            return pltpu.make_async_copy(zero_v, out_hbm.at[pad_v.at[c]], zsem)

        for c in range(npad):
            write_zero(c).start()
        for b in range(nbuf - 1):
            read(b, b).start()

        @pl.loop(0, nchunk, step=nbuf)
        def _(c):
            for b in range(nbuf):
                cc = c + b
                read(cc, b).wait()
                write(cc, b).start()
                pb = (b - 1) % nbuf

                @pl.when(cc + nbuf - 1 < nchunk)
                def _():
                    @pl.when(cc >= 1)
                    def _():
                        write(cc - 1, pb).wait()

                    read(cc + nbuf - 1, pb).start()

        for b in range(nbuf):
            write(nchunk - nbuf + b, b).wait()
        for c in range(npad):
            write_zero(c).wait()

    return scatter_kernel(x, dest.reshape(SC_WORKERS, nchunk, ch), pad_dest.reshape(SC_WORKERS, npad, ch),
                          jnp.zeros((ch, D), x.dtype))


def _moe_sparse(x, gates, route, counts, wg, wu, wd, l, g, b):
    T = x.shape[0]
    tm = SPARSE_ROW_TILE
    n_tiles = T // tm + N_SEG
    n_rows = n_tiles * tm
    sid, rank = route[0], route[1]
    cnt = counts[:N_SEG, 0].astype(jnp.int32)
    tiles = (cnt + tm - 1) // tm
    ends = jnp.cumsum(tiles)
    starts = (ends - tiles) * tm
    seg_ids = jnp.arange(N_SEG, dtype=jnp.int32)
    pick = lambda which, vals: jnp.sum(jnp.where(which[:, None] == seg_ids[None, :], vals[None, :], 0), 1)
    passed = lambda pos, bounds: jnp.sum((pos[:, None] >= bounds[None, :]).astype(jnp.int32), 1)
    dest = rank + pick(sid, starts)
    pad_cnt = tiles * tm - cnt
    pad_end = jnp.cumsum(pad_cnt)
    p = jnp.arange(n_rows - T, dtype=jnp.int32)
    seg = passed(p, pad_end)
    pad_dest = jnp.where(seg < N_SEG,
                         pick(seg, starts + cnt) + p - pick(seg, pad_end - pad_cnt),
                         ends[N_SEG - 1] * tm + p - pad_end[N_SEG - 1])
    tile_seg = jnp.minimum(passed(jnp.arange(n_tiles, dtype=jnp.int32), ends), N_SEG - 1)
    pair = jnp.asarray(np.asarray(EXPERT_PAIRS, np.int32))
    tile_experts = (tile_seg // N_PAIRS * EXPERTS_PER_GROUP)[:, None] + pair[tile_seg % N_PAIRS]
    xs = _sc_scatter(x, dest, pad_dest, n_rows)
    gs = _sc_scatter(gates, dest, pad_dest, n_rows)
    ys = _moe_group(tile_experts.reshape(-1), ends[N_SEG - 1:], xs, gs, wg, wu, wd, l, g, b, tm)
    return _sc_gather(ys, dest)


def _prep_weights(w_in_even, w_uq, w_ukv, w_out_even, w_in_odd, w_out_odd, w_router,
                  w_expert_gate, w_expert_up, w_expert_down):
    d = w_in_even.shape[1]
    n_main = 4 * RET_W + MLA_Q_RANK + MLA_KV_RANK
    w_in = w_in_even[0]
    kr_cols = jnp.pad(w_in[:, n_main:], ((0, 0), (MLA_NOPE, LANES - MLA_NOPE - MLA_ROPE)))
    w_a = jnp.concatenate([w_in[:, :n_main], kr_cols], 1).astype(BF16)
    qd = MLA_NOPE + MLA_ROPE
    wq = jnp.pad(w_uq[0].reshape(MLA_Q_RANK, MLA_HEADS, qd), ((0, 0), (0, 0), (0, LANES - qd)))
    wq = wq.reshape(MLA_Q_RANK, MLA_PAD).astype(BF16)
    wkv = w_ukv[0].reshape(MLA_KV_RANK, MLA_HEADS, MLA_NOPE + MLA_V)
    wk = jnp.pad(wkv[:, :, :MLA_NOPE], ((0, 0), (0, 0), (0, LANES - MLA_NOPE)))
    wk = wk.reshape(MLA_KV_RANK, MLA_PAD).astype(BF16)
    wvt = wkv[:, :, MLA_NOPE:].reshape(MLA_KV_RANK, MLA_HEADS * MLA_V).T.astype(BF16)
    e_np = np.zeros((MLA_ROPE, MLA_HEADS, LANES), np.float32)
    for j in range(MLA_ROPE):
        e_np[j, :, MLA_NOPE + j] = 1.0
    e_mat = jnp.asarray(e_np.reshape(MLA_ROPE, MLA_PAD)).astype(BF16)
    return dict(
        w_a=w_a, wq=wq, wk=wk, wvt=wvt, e_mat=e_mat,
        w_out_even=w_out_even[0].astype(BF16), w_in_odd=w_in_odd[0].astype(BF16),
        w_out_odd=w_out_odd[0].astype(BF16), wr_t=w_router.T.astype(BF16),
        wg=w_expert_gate, wu=w_expert_up, wd=w_expert_down,
    )


def _moe(routed, wts, l, ln_g, ln_b, tm):
    x, gates, route, counts = routed
    g, b = ln_g[l, 1][None], ln_b[l, 1][None]
    if x.shape[0] >= SPARSE_MIN_TOKENS:
        return _moe_sparse(x, gates, route, counts, wts["wg"], wts["wu"], wts["wd"], l, g, b)
    return _moe_dense(x, gates, wts["wg"], wts["wu"], wts["wd"], l, g, b, tm)


def _trunk(x3, pos0, past, wts, prm):
    B, S, D = x3.shape
    T = B * S
    x = x3.reshape(T, D)
    tm = min(T, 1024)
    rep = max(tm // S, 1)
    pos = np.tile(pos0 + np.arange(S), rep)
    ln_g, ln_b = prm["ln_g"], prm["ln_b"]

    rq, rk, rv, rg, q, lat, kr = _even_in(x, wts["w_a"], wts["wq"], prm["gq"], prm["gkv"], _even_tables(pos), tm)
    causal = past is None
    if causal:
        state0 = jnp.zeros((B, RET_HEADS, RET_DK, RET_DV), F32)
        lat_all, kr_all, sk, skp = lat, kr, S, S
        tq, tk = min(S, 1024), min(S, 1024)
    else:
        state0 = past["state"]
        sk = past["lat"].shape[1] + S
        skp = -(-sk // LANES) * LANES
        padk = lambda parts: jnp.concatenate(
            parts + [jnp.zeros((B, skp - sk, parts[0].shape[2]), parts[0].dtype)], 1).reshape(B * skp, -1)
        lat_all = padk([past["lat"], lat.reshape(B, S, -1)])
        kr_all = padk([past["kr"], kr.reshape(B, S, -1)])
        tq, tk = S, skp
    ret_out, ret_state = _retention(rq, rk, rv, rg, state0, B, S)
    tkv = 512 if (B * skp) % 512 == 0 else skp
    k_mla, vt_mla = _kv_up(lat_all, kr_all, wts["wk"], wts["wvt"], wts["e_mat"], skp, tkv)
    if vt_mla.shape[0] != B:
        vt_mla = vt_mla.reshape(-1, B, skp).transpose(1, 0, 2)
    mla_out = _mla_attn(q, k_mla, vt_mla, B, S, skp, sk, tq, tk, causal)
    routed = _out_proj([ret_out, mla_out], wts["w_out_even"], x, ln_g[0, 0][None], ln_b[0, 0][None],
                       wts["wr_t"], prm["bias"], tm)
    x = _moe(routed, wts, 0, ln_g, ln_b, tm)

    qd, kt, kb, vf, vt = _odd_in(x, wts["w_in_odd"], _odd_tables(pos), S, tm)
    if vt.shape[0] != B:
        vt = vt.reshape(-1, B, S).transpose(1, 0, 2)
        kt = kt.reshape(-1, B, S).transpose(1, 0, 2)
    kf = kt.reshape(1, B, 2 * DIFF_HEADS, DIFF_HD, S).transpose(0, 1, 4, 2, 3)
    if causal:
        k_all, vt_all = kb, vt
    else:
        k_all = padk([past["dk"], kb.reshape(B, S, -1)])
        vt_all = jnp.concatenate([past["dv"].transpose(0, 2, 1), vt,
                                  jnp.zeros((B, vt.shape[1], skp - sk), BF16)], 2)
    lam_init = 0.8 - 0.6 * math.exp(-0.3 * 1)
    d_out = _diff_attn(prm["lam"], prm["gn"], qd, k_all, vt_all, B, S, skp, sk, tq, tk, causal, lam_init)
    routed = _out_proj([d_out], wts["w_out_odd"], x, ln_g[1, 0][None], ln_b[1, 0][None],
                       wts["wr_t"], prm["bias"], tm)
    x = _moe(routed, wts, 1, ln_g, ln_b, tm)

    return (x.reshape(B, S, D), ret_state[None], lat.reshape(1, B, S, -1), kr.reshape(1, B, S, -1),
            kf, vf.reshape(1, B, S, DIFF_HEADS, DIFF_V))


def kernel(x_prompt, x_sample, state_ret, cache_mla_latent, cache_mla_krope, cache_diff_k, cache_diff_v,
           w_in_even, w_uq, w_ukv, g_qnorm, g_kvnorm, w_out_even,
           w_in_odd, lambda_q1, lambda_k1, lambda_q2, lambda_k2, g_diff_norm, w_out_odd,
           ln_g, ln_b, w_router, router_bias, w_expert_gate, w_expert_up, w_expert_down):
    wts = _prep_weights(w_in_even, w_uq, w_ukv, w_out_even, w_in_odd, w_out_odd, w_router,
                        w_expert_gate, w_expert_up, w_expert_down)
    prm = dict(
        gq=g_qnorm[0][None].astype(F32), gkv=g_kvnorm[0][None].astype(F32),
        lam=jnp.stack([lambda_q1[0], lambda_k1[0], lambda_q2[0], lambda_k2[0]]).astype(F32),
        gn=g_diff_norm[0][None].astype(F32), bias=router_bias.reshape(N_EXPERTS, 1).astype(F32),
        ln_g=ln_g.astype(F32), ln_b=ln_b.astype(F32),
    )
    past_len = cache_mla_latent.shape[2]
    db = x_sample.shape[0]
    past = dict(
        state=state_ret[0].astype(F32), lat=cache_mla_latent[0], kr=cache_mla_krope[0],
        dk=cache_diff_k[0].reshape(db, past_len, -1).astype(BF16),
        dv=cache_diff_v[0].reshape(db, past_len, -1).astype(BF16),
    )
    outs_p = _trunk(x_prompt, 0, None, wts, prm)
    outs_s = _trunk(x_sample, past_len, past, wts, prm)
    return (outs_p[0], outs_s[0]) + outs_p[1:] + outs_s[1:]
```

```python
import functools
import math

import numpy as np
import jax
import jax.numpy as jnp
from jax import lax
from jax.experimental import pallas as pl
from jax.experimental.pallas import tpu as pltpu
from jax.experimental.pallas import tpu_sc as plsc

F32 = jnp.float32
BF16 = jnp.bfloat16

CHUNK = 64
ROPE_THETA = 10000.0
NEG_INF = -1e30
LN_EPS = 1e-5
NORM_EPS = 1e-6
DEPTH = 2
DN_ALPHA = (2.0 * DEPTH) ** 0.25
RET_HEADS = 4
RET_DK = 128
RET_DV = 128
RET_LOG_GAMMA = tuple(math.log(1.0 - 2.0 ** (-5 - h)) for h in range(RET_HEADS))
MLA_HEADS = 8
MLA_Q_RANK = 384
MLA_KV_RANK = 256
MLA_NOPE = 64
MLA_ROPE = 32
MLA_V = 64
DIFF_HEADS = 8
DIFF_HD = 64
DIFF_V = 128
DIFF_GROUP = 1
N_EXPERTS = 16
N_GROUPS = 4
EXPERTS_PER_GROUP = 4
TOP_K = 2
EXPERT_PAIRS = tuple((a, b) for a in range(EXPERTS_PER_GROUP) for b in range(a + 1, EXPERTS_PER_GROUP))
N_PAIRS = len(EXPERT_PAIRS)
N_SEG = N_GROUPS * N_PAIRS
SEG_ROWS = 32
LOG2E = math.log2(math.e)

LANES = 128
RET_W = RET_HEADS * RET_DK
MLA_PAD = MLA_HEADS * LANES
VMEM_LIMIT = 56 * 1024 * 1024


def _cparams(sem):
    return pltpu.CompilerParams(dimension_semantics=sem, vmem_limit_bytes=VMEM_LIMIT)


def _rope_tables(pos, d, group, offset, scale):
    pos = np.asarray(pos, np.float64)
    half = d // 2
    inv = 1.0 / (ROPE_THETA ** (np.arange(0, d, 2, dtype=np.float64) / d))
    ang = pos[:, None] * inv[None, :]
    cos = np.full((pos.shape[0], LANES), scale, np.float64)
    s_lo = np.zeros((pos.shape[0], LANES), np.float64)
    s_hi = np.zeros((pos.shape[0], LANES), np.float64)
    start = offset
    while start + d <= LANES:
        cos[:, start:start + half] = np.cos(ang) * scale
        cos[:, start + half:start + d] = np.cos(ang) * scale
        s_lo[:, start:start + half] = -np.sin(ang) * scale
        s_hi[:, start + half:start + d] = np.sin(ang) * scale
        start += group
    return cos, s_lo, s_hi


def _even_tables(pos):
    rq = _rope_tables(pos, RET_DK, LANES, 0, 1.0)
    rk = _rope_tables(pos, RET_DK, LANES, 0, RET_DK ** -0.5)
    c = (MLA_NOPE + MLA_ROPE) ** -0.5 * LOG2E
    mq = _rope_tables(pos, MLA_ROPE, LANES, MLA_NOPE, c)
    mk = _rope_tables(pos, MLA_ROPE, LANES, MLA_NOPE, 1.0)
    tabs = [rq[0], rq[1] + rq[2], rk[0], rk[1] + rk[2], mq[0], mq[1], mq[2], mk[0], mk[1], mk[2]]
    return jnp.asarray(np.stack(tabs).astype(np.float32))


def _odd_tables(pos):
    c = DIFF_HD ** -0.5 * LOG2E
    dq = _rope_tables(pos, DIFF_HD, DIFF_HD, 0, c)
    dk = _rope_tables(pos, DIFF_HD, DIFF_HD, 0, 1.0)
    return jnp.asarray(np.stack(list(dq) + list(dk)).astype(np.float32))


def _retention_tables(L):
    lg = np.asarray(RET_LOG_GAMMA, np.float64)
    idx = np.arange(L, dtype=np.float64)
    diff = idx[:, None] - idx[None, :]
    dmask = np.where(diff[None] >= 0, np.exp(np.maximum(diff, 0.0)[None] * lg[:, None, None]), 0.0)
    qd = np.exp((idx[None, :] + 1.0) * lg[:, None])
    kd = np.exp((L - 1.0 - idx)[None, :] * lg[:, None])
    gl = np.exp(L * lg)
    qd = np.broadcast_to(qd[:, :, None], (RET_HEADS, L, LANES))
    kd = np.broadcast_to(kd[:, :, None], (RET_HEADS, L, LANES))
    gl = np.broadcast_to(gl[:, None, None], (RET_HEADS, RET_DK, RET_DV))
    f = lambda a: jnp.asarray(np.ascontiguousarray(a).astype(np.float32))
    return f(dmask), f(qd), f(kd), f(gl)


def _dot(a, b):
    return jnp.dot(a, b, preferred_element_type=F32)


def _dot_nt(a, b):
    return lax.dot_general(a, b, (((1,), (1,)), ((), ())), preferred_element_type=F32)


def _dot_tn(a, b):
    return lax.dot_general(a, b, (((0,), (0,)), ((), ())), preferred_element_type=F32)


def _rope(x, cos, s_lo, s_hi, half):
    return x * cos + pltpu.roll(x, LANES - half, 1) * s_lo + pltpu.roll(x, half, 1) * s_hi


def _layer_norm(x, g, b):
    mu = jnp.mean(x, -1, keepdims=True)
    xc = x - mu
    var = jnp.mean(xc * xc, -1, keepdims=True)
    return xc * lax.rsqrt(var + LN_EPS) * g + b


def _rms_norm(x, g):
    return x * lax.rsqrt(jnp.mean(x * x, -1, keepdims=True) + NORM_EPS) * g


def _even_in_kernel(x_ref, w_ref, wq_ref, gq_ref, gkv_ref, tab_ref,
                    rq_ref, rk_ref, rv_ref, rg_ref, q_ref, lat_ref, kr_ref):
    xb = x_ref[...].astype(BF16)
    c_rq, s_rq, c_rk, s_rk = tab_ref[0], tab_ref[1], tab_ref[2], tab_ref[3]
    hq = _dot(xb, w_ref[:, 0:RET_W])
    hk = _dot(xb, w_ref[:, RET_W:2 * RET_W])
    for h in range(RET_HEADS):
        sl = slice(h * LANES, (h + 1) * LANES)
        xq = hq[:, sl]
        rq_ref[:, sl] = (xq * c_rq + pltpu.roll(xq, RET_DK // 2, 1) * s_rq).astype(BF16)
        xk = hk[:, sl]
        rk_ref[:, sl] = (xk * c_rk + pltpu.roll(xk, RET_DK // 2, 1) * s_rk).astype(BF16)
    rv_ref[...] = _dot(xb, w_ref[:, 2 * RET_W:3 * RET_W]).astype(BF16)
    rg_ref[...] = _dot(xb, w_ref[:, 3 * RET_W:4 * RET_W]).astype(BF16)
    o = 4 * RET_W
    cq = _dot(xb, w_ref[:, o:o + MLA_Q_RANK])
    qn = _rms_norm(cq, gq_ref[...]).astype(BF16)
    qf = _dot(qn, wq_ref[...])
    c_q, lo_q, hi_q = tab_ref[4], tab_ref[5], tab_ref[6]
    for h in range(MLA_HEADS):
        sl = slice(h * LANES, (h + 1) * LANES)
        q_ref[:, sl] = _rope(qf[:, sl], c_q, lo_q, hi_q, MLA_ROPE // 2).astype(BF16)
    o += MLA_Q_RANK
    ckv = _dot(xb, w_ref[:, o:o + MLA_KV_RANK])
    lat_ref[...] = _rms_norm(ckv, gkv_ref[...])
    o += MLA_KV_RANK
    krp = _dot(xb, w_ref[:, o:o + LANES])
    krp = _rope(krp, tab_ref[7], tab_ref[8], tab_ref[9], MLA_ROPE // 2)
    kr_ref[...] = krp[:, MLA_NOPE:MLA_NOPE + MLA_ROPE]


def _even_in(x, w_a, wq, gq, gkv, tabs, tm):
    T, D = x.shape
    P = tabs.shape[1]
    nt = P // tm
    row = lambda i: (i, 0)
    const = lambda i: (0, 0)
    outs = [
        jax.ShapeDtypeStruct((T, RET_W), BF16), jax.ShapeDtypeStruct((T, RET_W), BF16),
        jax.ShapeDtypeStruct((T, RET_W), BF16), jax.ShapeDtypeStruct((T, RET_W), BF16),
        jax.ShapeDtypeStruct((T, MLA_PAD), BF16),
        jax.ShapeDtypeStruct((T, MLA_KV_RANK), F32), jax.ShapeDtypeStruct((T, MLA_ROPE), F32),
    ]
    return pl.pallas_call(
        _even_in_kernel,
        grid=(T // tm,),
        in_specs=[
            pl.BlockSpec((tm, D), row),
            pl.BlockSpec(w_a.shape, const),
            pl.BlockSpec(wq.shape, const),
            pl.BlockSpec(gq.shape, const),
            pl.BlockSpec(gkv.shape, const),
            pl.BlockSpec((tabs.shape[0], tm, LANES), lambda i: (0, i % nt, 0)),
        ],
        out_specs=[
            pl.BlockSpec((tm, RET_W), row), pl.BlockSpec((tm, RET_W), row),
            pl.BlockSpec((tm, RET_W), row), pl.BlockSpec((tm, RET_W), row),
            pl.BlockSpec((tm, MLA_PAD), row),
            pl.BlockSpec((tm, MLA_KV_RANK), row), pl.BlockSpec((tm, MLA_ROPE), row),
        ],
        out_shape=outs,
        compiler_params=_cparams(("parallel",)),
        name="even_in",
    )(x, w_a, wq, gq, gkv, tabs)


def _kv_up_kernel(lat_ref, kr_ref, wk_ref, wvt_ref, e_ref, k_ref, vt_ref):
    lb = lat_ref[...].astype(BF16)
    krb = kr_ref[...].astype(BF16)
    k_ref[...] = (_dot(lb, wk_ref[...]) + _dot(krb, e_ref[...])).astype(BF16)
    vt_ref[0] = _dot_nt(wvt_ref[...], lb).astype(BF16)


def _vt_layout(T, S, tm, width, dtype=BF16):
    nb, cols = (T // S, S) if S % tm == 0 else (1, T)
    nt = cols // tm
    spec = pl.BlockSpec((1, width, tm), lambda i: (i // nt, 0, i % nt))
    return spec, jax.ShapeDtypeStruct((nb, width, cols), dtype)


def _kv_up(lat, kr, wk, wvt, e_mat, S, tm):
    T = lat.shape[0]
    row = lambda i: (i, 0)
    const = lambda i: (0, 0)
    vt_spec, vt_shape = _vt_layout(T, S, tm, MLA_HEADS * MLA_V)
    return pl.pallas_call(
        _kv_up_kernel,
        grid=(T // tm,),
        in_specs=[
            pl.BlockSpec((tm, MLA_KV_RANK), row), pl.BlockSpec((tm, MLA_ROPE), row),
            pl.BlockSpec(wk.shape, const), pl.BlockSpec(wvt.shape, const), pl.BlockSpec(e_mat.shape, const),
        ],
        out_specs=[pl.BlockSpec((tm, MLA_PAD), row), vt_spec],
        out_shape=[jax.ShapeDtypeStruct((T, MLA_PAD), BF16), vt_shape],
        compiler_params=_cparams(("parallel",)),
        name="kv_up",
    )(lat, kr, wk, wvt, e_mat)


def _retention_kernel(q_ref, k_ref, v_ref, g_ref, s0_ref, dm_ref, qd_ref, kd_ref, gl_ref,
                      o_ref, st_ref, *, L, nchunk):
    @pl.when(pl.program_id(1) == 0)
    def _():
        st_ref[...] = s0_ref[...]

    for c in range(nchunk):
        rows = slice(c * L, (c + 1) * L)
        for h in range(RET_HEADS):
            sl = slice(h * LANES, (h + 1) * LANES)
            q = q_ref[rows, sl]
            k = k_ref[rows, sl]
            v = v_ref[rows, sl]
            st = st_ref[0, h]
            a = (_dot_nt(q, k) * dm_ref[h]).astype(BF16)
            o = _dot(a, v) + _dot(q, st.astype(BF16)) * qd_ref[h]
            kdec = (k.astype(F32) * kd_ref[h]).astype(BF16)
            st_ref[0, h] = st * gl_ref[h] + _dot_tn(kdec, v)
            mu = jnp.mean(o, -1, keepdims=True)
            oc = o - mu
            var = jnp.mean(oc * oc, -1, keepdims=True)
            on = oc * lax.rsqrt(var + LN_EPS)
            g = g_ref[rows, sl].astype(F32)
            o_ref[rows, sl] = (g * jax.nn.sigmoid(g) * on).astype(BF16)


def _retention(rq, rk, rv, rg, state0, B, S):
    L = min(S, 256)
    lt = min(S, 512)
    nj = S // lt
    dm, qd, kd, gl = _retention_tables(L)
    row = lambda b, j: (b * nj + j, 0)
    c3 = lambda b, j: (0, 0, 0)
    st_spec = pl.BlockSpec((1, RET_HEADS, RET_DK, RET_DV), lambda b, j: (b, 0, 0, 0))
    return pl.pallas_call(
        functools.partial(_retention_kernel, L=L, nchunk=lt // L),
        grid=(B, nj),
        in_specs=[pl.BlockSpec((lt, RET_W), row)] * 4 + [
            st_spec,
            pl.BlockSpec(dm.shape, c3), pl.BlockSpec(qd.shape, c3),
            pl.BlockSpec(kd.shape, c3), pl.BlockSpec(gl.shape, c3),
        ],
        out_specs=[pl.BlockSpec((lt, RET_W), row), st_spec],
        out_shape=[jax.ShapeDtypeStruct((B * S, RET_W), BF16),
                   jax.ShapeDtypeStruct((B, RET_HEADS, RET_DK, RET_DV), F32)],
        compiler_params=_cparams(("parallel", "arbitrary")),
        name="retention",
    )(rq, rk, rv, rg, state0, dm, qd, kd, gl)


def _query_t(q, tq):
    q = q.astype(F32)
    if tq < LANES:
        q = jnp.concatenate([q, jnp.zeros((LANES - tq, LANES), F32)], 0)
    return q.T


def _flash_t(streams, qi, tq, tk, sk, sk_valid, causal, m_ref, l_ref, acc_ref):
    for s in range(len(streams)):
        m_ref[s] = jnp.full(m_ref.shape[1:], NEG_INF, F32)
        l_ref[s] = jnp.zeros(l_ref.shape[1:], F32)
        acc_ref[s] = jnp.zeros(acc_ref.shape[1:], F32)

    def step(start, size, rel=None, valid=None):
        for s, (q_t, k_at, vt_at) in enumerate(streams):
            st = _dot(k_at(start, size), q_t)
            if rel is not None:
                kc = (lax.broadcasted_iota(jnp.int32, st.shape, 0) + rel) // CHUNK
                qc = lax.broadcasted_iota(jnp.int32, st.shape, 1) // CHUNK
                st = jnp.where(kc <= qc, st, NEG_INF)
            if valid is not None:
                st = jnp.where(lax.broadcasted_iota(jnp.int32, st.shape, 0) < valid, st, NEG_INF)
            m_old = m_ref[s]
            m_new = jnp.maximum(m_old, jnp.max(st, 0, keepdims=True))
            p = jnp.exp2(st - m_new)
            alpha = jnp.exp2(m_old - m_new)
            l_ref[s] = alpha * l_ref[s] + jnp.sum(p, 0, keepdims=True)
            acc_ref[s] = acc_ref[s] * alpha + _dot(vt_at(start, size), p.astype(BF16))
            m_ref[s] = m_new

    if causal:
        n_full = qi * (tq // tk)

        def body(j, c):
            step(pl.multiple_of(2 * j * tk, tk), tk)
            step(pl.multiple_of((2 * j + 1) * tk, tk), tk)
            return c

        lax.fori_loop(0, n_full // 2, body, 0)

        if (tq // tk) % 2 == 1:
            @pl.when(n_full % 2 == 1)
            def _():
                step(pl.multiple_of((n_full - 1) * tk, tk), tk)

        for d in range(tq // tk):
            step(pl.multiple_of(qi * tq + d * tk, tk), tk, rel=d * tk)
    else:
        for j in range(sk // tk):
            last_valid = sk_valid - j * tk
            step(j * tk, tk, valid=last_valid if last_valid < tk else None)
    return [acc_ref[s] / l_ref[s] for s in range(len(streams))]


def _attn_scratch(n_streams, dv, tq):
    tqp = max(tq, LANES)
    return [pltpu.VMEM((n_streams, 1, tqp), F32), pltpu.VMEM((n_streams, 1, tqp), F32),
            pltpu.VMEM((n_streams, dv, tqp), F32)]


def _mla_attn_kernel(q_ref, k_ref, vt_ref, o_ref, m_ref, l_ref, acc_ref, *, tq, tk, sk, sk_valid, causal):
    streams = []
    for hh in range(2):
        sl = slice(hh * LANES, (hh + 1) * LANES)
        vrows = slice(hh * MLA_V, (hh + 1) * MLA_V)
        k_at = lambda start, n, sl=sl: k_ref[pl.ds(start, n), sl]
        vt_at = lambda start, n, vrows=vrows: vt_ref[0, vrows, pl.ds(start, n)]
        streams.append((_query_t(q_ref[:, sl], tq).astype(BF16), k_at, vt_at))
    outs = _flash_t(streams, pl.program_id(2), tq, tk, sk, sk_valid, causal, m_ref, l_ref, acc_ref)
    o_ref[...] = jnp.concatenate(outs, 0).T[:tq].astype(BF16)


def _mla_attn(q, k, vt, B, sq, sk, sk_valid, tq, tk, causal):
    nq = sq // tq
    npair = MLA_HEADS // 2
    return pl.pallas_call(
        functools.partial(_mla_attn_kernel, tq=tq, tk=tk, sk=sk, sk_valid=sk_valid, causal=causal),
        grid=(B, npair, nq),
        in_specs=[
            pl.BlockSpec((tq, 2 * LANES), lambda b, p, i: (b * nq + i, p)),
            pl.BlockSpec((sk, 2 * LANES), lambda b, p, i: (b, p)),
            pl.BlockSpec((1, 2 * MLA_V, sk), lambda b, p, i: (b, p, 0)),
        ],
        out_specs=pl.BlockSpec((tq, LANES), lambda b, p, i: (b * nq + i, p)),
        out_shape=jax.ShapeDtypeStruct((B * sq, MLA_HEADS * MLA_V), BF16),
        scratch_shapes=_attn_scratch(2, MLA_V, tq),
        compiler_params=_cparams(("parallel", "parallel", "arbitrary")),
        name="mla_attn",
    )(q, k, vt)


def _diff_attn_kernel(lam_ref, gn_ref, q_ref, k_ref, vt_ref, o_ref, m_ref, l_ref, acc_ref,
                      *, tq, tk, sk, sk_valid, causal, lam_init):
    streams = []
    for g in range(DIFF_GROUP):
        sl = slice(g * LANES, (g + 1) * LANES)
        k_at = lambda start, n, sl=sl: k_ref[pl.ds(start, n), sl]
        vt_at = lambda start, n, sl=sl: vt_ref[0, sl, pl.ds(start, n)]
        q_t = _query_t(q_ref[:, sl], tq)
        feat = lax.broadcasted_iota(jnp.int32, q_t.shape, 0)
        zero = jnp.zeros_like(q_t)
        streams.append((jnp.where(feat < DIFF_HD, q_t, zero).astype(BF16), k_at, vt_at))
        streams.append((jnp.where(feat < DIFF_HD, zero, q_t).astype(BF16), k_at, vt_at))
    outs = _flash_t(streams, pl.program_id(2), tq, tk, sk, sk_valid, causal, m_ref, l_ref, acc_ref)
    lv = lam_ref[...]
    lam = (jnp.exp(jnp.sum(lv[0:1] * lv[1:2], -1, keepdims=True))
           - jnp.exp(jnp.sum(lv[2:3] * lv[3:4], -1, keepdims=True)) + lam_init)
    for g in range(DIFF_GROUP):
        o = (outs[2 * g] - lam * outs[2 * g + 1]).T[:tq]
        o_ref[:, g * LANES:(g + 1) * LANES] = (_rms_norm(o, gn_ref[...]) * (1.0 - lam_init)).astype(BF16)


def _diff_attn(lam_vecs, gn, q, k, vt, B, sq, sk, sk_valid, tq, tk, causal, lam_init):
    nq = sq // tq
    const = lambda b, h, i: (0, 0)
    return pl.pallas_call(
        functools.partial(_diff_attn_kernel, tq=tq, tk=tk, sk=sk, sk_valid=sk_valid, causal=causal,
                          lam_init=lam_init),
        grid=(B, DIFF_HEADS // DIFF_GROUP, nq),
        in_specs=[
            pl.BlockSpec(lam_vecs.shape, const), pl.BlockSpec(gn.shape, const),
            pl.BlockSpec((tq, DIFF_GROUP * LANES), lambda b, h, i: (b * nq + i, h)),
            pl.BlockSpec((sk, DIFF_GROUP * LANES), lambda b, h, i: (b, h)),
            pl.BlockSpec((1, DIFF_GROUP * DIFF_V, sk), lambda b, h, i: (b, h, 0)),
        ],
        out_specs=pl.BlockSpec((tq, DIFF_GROUP * LANES), lambda b, h, i: (b * nq + i, h)),
        out_shape=jax.ShapeDtypeStruct((B * sq, DIFF_HEADS * DIFF_V), BF16),
        scratch_shapes=_attn_scratch(2 * DIFF_GROUP, DIFF_V, tq),
        compiler_params=_cparams(("parallel", "parallel", "arbitrary")),
        name="diff_attn",
    )(lam_vecs, gn, q, k, vt)


def _out_proj_kernel(*refs, n_in):
    a_refs = refs[:n_in]
    (w_ref, x_ref, g_ref, b_ref, wr_ref, bias_ref, tri_ref,
     o_ref, gate_ref, route_ref, cnt_out_ref, gt_ref, oh_ref, cnt_ref) = refs[n_in:]
    y = None
    off = 0
    for a_ref in a_refs:
        width = a_ref.shape[1]
        part = _dot(a_ref[...], w_ref[off:off + width, :])
        y = part if y is None else y + part
        off += width
    x1 = _layer_norm(DN_ALPHA * x_ref[...] + y, g_ref[...], b_ref[...])
    o_ref[...] = x1
    _route(x1, wr_ref, bias_ref, tri_ref, gate_ref, route_ref, cnt_out_ref, gt_ref, oh_ref, cnt_ref)


def _out_proj(acts, w, x, g, b, wr_t, bias, tm):
    T, D = x.shape
    tri = jnp.asarray(np.triu(np.ones((tm, tm), np.float32), 1)).astype(BF16)
    row = lambda i: (i, 0)
    const = lambda i: (0, 0)
    return pl.pallas_call(
        functools.partial(_out_proj_kernel, n_in=len(acts)),
        grid=(T // tm,),
        in_specs=[pl.BlockSpec((tm, a.shape[1]), row) for a in acts] + [
            pl.BlockSpec(w.shape, const), pl.BlockSpec((tm, D), row),
            pl.BlockSpec(g.shape, const), pl.BlockSpec(b.shape, const),
            pl.BlockSpec(wr_t.shape, const), pl.BlockSpec(bias.shape, const), pl.BlockSpec(tri.shape, const),
        ],
        out_specs=[pl.BlockSpec((tm, D), row), pl.BlockSpec((tm, LANES), row),
                   pl.BlockSpec((8, tm), lambda i: (0, i)), pl.BlockSpec((SEG_ROWS, LANES), const)],
        out_shape=[jax.ShapeDtypeStruct((T, D), F32), jax.ShapeDtypeStruct((T, LANES), F32),
                   jax.ShapeDtypeStruct((8, T), jnp.int32), jax.ShapeDtypeStruct((SEG_ROWS, LANES), F32)],
        scratch_shapes=[pltpu.VMEM((LANES, tm), F32), pltpu.VMEM((SEG_ROWS, tm), F32),
                        pltpu.VMEM((SEG_ROWS, LANES), F32)],
        compiler_params=_cparams(("arbitrary",)),
        name="out_proj",
    )(*acts, w, x, g, b, wr_t, bias, tri)


def _odd_in_kernel(x_ref, w_ref, tab_ref, q_ref, kt_ref, kb_ref, vf_ref, vt_ref):
    xb = x_ref[...].astype(BF16)
    W = DIFF_HEADS * 2 * DIFF_HD
    hq = _dot(xb, w_ref[:, 0:W])
    hk = _dot(xb, w_ref[:, W:2 * W])
    for h in range(W // LANES):
        sl = slice(h * LANES, (h + 1) * LANES)
        q_ref[:, sl] = _rope(hq[:, sl], tab_ref[0], tab_ref[1], tab_ref[2], DIFF_HD // 2).astype(BF16)
        kk = _rope(hk[:, sl], tab_ref[3], tab_ref[4], tab_ref[5], DIFF_HD // 2)
        kt_ref[0, sl, :] = kk.T
        kb_ref[:, sl] = kk.astype(BF16)
    hv = _dot(xb, w_ref[:, 2 * W:])
    vf_ref[...] = hv
    vt_ref[0] = hv.T.astype(BF16)


def _odd_in(x, w, tabs, S, tm):
    T, D = x.shape
    W = DIFF_HEADS * 2 * DIFF_HD
    nt = tabs.shape[1] // tm
    row = lambda i: (i, 0)
    blk = pl.BlockSpec((tm, W), row)
    vt_spec, vt_shape = _vt_layout(T, S, tm, DIFF_HEADS * DIFF_V)
    kt_spec, kt_shape = _vt_layout(T, S, tm, W, F32)
    return pl.pallas_call(
        _odd_in_kernel,
        grid=(T // tm,),
        in_specs=[pl.BlockSpec((tm, D), row), pl.BlockSpec(w.shape, lambda i: (0, 0)),
                  pl.BlockSpec((tabs.shape[0], tm, LANES), lambda i: (0, i % nt, 0))],
        out_specs=[blk, kt_spec, blk, blk, vt_spec],
        out_shape=[jax.ShapeDtypeStruct((T, W), BF16), kt_shape,
                   jax.ShapeDtypeStruct((T, W), BF16), jax.ShapeDtypeStruct((T, W), F32), vt_shape],
        compiler_params=_cparams(("parallel",)),
        name="odd_in",
    )(x, w, tabs)


def _route(x, wr_ref, bias_ref, tri_ref, g_ref, route_ref, cnt_out_ref, gt_ref, oh_ref, cnt_ref):
    tm = x.shape[0]
    logits = _dot_nt(wr_ref[...], x.astype(BF16))
    sc = jax.nn.sigmoid(logits)
    sel = sc + bias_ref[...]
    r = [sel[e:e + 1, :] for e in range(N_EXPERTS)]
    s = [sc[e:e + 1, :] for e in range(N_EXPERTS)]
    grp = []
    for g in range(N_GROUPS):
        a, b, c, d = r[4 * g:4 * g + 4]
        top2 = jnp.maximum(jnp.maximum(jnp.maximum(a + b, a + c), jnp.maximum(a + d, b + c)),
                           jnp.maximum(b + d, c + d))
        grp.append(top2)
    best = jnp.maximum(jnp.maximum(grp[0], grp[1]), jnp.maximum(grp[2], grp[3]))
    taken = jnp.zeros((1, tm), jnp.bool_)
    chosen = []
    for g in range(N_GROUPS):
        win = jnp.logical_and(grp[g] == best, jnp.logical_not(taken))
        chosen.append(win)
        taken = jnp.logical_or(taken, win)
    picked = []
    for e in range(N_EXPERTS):
        g = e // EXPERTS_PER_GROUP
        rank = jnp.zeros((1, tm), F32)
        for k in range(4 * g, 4 * g + 4):
            if k < e:
                rank = rank + (r[k] >= r[e]).astype(F32)
            elif k > e:
                rank = rank + (r[k] > r[e]).astype(F32)
        picked.append(jnp.logical_and(chosen[g], rank < 2.0))
    w = [jnp.where(picked[e], s[e], 0.0) for e in range(N_EXPERTS)]
    denom = w[0]
    for e in range(1, N_EXPERTS):
        denom = denom + w[e]
    gt_ref[...] = jnp.zeros_like(gt_ref)
    for e in range(N_EXPERTS):
        gt_ref[e:e + 1, :] = w[e] / denom
    g_ref[...] = gt_ref[...].T

    @pl.when(pl.program_id(0) == 0)
    def _():
        cnt_ref[...] = jnp.zeros_like(cnt_ref)

    oh_ref[...] = jnp.zeros_like(oh_ref)
    for g in range(N_GROUPS):
        for p, (a, b) in enumerate(EXPERT_PAIRS):
            both = jnp.logical_and(picked[4 * g + a], picked[4 * g + b])
            oh_ref[g * N_PAIRS + p:g * N_PAIRS + p + 1, :] = both.astype(F32)
    oh = oh_ref[...]
    before = _dot(oh.astype(BF16), tri_ref[...])
    base = cnt_ref[:, 0:1]
    rank = jnp.sum(oh * (base + before), 0, keepdims=True)
    sid = jnp.sum(oh * lax.broadcasted_iota(jnp.int32, oh.shape, 0).astype(F32), 0, keepdims=True)
    row = lax.broadcasted_iota(jnp.int32, route_ref.shape, 0)
    route_ref[...] = jnp.where(row == 0, sid, jnp.where(row == 1, rank, 0.0)).astype(jnp.int32)
    cnt_ref[...] = cnt_ref[...] + jnp.sum(oh, 1, keepdims=True)
    cnt_out_ref[...] = cnt_ref[...]


def _moe_dense_kernel(x_ref, gate_ref, wg_ref, wu_ref, wd_ref, g_ref, b_ref, o_ref, xb_ref, acc_ref):
    e = pl.program_id(1)

    @pl.when(e == 0)
    def _():
        xb_ref[...] = x_ref[...].astype(BF16)
        acc_ref[...] = jnp.zeros_like(acc_ref)

    xb = xb_ref[...]
    h = jax.nn.silu(_dot(xb, wg_ref[0, 0].astype(BF16))) * _dot(xb, wu_ref[0, 0].astype(BF16))
    y = _dot(h.astype(BF16), wd_ref[0, 0].astype(BF16))
    gates = gate_ref[...]
    lane = lax.broadcasted_iota(jnp.int32, gates.shape, 1)
    gcol = jnp.sum(jnp.where(lane == e, gates, 0.0), -1, keepdims=True)
    acc_ref[...] += gcol * y

    @pl.when(e == pl.num_programs(1) - 1)
    def _():
        o_ref[...] = _layer_norm(DN_ALPHA * x_ref[...] + acc_ref[...], g_ref[...], b_ref[...])


def _moe_dense(x, gates, wg, wu, wd, l, g, b, tm):
    T, D = x.shape
    _, E, _, H = wg.shape
    row = lambda i, e: (i, 0)
    const = lambda i, e: (0, 0)
    wsel = lambda i, e: (l, e, 0, 0)
    return pl.pallas_call(
        _moe_dense_kernel,
        grid=(T // tm, E),
        in_specs=[
            pl.BlockSpec((tm, D), row), pl.BlockSpec((tm, LANES), row),
            pl.BlockSpec((1, 1, D, H), wsel), pl.BlockSpec((1, 1, D, H), wsel),
            pl.BlockSpec((1, 1, H, D), wsel),
            pl.BlockSpec(g.shape, const), pl.BlockSpec(b.shape, const),
        ],
        out_specs=pl.BlockSpec((tm, D), row),
        out_shape=jax.ShapeDtypeStruct((T, D), F32),
        scratch_shapes=[pltpu.VMEM((tm, D), BF16), pltpu.VMEM((tm, D), F32)],
        compiler_params=_cparams(("parallel", "arbitrary")),
        name="moe_dense",
    )(x, gates, wg, wu, wd, g, b)


SC_CORES = 2
SC_SUBCORES = 16
SC_WORKERS = SC_CORES * SC_SUBCORES
SC_TILE_BYTES = 384 * 1024
SPARSE_ROW_TILE = 512
SPARSE_MIN_TOKENS = 4096


def _sc_ring(per_w, row_bytes):
    for ch, nbuf in ((16, 4), (16, 2), (8, 2)):
        if per_w % (ch * nbuf) == 0 and ch * nbuf * row_bytes <= SC_TILE_BYTES:
            return ch, nbuf
    raise ValueError(f"no SparseCore gather ring for {per_w} rows of {row_bytes} bytes per subcore")


def _sc_gather(table, idx):
    R = idx.shape[0]
    D = table.shape[1]
    per_w = R // SC_WORKERS
    assert per_w * SC_WORKERS == R and per_w % 8 == 0
    ch, nbuf = _sc_ring(per_w, D * table.dtype.itemsize)
    nchunk = per_w // ch
    mesh = plsc.VectorSubcoreMesh(core_axis_name="c", subcore_axis_name="s")

    @functools.partial(
        pl.kernel, mesh=mesh,
        out_type=jax.ShapeDtypeStruct((R, D), table.dtype),
        scratch_types=[pltpu.VMEM((per_w,), jnp.int32), pltpu.VMEM((nbuf, ch, D), table.dtype),
                       pltpu.SemaphoreType.DMA((nbuf,)), pltpu.SemaphoreType.DMA((nbuf,))],
    )
    def gather_kernel(table_hbm, idx_hbm, out_hbm, idx_v, rows_v, gsem, wsem):
        base = (lax.axis_index("s") * SC_CORES + lax.axis_index("c")) * per_w
        pltpu.sync_copy(idx_hbm.at[pl.ds(base, per_w)], idx_v)

        def gather(c, b):
            return pltpu.make_async_copy(table_hbm.at[idx_v.at[pl.ds(c * ch, ch)]], rows_v.at[b], gsem.at[b])

        def write(c, b):
            return pltpu.make_async_copy(rows_v.at[b], out_hbm.at[pl.ds(base + c * ch, ch)], wsem.at[b])

        for b in range(nbuf - 1):
            gather(b, b).start()

        @pl.loop(0, nchunk, step=nbuf)
        def _(c):
            for b in range(nbuf):
                cc = c + b
                gather(cc, b).wait()
                write(cc, b).start()
                pb = (b - 1) % nbuf

                @pl.when(cc + nbuf - 1 < nchunk)
                def _():
                    @pl.when(cc >= 1)
                    def _():
                        write(cc - 1, pb).wait()

                    gather(cc + nbuf - 1, pb).start()

        for b in range(nbuf):
            write(nchunk - nbuf + b, b).wait()

    return gather_kernel(table, idx)


def _moe_group_kernel(te_ref, nv_ref, x_ref, gate_ref, *refs):
    w_refs, (g_ref, b_ref, o_ref) = refs[:3 * TOP_K], refs[3 * TOP_K:]
    j = pl.program_id(0)
    live = j < nv_ref[0]

    @pl.when(live)
    def _():
        x = x_ref[...]
        xb = x.astype(BF16)
        gates = gate_ref[...]
        lane = lax.broadcasted_iota(jnp.int32, gates.shape, 1)
        acc = None
        for k in range(TOP_K):
            wg_ref, wu_ref, wd_ref = w_refs[3 * k:3 * k + 3]
            h = jax.nn.silu(_dot(xb, wg_ref[0, 0].astype(BF16))) * _dot(xb, wu_ref[0, 0].astype(BF16))
            y = _dot(h.astype(BF16), wd_ref[0, 0].astype(BF16))
            gcol = jnp.sum(jnp.where(lane == te_ref[TOP_K * j + k], gates, 0.0), -1, keepdims=True)
            acc = gcol * y if acc is None else acc + gcol * y
        o_ref[...] = _layer_norm(DN_ALPHA * x + acc, g_ref[...], b_ref[...])

    @pl.when(jnp.logical_not(live))
    def _():
        o_ref[...] = jnp.zeros_like(o_ref)


def _moe_group(tile_experts, n_valid, xs, gs, wg, wu, wd, l, g, b, tm):
    R, D = xs.shape
    H = wg.shape[3]
    row = lambda j, te, nv: (j, 0)
    const = lambda j, te, nv: (0, 0)
    w_specs, w_args = [], []
    for k in range(TOP_K):
        wsel = lambda j, te, nv, k=k: (l, te[TOP_K * j + k], 0, 0)
        w_specs += [pl.BlockSpec((1, 1, D, H), wsel), pl.BlockSpec((1, 1, D, H), wsel),
                    pl.BlockSpec((1, 1, H, D), wsel)]
        w_args += [wg, wu, wd]
    grid_spec = pltpu.PrefetchScalarGridSpec(
        num_scalar_prefetch=2,
        grid=(R // tm,),
        in_specs=[pl.BlockSpec((tm, D), row), pl.BlockSpec((tm, LANES), row)] + w_specs + [
            pl.BlockSpec(g.shape, const), pl.BlockSpec(b.shape, const)],
        out_specs=pl.BlockSpec((tm, D), row),
    )
    return pl.pallas_call(
        _moe_group_kernel,
        grid_spec=grid_spec,
        out_shape=jax.ShapeDtypeStruct((R, D), F32),
        compiler_params=_cparams(("arbitrary",)),
        name="moe_group",
    )(tile_experts, n_valid, xs, gs, *w_args, g, b)


def _sc_scatter(x, dest, pad_dest, n_rows):
    T, D = x.shape
    P = pad_dest.shape[0]
    per_w, pad_w = T // SC_WORKERS, P // SC_WORKERS
    assert per_w * SC_WORKERS == T and pad_w * SC_WORKERS == P and T + P == n_rows
    ch, nbuf = _sc_ring(per_w, D * x.dtype.itemsize)
    assert pad_w % ch == 0
    nchunk, npad = per_w // ch, pad_w // ch
    mesh = plsc.VectorSubcoreMesh(core_axis_name="c", subcore_axis_name="s")

    @functools.partial(
        pl.kernel, mesh=mesh,
        out_type=jax.ShapeDtypeStruct((n_rows, D), x.dtype),
        scratch_types=[pltpu.VMEM((nchunk, ch), jnp.int32), pltpu.VMEM((npad, ch), jnp.int32),
                       pltpu.VMEM((nbuf, ch, D), x.dtype), pltpu.VMEM((ch, D), x.dtype),
                       pltpu.SemaphoreType.DMA((nbuf,)), pltpu.SemaphoreType.DMA((nbuf,)),
                       pltpu.SemaphoreType.DMA],
    )
    def scatter_kernel(x_hbm, dest_hbm, pad_hbm, zero_hbm, out_hbm, dest_v, pad_v, rows_v, zero_v,
                       rsem, wsem, zsem):
        wid = lax.axis_index("s") * SC_CORES + lax.axis_index("c")
        base = wid * per_w
        pltpu.sync_copy(dest_hbm.at[wid], dest_v)
        pltpu.sync_copy(pad_hbm.at[wid], pad_v)
        pltpu.sync_copy(zero_hbm, zero_v)

        def read(c, b):
            return pltpu.make_async_copy(x_hbm.at[pl.ds(base + c * ch, ch)], rows_v.at[b], rsem.at[b])

        def write(c, b):
            return pltpu.make_async_copy(rows_v.at[b], out_hbm.at[dest_v.at[c]], wsem.at[b])

        def write_zero(c):
            return pltpu.make_async_copy(zero_v, out_hbm.at[pad_v.at[c]], zsem)

        for c in range(npad):
            write_zero(c).start()
        for b in range(nbuf - 1):
            read(b, b).start()

        @pl.loop(0, nchunk, step=nbuf)
        def _(c):
            for b in range(nbuf):
                cc = c + b
                read(cc, b).wait()
                write(cc, b).start()
                pb = (b - 1) % nbuf

                @pl.when(cc + nbuf - 1 < nchunk)
                def _():
                    @pl.when(cc >= 1)
                    def _():
                        write(cc - 1, pb).wait()

                    read(cc + nbuf - 1, pb).start()

        for b in range(nbuf):
            write(nchunk - nbuf + b, b).wait()
        for c in range(npad):
            write_zero(c).wait()

    return scatter_kernel(x, dest.reshape(SC_WORKERS, nchunk, ch), pad_dest.reshape(SC_WORKERS, npad, ch),
                          jnp.zeros((ch, D), x.dtype))


def _after(value, token):
    if token is None:
        return value
    return value + (0.0 * token[0]).astype(value.dtype)


def _moe_sparse(x, gates, route, counts, wg, wu, wd, l, g, b, sync):
    T = x.shape[0]
    tm = SPARSE_ROW_TILE
    n_tiles = T // tm + N_SEG
    n_rows = n_tiles * tm
    sid, rank = route[0], route[1]
    cnt = counts[:N_SEG, 0].astype(jnp.int32)
    tiles = (cnt + tm - 1) // tm
    ends = jnp.cumsum(tiles)
    starts = (ends - tiles) * tm
    seg_ids = jnp.arange(N_SEG, dtype=jnp.int32)
    pick = lambda which, vals: jnp.sum(jnp.where(which[:, None] == seg_ids[None, :], vals[None, :], 0), 1)
    passed = lambda pos, bounds: jnp.sum((pos[:, None] >= bounds[None, :]).astype(jnp.int32), 1)
    dest = rank + pick(sid, starts)
    pad_cnt = tiles * tm - cnt
    pad_end = jnp.cumsum(pad_cnt)
    p = jnp.arange(n_rows - T, dtype=jnp.int32)
    seg = passed(p, pad_end)
    pad_dest = jnp.where(seg < N_SEG,
                         pick(seg, starts + cnt) + p - pick(seg, pad_end - pad_cnt),
                         ends[N_SEG - 1] * tm + p - pad_end[N_SEG - 1])
    tile_seg = jnp.minimum(passed(jnp.arange(n_tiles, dtype=jnp.int32), ends), N_SEG - 1)
    pair = jnp.asarray(np.asarray(EXPERT_PAIRS, np.int32))
    tile_experts = (tile_seg // N_PAIRS * EXPERTS_PER_GROUP)[:, None] + pair[tile_seg % N_PAIRS]
    xs = _sc_scatter(x, dest, pad_dest, n_rows)
    gs = _sc_scatter(gates, dest, pad_dest, n_rows)
    sync[("scatter", l)] = dest[:1].astype(F32)
    yield
    n_valid = _after(ends[N_SEG - 1:], sync.get(("dense", l)))
    ys = _moe_group(tile_experts.reshape(-1), n_valid, xs, gs, wg, wu, wd, l, g, b, tm)
    return _sc_gather(ys, dest)


def _prep_weights(w_in_even, w_uq, w_ukv, w_out_even, w_in_odd, w_out_odd, w_router,
                  w_expert_gate, w_expert_up, w_expert_down):
    d = w_in_even.shape[1]
    n_main = 4 * RET_W + MLA_Q_RANK + MLA_KV_RANK
    w_in = w_in_even[0]
    kr_cols = jnp.pad(w_in[:, n_main:], ((0, 0), (MLA_NOPE, LANES - MLA_NOPE - MLA_ROPE)))
    w_a = jnp.concatenate([w_in[:, :n_main], kr_cols], 1).astype(BF16)
    qd = MLA_NOPE + MLA_ROPE
    wq = jnp.pad(w_uq[0].reshape(MLA_Q_RANK, MLA_HEADS, qd), ((0, 0), (0, 0), (0, LANES - qd)))
    wq = wq.reshape(MLA_Q_RANK, MLA_PAD).astype(BF16)
    wkv = w_ukv[0].reshape(MLA_KV_RANK, MLA_HEADS, MLA_NOPE + MLA_V)
    wk = jnp.pad(wkv[:, :, :MLA_NOPE], ((0, 0), (0, 0), (0, LANES - MLA_NOPE)))
    wk = wk.reshape(MLA_KV_RANK, MLA_PAD).astype(BF16)
    wvt = wkv[:, :, MLA_NOPE:].reshape(MLA_KV_RANK, MLA_HEADS * MLA_V).T.astype(BF16)
    e_np = np.zeros((MLA_ROPE, MLA_HEADS, LANES), np.float32)
    for j in range(MLA_ROPE):
        e_np[j, :, MLA_NOPE + j] = 1.0
    e_mat = jnp.asarray(e_np.reshape(MLA_ROPE, MLA_PAD)).astype(BF16)
    return dict(
        w_a=w_a, wq=wq, wk=wk, wvt=wvt, e_mat=e_mat,
        w_out_even=w_out_even[0].astype(BF16), w_in_odd=w_in_odd[0].astype(BF16),
        w_out_odd=w_out_odd[0].astype(BF16), wr_t=w_router.T.astype(BF16),
        wg=w_expert_gate, wu=w_expert_up, wd=w_expert_down,
    )


def _moe(routed, wts, l, ln_g, ln_b, tm, sync):
    x, gates, route, counts = routed
    g, b = ln_g[l, 1][None], ln_b[l, 1][None]
    if x.shape[0] >= SPARSE_MIN_TOKENS:
        return (yield from _moe_sparse(x, gates, route, counts, wts["wg"], wts["wu"], wts["wd"], l, g, b, sync))
    out = _moe_dense(x, gates, wts["wg"], wts["wu"], wts["wd"], l, g, b, tm)
    sync[("dense", l)] = out[0, :1]
    yield
    return out


def _interleave(*gens):
    results = [None] * len(gens)
    live = list(range(len(gens)))
    while live:
        for i in list(live):
            try:
                next(gens[i])
            except StopIteration as done:
                results[i] = done.value
                live.remove(i)
    return results


def _trunk(x3, pos0, past, wts, prm, sync):
    B, S, D = x3.shape
    T = B * S
    x = _after(x3.reshape(T, D), sync.get(("scatter", 0)))
    tm = min(T, 1024)
    rep = max(tm // S, 1)
    pos = np.tile(pos0 + np.arange(S), rep)
    ln_g, ln_b = prm["ln_g"], prm["ln_b"]

    rq, rk, rv, rg, q, lat, kr = _even_in(x, wts["w_a"], wts["wq"], prm["gq"], prm["gkv"], _even_tables(pos), tm)
    causal = past is None
    if causal:
        state0 = jnp.zeros((B, RET_HEADS, RET_DK, RET_DV), F32)
        lat_all, kr_all, sk, skp = lat, kr, S, S
        tq, tk = min(S, 1024), min(S, 1024)
    else:
        state0 = past["state"]
        sk = past["lat"].shape[1] + S
        skp = -(-sk // LANES) * LANES
        padk = lambda parts: jnp.concatenate(
            parts + [jnp.zeros((B, skp - sk, parts[0].shape[2]), parts[0].dtype)], 1).reshape(B * skp, -1)
        lat_all = padk([past["lat"], lat.reshape(B, S, -1)])
        kr_all = padk([past["kr"], kr.reshape(B, S, -1)])
        tq, tk = S, skp
    ret_out, ret_state = _retention(rq, rk, rv, rg, state0, B, S)
    tkv = 512 if (B * skp) % 512 == 0 else skp
    k_mla, vt_mla = _kv_up(lat_all, kr_all, wts["wk"], wts["wvt"], wts["e_mat"], skp, tkv)
    if vt_mla.shape[0] != B:
        vt_mla = vt_mla.reshape(-1, B, skp).transpose(1, 0, 2)
    mla_out = _mla_attn(q, k_mla, vt_mla, B, S, skp, sk, tq, tk, causal)
    routed = _out_proj([ret_out, mla_out], wts["w_out_even"], x, ln_g[0, 0][None], ln_b[0, 0][None],
                       wts["wr_t"], prm["bias"], tm)
    x = yield from _moe(routed, wts, 0, ln_g, ln_b, tm, sync)
    x = _after(x, sync.get(("scatter", 1)))

    qd, kt, kb, vf, vt = _odd_in(x, wts["w_in_odd"], _odd_tables(pos), S, tm)
    if vt.shape[0] != B:
        vt = vt.reshape(-1, B, S).transpose(1, 0, 2)
        kt = kt.reshape(-1, B, S).transpose(1, 0, 2)
    kf = kt.reshape(1, B, 2 * DIFF_HEADS, DIFF_HD, S).transpose(0, 1, 4, 2, 3)
    if causal:
        k_all, vt_all = kb, vt
    else:
        k_all = padk([past["dk"], kb.reshape(B, S, -1)])
        vt_all = jnp.concatenate([past["dv"].transpose(0, 2, 1), vt,
                                  jnp.zeros((B, vt.shape[1], skp - sk), BF16)], 2)
    lam_init = 0.8 - 0.6 * math.exp(-0.3 * 1)
    d_out = _diff_attn(prm["lam"], prm["gn"], qd, k_all, vt_all, B, S, skp, sk, tq, tk, causal, lam_init)
    routed = _out_proj([d_out], wts["w_out_odd"], x, ln_g[1, 0][None], ln_b[1, 0][None],
                       wts["wr_t"], prm["bias"], tm)
    x = yield from _moe(routed, wts, 1, ln_g, ln_b, tm, sync)

    return (x.reshape(B, S, D), ret_state[None], lat.reshape(1, B, S, -1), kr.reshape(1, B, S, -1),
            kf, vf.reshape(1, B, S, DIFF_HEADS, DIFF_V))


def kernel(x_prompt, x_sample, state_ret, cache_mla_latent, cache_mla_krope, cache_diff_k, cache_diff_v,
           w_in_even, w_uq, w_ukv, g_qnorm, g_kvnorm, w_out_even,
           w_in_odd, lambda_q1, lambda_k1, lambda_q2, lambda_k2, g_diff_norm, w_out_odd,
           ln_g, ln_b, w_router, router_bias, w_expert_gate, w_expert_up, w_expert_down):
    wts = _prep_weights(w_in_even, w_uq, w_ukv, w_out_even, w_in_odd, w_out_odd, w_router,
                        w_expert_gate, w_expert_up, w_expert_down)
    prm = dict(
        gq=g_qnorm[0][None].astype(F32), gkv=g_kvnorm[0][None].astype(F32),
        lam=jnp.stack([lambda_q1[0], lambda_k1[0], lambda_q2[0], lambda_k2[0]]).astype(F32),
        gn=g_diff_norm[0][None].astype(F32), bias=router_bias.reshape(N_EXPERTS, 1).astype(F32),
        ln_g=ln_g.astype(F32), ln_b=ln_b.astype(F32),
    )
    past_len = cache_mla_latent.shape[2]
    db = x_sample.shape[0]
    past = dict(
        state=state_ret[0].astype(F32), lat=cache_mla_latent[0], kr=cache_mla_krope[0],
        dk=cache_diff_k[0].reshape(db, past_len, -1).astype(BF16),
        dv=cache_diff_v[0].reshape(db, past_len, -1).astype(BF16),
    )
    sync = {}
    outs_p, outs_s = _interleave(_trunk(x_prompt, 0, None, wts, prm, sync),
                                 _trunk(x_sample, past_len, past, wts, prm, sync))
    return (outs_p[0], outs_s[0]) + outs_p[1:] + outs_s[1:]
```

```python
import functools
import math

import numpy as np
import jax
import jax.numpy as jnp
from jax import lax
from jax.experimental import pallas as pl
from jax.experimental.pallas import tpu as pltpu
from jax.experimental.pallas import tpu_sc as plsc

F32 = jnp.float32
BF16 = jnp.bfloat16

CHUNK = 64
ROPE_THETA = 10000.0
NEG_INF = -1e30
LN_EPS = 1e-5
NORM_EPS = 1e-6
DEPTH = 2
DN_ALPHA = (2.0 * DEPTH) ** 0.25
RET_HEADS = 4
RET_DK = 128
RET_DV = 128
RET_LOG_GAMMA = tuple(math.log(1.0 - 2.0 ** (-5 - h)) for h in range(RET_HEADS))
MLA_HEADS = 8
MLA_Q_RANK = 384
MLA_KV_RANK = 256
MLA_NOPE = 64
MLA_ROPE = 32
MLA_V = 64
DIFF_HEADS = 8
DIFF_HD = 64
DIFF_V = 128
N_EXPERTS = 16
N_GROUPS = 4
EXPERTS_PER_GROUP = 4
TOP_K = 2
EXPERT_PAIRS = tuple((a, b) for a in range(EXPERTS_PER_GROUP) for b in range(a + 1, EXPERTS_PER_GROUP))
N_PAIRS = len(EXPERT_PAIRS)
N_SEG = N_GROUPS * N_PAIRS
SEG_ROWS = 32
LOG2E = math.log2(math.e)

LANES = 128
RET_W = RET_HEADS * RET_DK
MLA_PAD = MLA_HEADS * LANES
VMEM_LIMIT = 56 * 1024 * 1024
TOKEN_TILE = 1024
KV_UP_TILE = 512
RET_CHUNK = 256
RET_TILE = 512
ATTN_TILE = 1024


def _cparams(sem):
    return pltpu.CompilerParams(dimension_semantics=sem, vmem_limit_bytes=VMEM_LIMIT)


def _rope_tables(pos, d, group, offset, scale):
    pos = np.asarray(pos, np.float64)
    half = d // 2
    inv = 1.0 / (ROPE_THETA ** (np.arange(0, d, 2, dtype=np.float64) / d))
    ang = pos[:, None] * inv[None, :]
    cos = np.full((pos.shape[0], LANES), scale, np.float64)
    s_lo = np.zeros((pos.shape[0], LANES), np.float64)
    s_hi = np.zeros((pos.shape[0], LANES), np.float64)
    start = offset
    while start + d <= LANES:
        cos[:, start:start + half] = np.cos(ang) * scale
        cos[:, start + half:start + d] = np.cos(ang) * scale
        s_lo[:, start:start + half] = -np.sin(ang) * scale
        s_hi[:, start + half:start + d] = np.sin(ang) * scale
        start += group
    return cos, s_lo, s_hi


def _even_tables(pos):
    rq = _rope_tables(pos, RET_DK, LANES, 0, 1.0)
    rk = _rope_tables(pos, RET_DK, LANES, 0, RET_DK ** -0.5)
    c = (MLA_NOPE + MLA_ROPE) ** -0.5 * LOG2E
    mq = _rope_tables(pos, MLA_ROPE, LANES, MLA_NOPE, c)
    mk = _rope_tables(pos, MLA_ROPE, LANES, MLA_NOPE, 1.0)
    tabs = [rq[0], rq[1] + rq[2], rk[0], rk[1] + rk[2], mq[0], mq[1], mq[2], mk[0], mk[1], mk[2]]
    return jnp.asarray(np.stack(tabs).astype(np.float32))


def _odd_tables(pos):
    c = DIFF_HD ** -0.5 * LOG2E
    dq = _rope_tables(pos, DIFF_HD, DIFF_HD, 0, c)
    dk = _rope_tables(pos, DIFF_HD, DIFF_HD, 0, 1.0)
    return jnp.asarray(np.stack(list(dq) + list(dk)).astype(np.float32))


def _retention_tables(L):
    lg = np.asarray(RET_LOG_GAMMA, np.float64)
    idx = np.arange(L, dtype=np.float64)
    diff = idx[:, None] - idx[None, :]
    dmask = np.where(diff[None] >= 0, np.exp(np.maximum(diff, 0.0)[None] * lg[:, None, None]), 0.0)
    qd = np.exp((idx[None, :] + 1.0) * lg[:, None])
    kd = np.exp((L - 1.0 - idx)[None, :] * lg[:, None])
    gl = np.exp(L * lg)
    qd = np.broadcast_to(qd[:, :, None], (RET_HEADS, L, LANES))
    kd = np.broadcast_to(kd[:, :, None], (RET_HEADS, L, LANES))
    gl = np.broadcast_to(gl[:, None, None], (RET_HEADS, RET_DK, RET_DV))
    f = lambda a: jnp.asarray(np.ascontiguousarray(a).astype(np.float32))
    return f(dmask), f(qd), f(kd), f(gl)


def _dot(a, b):
    return jnp.dot(a, b, preferred_element_type=F32)


def _dot_nt(a, b):
    return lax.dot_general(a, b, (((1,), (1,)), ((), ())), preferred_element_type=F32)


def _dot_tn(a, b):
    return lax.dot_general(a, b, (((0,), (0,)), ((), ())), preferred_element_type=F32)


def _rope(x, cos, s_lo, s_hi, half):
    return x * cos + pltpu.roll(x, LANES - half, 1) * s_lo + pltpu.roll(x, half, 1) * s_hi


def _layer_norm(x, g, b):
    mu = jnp.mean(x, -1, keepdims=True)
    xc = x - mu
    var = jnp.mean(xc * xc, -1, keepdims=True)
    return xc * lax.rsqrt(var + LN_EPS) * g + b


def _rms_norm(x, g):
    return x * lax.rsqrt(jnp.mean(x * x, -1, keepdims=True) + NORM_EPS) * g


def _even_in_kernel(x_ref, w_ref, wq_ref, gq_ref, gkv_ref, tab_ref,
                    rq_ref, rk_ref, rv_ref, rg_ref, q_ref, lat_ref, kr_ref):
    xb = x_ref[...].astype(BF16)
    c_rq, s_rq, c_rk, s_rk = tab_ref[0], tab_ref[1], tab_ref[2], tab_ref[3]
    hq = _dot(xb, w_ref[:, 0:RET_W])
    hk = _dot(xb, w_ref[:, RET_W:2 * RET_W])
    for h in range(RET_HEADS):
        sl = slice(h * LANES, (h + 1) * LANES)
        xq = hq[:, sl]
        rq_ref[:, sl] = (xq * c_rq + pltpu.roll(xq, RET_DK // 2, 1) * s_rq).astype(BF16)
        xk = hk[:, sl]
        rk_ref[:, sl] = (xk * c_rk + pltpu.roll(xk, RET_DK // 2, 1) * s_rk).astype(BF16)
    rv_ref[...] = _dot(xb, w_ref[:, 2 * RET_W:3 * RET_W]).astype(BF16)
    rg_ref[...] = _dot(xb, w_ref[:, 3 * RET_W:4 * RET_W]).astype(BF16)
    o = 4 * RET_W
    cq = _dot(xb, w_ref[:, o:o + MLA_Q_RANK])
    qn = _rms_norm(cq, gq_ref[...]).astype(BF16)
    qf = _dot(qn, wq_ref[...])
    c_q, lo_q, hi_q = tab_ref[4], tab_ref[5], tab_ref[6]
    for h in range(MLA_HEADS):
        sl = slice(h * LANES, (h + 1) * LANES)
        q_ref[:, sl] = _rope(qf[:, sl], c_q, lo_q, hi_q, MLA_ROPE // 2).astype(BF16)
    o += MLA_Q_RANK
    ckv = _dot(xb, w_ref[:, o:o + MLA_KV_RANK])
    lat_ref[...] = _rms_norm(ckv, gkv_ref[...])
    o += MLA_KV_RANK
    krp = _dot(xb, w_ref[:, o:o + LANES])
    krp = _rope(krp, tab_ref[7], tab_ref[8], tab_ref[9], MLA_ROPE // 2)
    kr_ref[...] = krp[:, MLA_NOPE:MLA_NOPE + MLA_ROPE]


def _even_in(x, w_a, wq, gq, gkv, tabs, tm):
    T, D = x.shape
    P = tabs.shape[1]
    nt = P // tm
    row = lambda i: (i, 0)
    const = lambda i: (0, 0)
    outs = [
        jax.ShapeDtypeStruct((T, RET_W), BF16), jax.ShapeDtypeStruct((T, RET_W), BF16),
        jax.ShapeDtypeStruct((T, RET_W), BF16), jax.ShapeDtypeStruct((T, RET_W), BF16),
        jax.ShapeDtypeStruct((T, MLA_PAD), BF16),
        jax.ShapeDtypeStruct((T, MLA_KV_RANK), F32), jax.ShapeDtypeStruct((T, MLA_ROPE), F32),
    ]
    return pl.pallas_call(
        _even_in_kernel,
        grid=(T // tm,),
        in_specs=[
            pl.BlockSpec((tm, D), row),
            pl.BlockSpec(w_a.shape, const),
            pl.BlockSpec(wq.shape, const),
            pl.BlockSpec(gq.shape, const),
            pl.BlockSpec(gkv.shape, const),
            pl.BlockSpec((tabs.shape[0], tm, LANES), lambda i: (0, i % nt, 0)),
        ],
        out_specs=[
            pl.BlockSpec((tm, RET_W), row), pl.BlockSpec((tm, RET_W), row),
            pl.BlockSpec((tm, RET_W), row), pl.BlockSpec((tm, RET_W), row),
            pl.BlockSpec((tm, MLA_PAD), row),
            pl.BlockSpec((tm, MLA_KV_RANK), row), pl.BlockSpec((tm, MLA_ROPE), row),
        ],
        out_shape=outs,
        compiler_params=_cparams(("parallel",)),
        name="even_in",
    )(x, w_a, wq, gq, gkv, tabs)


def _kv_up_kernel(lat_ref, kr_ref, wk_ref, wvt_ref, e_ref, k_ref, vt_ref):
    lb = lat_ref[...].astype(BF16)
    krb = kr_ref[...].astype(BF16)
    k_ref[...] = (_dot(lb, wk_ref[...]) + _dot(krb, e_ref[...])).astype(BF16)
    vt_ref[0] = _dot_nt(wvt_ref[...], lb).astype(BF16)


def _vt_layout(T, S, tm, width, dtype=BF16):
    nb, cols = (T // S, S) if S % tm == 0 else (1, T)
    nt = cols // tm
    spec = pl.BlockSpec((1, width, tm), lambda i: (i // nt, 0, i % nt))
    return spec, jax.ShapeDtypeStruct((nb, width, cols), dtype)


def _kv_up(lat, kr, wk, wvt, e_mat, S, tm):
    T = lat.shape[0]
    row = lambda i: (i, 0)
    const = lambda i: (0, 0)
    vt_spec, vt_shape = _vt_layout(T, S, tm, MLA_HEADS * MLA_V)
    return pl.pallas_call(
        _kv_up_kernel,
        grid=(T // tm,),
        in_specs=[
            pl.BlockSpec((tm, MLA_KV_RANK), row), pl.BlockSpec((tm, MLA_ROPE), row),
            pl.BlockSpec(wk.shape, const), pl.BlockSpec(wvt.shape, const), pl.BlockSpec(e_mat.shape, const),
        ],
        out_specs=[pl.BlockSpec((tm, MLA_PAD), row), vt_spec],
        out_shape=[jax.ShapeDtypeStruct((T, MLA_PAD), BF16), vt_shape],
        compiler_params=_cparams(("parallel",)),
        name="kv_up",
    )(lat, kr, wk, wvt, e_mat)


def _retention_kernel(q_ref, k_ref, v_ref, g_ref, s0_ref, dm_ref, qd_ref, kd_ref, gl_ref,
                      o_ref, st_ref, *, L, nchunk):
    @pl.when(pl.program_id(1) == 0)
    def _():
        st_ref[...] = s0_ref[...]

    for c in range(nchunk):
        rows = slice(c * L, (c + 1) * L)
        for h in range(RET_HEADS):
            sl = slice(h * LANES, (h + 1) * LANES)
            q = q_ref[rows, sl]
            k = k_ref[rows, sl]
            v = v_ref[rows, sl]
            st = st_ref[0, h]
            a = (_dot_nt(q, k) * dm_ref[h]).astype(BF16)
            o = _dot(a, v) + _dot(q, st.astype(BF16)) * qd_ref[h]
            kdec = (k.astype(F32) * kd_ref[h]).astype(BF16)
            st_ref[0, h] = st * gl_ref[h] + _dot_tn(kdec, v)
            mu = jnp.mean(o, -1, keepdims=True)
            oc = o - mu
            var = jnp.mean(oc * oc, -1, keepdims=True)
            on = oc * lax.rsqrt(var + LN_EPS)
            g = g_ref[rows, sl].astype(F32)
            o_ref[rows, sl] = (g * jax.nn.sigmoid(g) * on).astype(BF16)


def _retention(rq, rk, rv, rg, state0, B, S):
    L = min(S, RET_CHUNK)
    lt = min(S, RET_TILE)
    nj = S // lt
    dm, qd, kd, gl = _retention_tables(L)
    row = lambda b, j: (b * nj + j, 0)
    c3 = lambda b, j: (0, 0, 0)
    st_spec = pl.BlockSpec((1, RET_HEADS, RET_DK, RET_DV), lambda b, j: (b, 0, 0, 0))
    return pl.pallas_call(
        functools.partial(_retention_kernel, L=L, nchunk=lt // L),
        grid=(B, nj),
        in_specs=[pl.BlockSpec((lt, RET_W), row)] * 4 + [
            st_spec,
            pl.BlockSpec(dm.shape, c3), pl.BlockSpec(qd.shape, c3),
            pl.BlockSpec(kd.shape, c3), pl.BlockSpec(gl.shape, c3),
        ],
        out_specs=[pl.BlockSpec((lt, RET_W), row), st_spec],
        out_shape=[jax.ShapeDtypeStruct((B * S, RET_W), BF16),
                   jax.ShapeDtypeStruct((B, RET_HEADS, RET_DK, RET_DV), F32)],
        compiler_params=_cparams(("parallel", "arbitrary")),
        name="retention",
    )(rq, rk, rv, rg, state0, dm, qd, kd, gl)


def _query_t(q, tq):
    q = q.astype(F32)
    if tq < LANES:
        q = jnp.concatenate([q, jnp.zeros((LANES - tq, LANES), F32)], 0)
    return q.T


def _flash_t(streams, qi, tq, tk, sk, sk_valid, causal, m_ref, l_ref, acc_ref):
    for s in range(len(streams)):
        m_ref[s] = jnp.full(m_ref.shape[1:], NEG_INF, F32)
        l_ref[s] = jnp.zeros(l_ref.shape[1:], F32)
        acc_ref[s] = jnp.zeros(acc_ref.shape[1:], F32)

    def step(start, size, rel=None, valid=None):
        for s, (q_t, k_at, vt_at) in enumerate(streams):
            st = _dot(k_at(start, size), q_t)
            if rel is not None:
                kc = (lax.broadcasted_iota(jnp.int32, st.shape, 0) + rel) // CHUNK
                qc = lax.broadcasted_iota(jnp.int32, st.shape, 1) // CHUNK
                st = jnp.where(kc <= qc, st, NEG_INF)
            if valid is not None:
                st = jnp.where(lax.broadcasted_iota(jnp.int32, st.shape, 0) < valid, st, NEG_INF)
            m_old = m_ref[s]
            m_new = jnp.maximum(m_old, jnp.max(st, 0, keepdims=True))
            p = jnp.exp2(st - m_new)
            alpha = jnp.exp2(m_old - m_new)
            l_ref[s] = alpha * l_ref[s] + jnp.sum(p, 0, keepdims=True)
            acc_ref[s] = acc_ref[s] * alpha + _dot(vt_at(start, size), p.astype(BF16))
            m_ref[s] = m_new

    if causal:
        n_full = qi * (tq // tk)

        def body(j, c):
            step(pl.multiple_of(2 * j * tk, tk), tk)
            step(pl.multiple_of((2 * j + 1) * tk, tk), tk)
            return c

        lax.fori_loop(0, n_full // 2, body, 0)

        if (tq // tk) % 2 == 1:
            @pl.when(n_full % 2 == 1)
            def _():
                step(pl.multiple_of((n_full - 1) * tk, tk), tk)

        for d in range(tq // tk):
            step(pl.multiple_of(qi * tq + d * tk, tk), tk, rel=d * tk)
    else:
        for j in range(sk // tk):
            last_valid = sk_valid - j * tk
            step(j * tk, tk, valid=last_valid if last_valid < tk else None)
    return [acc_ref[s] / l_ref[s] for s in range(len(streams))]


def _attn_scratch(n_streams, dv, tq):
    tqp = max(tq, LANES)
    return [pltpu.VMEM((n_streams, 1, tqp), F32), pltpu.VMEM((n_streams, 1, tqp), F32),
            pltpu.VMEM((n_streams, dv, tqp), F32)]


def _mla_attn_kernel(q_ref, k_ref, vt_ref, o_ref, m_ref, l_ref, acc_ref, *, tq, tk, sk, sk_valid, causal):
    streams = []
    for hh in range(2):
        sl = slice(hh * LANES, (hh + 1) * LANES)
        vrows = slice(hh * MLA_V, (hh + 1) * MLA_V)
        k_at = lambda start, n, sl=sl: k_ref[pl.ds(start, n), sl]
        vt_at = lambda start, n, vrows=vrows: vt_ref[0, vrows, pl.ds(start, n)]
        streams.append((_query_t(q_ref[:, sl], tq).astype(BF16), k_at, vt_at))
    outs = _flash_t(streams, pl.program_id(2), tq, tk, sk, sk_valid, causal, m_ref, l_ref, acc_ref)
    o_ref[...] = jnp.concatenate(outs, 0).T[:tq].astype(BF16)


def _mla_attn(q, k, vt, B, sq, sk, sk_valid, tq, tk, causal):
    nq = sq // tq
    npair = MLA_HEADS // 2
    return pl.pallas_call(
        functools.partial(_mla_attn_kernel, tq=tq, tk=tk, sk=sk, sk_valid=sk_valid, causal=causal),
        grid=(B, npair, nq),
        in_specs=[
            pl.BlockSpec((tq, 2 * LANES), lambda b, p, i: (b * nq + i, p)),
            pl.BlockSpec((sk, 2 * LANES), lambda b, p, i: (b, p)),
            pl.BlockSpec((1, 2 * MLA_V, sk), lambda b, p, i: (b, p, 0)),
        ],
        out_specs=pl.BlockSpec((tq, LANES), lambda b, p, i: (b * nq + i, p)),
        out_shape=jax.ShapeDtypeStruct((B * sq, MLA_HEADS * MLA_V), BF16),
        scratch_shapes=_attn_scratch(2, MLA_V, tq),
        compiler_params=_cparams(("parallel", "parallel", "arbitrary")),
        name="mla_attn",
    )(q, k, vt)


def _diff_attn_kernel(lam_ref, gn_ref, q_ref, k_ref, vt_ref, o_ref, m_ref, l_ref, acc_ref,
                      *, tq, tk, sk, sk_valid, causal, lam_init):
    k_at = lambda start, n: k_ref[pl.ds(start, n), :]
    vt_at = lambda start, n: vt_ref[0, :, pl.ds(start, n)]
    q_t = _query_t(q_ref[...], tq)
    feat = lax.broadcasted_iota(jnp.int32, q_t.shape, 0)
    zero = jnp.zeros_like(q_t)
    q1 = jnp.where(feat < DIFF_HD, q_t, zero).astype(BF16)
    q2 = jnp.where(feat < DIFF_HD, zero, q_t).astype(BF16)
    o1, o2 = _flash_t([(q1, k_at, vt_at), (q2, k_at, vt_at)], pl.program_id(2), tq, tk, sk, sk_valid,
                      causal, m_ref, l_ref, acc_ref)
    lv = lam_ref[...]
    lam = (jnp.exp(jnp.sum(lv[0:1] * lv[1:2], -1, keepdims=True))
           - jnp.exp(jnp.sum(lv[2:3] * lv[3:4], -1, keepdims=True)) + lam_init)
    o = (o1 - lam * o2).T[:tq]
    o_ref[...] = (_rms_norm(o, gn_ref[...]) * (1.0 - lam_init)).astype(BF16)


def _diff_attn(lam_vecs, gn, q, k, vt, B, sq, sk, sk_valid, tq, tk, causal, lam_init):
    nq = sq // tq
    const = lambda b, h, i: (0, 0)
    return pl.pallas_call(
        functools.partial(_diff_attn_kernel, tq=tq, tk=tk, sk=sk, sk_valid=sk_valid, causal=causal,
                          lam_init=lam_init),
        grid=(B, DIFF_HEADS, nq),
        in_specs=[
            pl.BlockSpec(lam_vecs.shape, const), pl.BlockSpec(gn.shape, const),
            pl.BlockSpec((tq, LANES), lambda b, h, i: (b * nq + i, h)),
            pl.BlockSpec((sk, LANES), lambda b, h, i: (b, h)),
            pl.BlockSpec((1, DIFF_V, sk), lambda b, h, i: (b, h, 0)),
        ],
        out_specs=pl.BlockSpec((tq, LANES), lambda b, h, i: (b * nq + i, h)),
        out_shape=jax.ShapeDtypeStruct((B * sq, DIFF_HEADS * DIFF_V), BF16),
        scratch_shapes=_attn_scratch(2, DIFF_V, tq),
        compiler_params=_cparams(("parallel", "parallel", "arbitrary")),
        name="diff_attn",
    )(lam_vecs, gn, q, k, vt)


def _out_proj_kernel(*refs, n_in):
    a_refs = refs[:n_in]
    (w_ref, x_ref, g_ref, b_ref, wr_ref, bias_ref, tri_ref,
     o_ref, gate_ref, route_ref, cnt_out_ref, gt_ref, oh_ref, cnt_ref) = refs[n_in:]
    y = None
    off = 0
    for a_ref in a_refs:
        width = a_ref.shape[1]
        part = _dot(a_ref[...], w_ref[off:off + width, :])
        y = part if y is None else y + part
        off += width
    x1 = _layer_norm(DN_ALPHA * x_ref[...] + y, g_ref[...], b_ref[...])
    o_ref[...] = x1
    _route(x1, wr_ref, bias_ref, tri_ref, gate_ref, route_ref, cnt_out_ref, gt_ref, oh_ref, cnt_ref)


def _out_proj(acts, w, x, g, b, wr_t, bias, tm):
    T, D = x.shape
    tri = jnp.asarray(np.triu(np.ones((tm, tm), np.float32), 1)).astype(BF16)
    row = lambda i: (i, 0)
    const = lambda i: (0, 0)
    return pl.pallas_call(
        functools.partial(_out_proj_kernel, n_in=len(acts)),
        grid=(T // tm,),
        in_specs=[pl.BlockSpec((tm, a.shape[1]), row) for a in acts] + [
            pl.BlockSpec(w.shape, const), pl.BlockSpec((tm, D), row),
            pl.BlockSpec(g.shape, const), pl.BlockSpec(b.shape, const),
            pl.BlockSpec(wr_t.shape, const), pl.BlockSpec(bias.shape, const), pl.BlockSpec(tri.shape, const),
        ],
        out_specs=[pl.BlockSpec((tm, D), row), pl.BlockSpec((tm, LANES), row),
                   pl.BlockSpec((8, tm), lambda i: (0, i)), pl.BlockSpec((SEG_ROWS, LANES), const)],
        out_shape=[jax.ShapeDtypeStruct((T, D), F32), jax.ShapeDtypeStruct((T, LANES), F32),
                   jax.ShapeDtypeStruct((8, T), jnp.int32), jax.ShapeDtypeStruct((SEG_ROWS, LANES), F32)],
        scratch_shapes=[pltpu.VMEM((LANES, tm), F32), pltpu.VMEM((SEG_ROWS, tm), F32),
                        pltpu.VMEM((SEG_ROWS, LANES), F32)],
        compiler_params=_cparams(("arbitrary",)),
        name="out_proj",
    )(*acts, w, x, g, b, wr_t, bias, tri)


def _odd_in_kernel(x_ref, w_ref, tab_ref, q_ref, kt_ref, kb_ref, vf_ref, vt_ref):
    xb = x_ref[...].astype(BF16)
    W = DIFF_HEADS * 2 * DIFF_HD
    hq = _dot(xb, w_ref[:, 0:W])
    hk = _dot(xb, w_ref[:, W:2 * W])
    for h in range(W // LANES):
        sl = slice(h * LANES, (h + 1) * LANES)
        q_ref[:, sl] = _rope(hq[:, sl], tab_ref[0], tab_ref[1], tab_ref[2], DIFF_HD // 2).astype(BF16)
        kk = _rope(hk[:, sl], tab_ref[3], tab_ref[4], tab_ref[5], DIFF_HD // 2)
        kt_ref[0, sl, :] = kk.T
        kb_ref[:, sl] = kk.astype(BF16)
    hv = _dot(xb, w_ref[:, 2 * W:])
    vf_ref[...] = hv
    vt_ref[0] = hv.T.astype(BF16)


def _odd_in(x, w, tabs, S, tm):
    T, D = x.shape
    W = DIFF_HEADS * 2 * DIFF_HD
    nt = tabs.shape[1] // tm
    row = lambda i: (i, 0)
    blk = pl.BlockSpec((tm, W), row)
    vt_spec, vt_shape = _vt_layout(T, S, tm, DIFF_HEADS * DIFF_V)
    kt_spec, kt_shape = _vt_layout(T, S, tm, W, F32)
    return pl.pallas_call(
        _odd_in_kernel,
        grid=(T // tm,),
        in_specs=[pl.BlockSpec((tm, D), row), pl.BlockSpec(w.shape, lambda i: (0, 0)),
                  pl.BlockSpec((tabs.shape[0], tm, LANES), lambda i: (0, i % nt, 0))],
        out_specs=[blk, kt_spec, blk, blk, vt_spec],
        out_shape=[jax.ShapeDtypeStruct((T, W), BF16), kt_shape,
                   jax.ShapeDtypeStruct((T, W), BF16), jax.ShapeDtypeStruct((T, W), F32), vt_shape],
        compiler_params=_cparams(("parallel",)),
        name="odd_in",
    )(x, w, tabs)


def _route(x, wr_ref, bias_ref, tri_ref, g_ref, route_ref, cnt_out_ref, gt_ref, oh_ref, cnt_ref):
    tm = x.shape[0]
    logits = _dot_nt(wr_ref[...], x.astype(BF16))
    sc = jax.nn.sigmoid(logits)
    sel = sc + bias_ref[...]
    r = [sel[e:e + 1, :] for e in range(N_EXPERTS)]
    s = [sc[e:e + 1, :] for e in range(N_EXPERTS)]
    grp = []
    for g in range(N_GROUPS):
        a, b, c, d = r[4 * g:4 * g + 4]
        top2 = jnp.maximum(jnp.maximum(jnp.maximum(a + b, a + c), jnp.maximum(a + d, b + c)),
                           jnp.maximum(b + d, c + d))
        grp.append(top2)
    best = jnp.maximum(jnp.maximum(grp[0], grp[1]), jnp.maximum(grp[2], grp[3]))
    taken = jnp.zeros((1, tm), jnp.bool_)
    chosen = []
    for g in range(N_GROUPS):
        win = jnp.logical_and(grp[g] == best, jnp.logical_not(taken))
        chosen.append(win)
        taken = jnp.logical_or(taken, win)
    picked = []
    for e in range(N_EXPERTS):
        g = e // EXPERTS_PER_GROUP
        rank = jnp.zeros((1, tm), F32)
        for k in range(4 * g, 4 * g + 4):
            if k < e:
                rank = rank + (r[k] >= r[e]).astype(F32)
            elif k > e:
                rank = rank + (r[k] > r[e]).astype(F32)
        picked.append(jnp.logical_and(chosen[g], rank < 2.0))
    w = [jnp.where(picked[e], s[e], 0.0) for e in range(N_EXPERTS)]
    denom = w[0]
    for e in range(1, N_EXPERTS):
        denom = denom + w[e]
    gt_ref[...] = jnp.zeros_like(gt_ref)
    for e in range(N_EXPERTS):
        gt_ref[e:e + 1, :] = w[e] / denom
    g_ref[...] = gt_ref[...].T

    @pl.when(pl.program_id(0) == 0)
    def _():
        cnt_ref[...] = jnp.zeros_like(cnt_ref)

    oh_ref[...] = jnp.zeros_like(oh_ref)
    for g in range(N_GROUPS):
        for p, (a, b) in enumerate(EXPERT_PAIRS):
            both = jnp.logical_and(picked[4 * g + a], picked[4 * g + b])
            oh_ref[g * N_PAIRS + p:g * N_PAIRS + p + 1, :] = both.astype(F32)
    oh = oh_ref[...]
    before = _dot(oh.astype(BF16), tri_ref[...])
    base = cnt_ref[:, 0:1]
    rank = jnp.sum(oh * (base + before), 0, keepdims=True)
    sid = jnp.sum(oh * lax.broadcasted_iota(jnp.int32, oh.shape, 0).astype(F32), 0, keepdims=True)
    row = lax.broadcasted_iota(jnp.int32, route_ref.shape, 0)
    route_ref[...] = jnp.where(row == 0, sid, jnp.where(row == 1, rank, 0.0)).astype(jnp.int32)
    cnt_ref[...] = cnt_ref[...] + jnp.sum(oh, 1, keepdims=True)
    cnt_out_ref[...] = cnt_ref[...]


def _moe_dense_kernel(x_ref, gate_ref, wg_ref, wu_ref, wd_ref, g_ref, b_ref, o_ref, xb_ref, acc_ref):
    e = pl.program_id(1)

    @pl.when(e == 0)
    def _():
        xb_ref[...] = x_ref[...].astype(BF16)
        acc_ref[...] = jnp.zeros_like(acc_ref)

    xb = xb_ref[...]
    h = jax.nn.silu(_dot(xb, wg_ref[0, 0].astype(BF16))) * _dot(xb, wu_ref[0, 0].astype(BF16))
    y = _dot(h.astype(BF16), wd_ref[0, 0].astype(BF16))
    gates = gate_ref[...]
    lane = lax.broadcasted_iota(jnp.int32, gates.shape, 1)
    gcol = jnp.sum(jnp.where(lane == e, gates, 0.0), -1, keepdims=True)
    acc_ref[...] += gcol * y

    @pl.when(e == pl.num_programs(1) - 1)
    def _():
        o_ref[...] = _layer_norm(DN_ALPHA * x_ref[...] + acc_ref[...], g_ref[...], b_ref[...])


def _moe_dense(x, gates, wg, wu, wd, l, g, b, tm):
    T, D = x.shape
    _, E, _, H = wg.shape
    row = lambda i, e: (i, 0)
    const = lambda i, e: (0, 0)
    wsel = lambda i, e: (l, e, 0, 0)
    return pl.pallas_call(
        _moe_dense_kernel,
        grid=(T // tm, E),
        in_specs=[
            pl.BlockSpec((tm, D), row), pl.BlockSpec((tm, LANES), row),
            pl.BlockSpec((1, 1, D, H), wsel), pl.BlockSpec((1, 1, D, H), wsel),
            pl.BlockSpec((1, 1, H, D), wsel),
            pl.BlockSpec(g.shape, const), pl.BlockSpec(b.shape, const),
        ],
        out_specs=pl.BlockSpec((tm, D), row),
        out_shape=jax.ShapeDtypeStruct((T, D), F32),
        scratch_shapes=[pltpu.VMEM((tm, D), BF16), pltpu.VMEM((tm, D), F32)],
        compiler_params=_cparams(("parallel", "arbitrary")),
        name="moe_dense",
    )(x, gates, wg, wu, wd, g, b)


SC_CORES = 2
SC_SUBCORES = 16
SC_WORKERS = SC_CORES * SC_SUBCORES
SC_TILE_BYTES = 384 * 1024
SPARSE_ROW_TILE = 512
SPARSE_MIN_TOKENS = 4096


def _sc_ring(per_w, row_bytes):
    for ch, nbuf in ((16, 4), (16, 2), (8, 2)):
        if per_w % (ch * nbuf) == 0 and ch * nbuf * row_bytes <= SC_TILE_BYTES:
            return ch, nbuf
    raise ValueError(f"no SparseCore gather ring for {per_w} rows of {row_bytes} bytes per subcore")


def _sc_gather(table, idx):
    R = idx.shape[0]
    D = table.shape[1]
    per_w = R // SC_WORKERS
    assert per_w * SC_WORKERS == R and per_w % 8 == 0
    ch, nbuf = _sc_ring(per_w, D * table.dtype.itemsize)
    nchunk = per_w // ch
    mesh = plsc.VectorSubcoreMesh(core_axis_name="c", subcore_axis_name="s")

    @functools.partial(
        pl.kernel, mesh=mesh,
        out_type=jax.ShapeDtypeStruct((R, D), table.dtype),
        scratch_types=[pltpu.VMEM((per_w,), jnp.int32), pltpu.VMEM((nbuf, ch, D), table.dtype),
                       pltpu.SemaphoreType.DMA((nbuf,)), pltpu.SemaphoreType.DMA((nbuf,))],
    )
    def gather_kernel(table_hbm, idx_hbm, out_hbm, idx_v, rows_v, gsem, wsem):
        base = (lax.axis_index("s") * SC_CORES + lax.axis_index("c")) * per_w
        pltpu.sync_copy(idx_hbm.at[pl.ds(base, per_w)], idx_v)

        def gather(c, b):
            return pltpu.make_async_copy(table_hbm.at[idx_v.at[pl.ds(c * ch, ch)]], rows_v.at[b], gsem.at[b])

        def write(c, b):
            return pltpu.make_async_copy(rows_v.at[b], out_hbm.at[pl.ds(base + c * ch, ch)], wsem.at[b])

        for b in range(nbuf - 1):
            gather(b, b).start()

        @pl.loop(0, nchunk, step=nbuf)
        def _(c):
            for b in range(nbuf):
                cc = c + b
                gather(cc, b).wait()
                write(cc, b).start()
                pb = (b - 1) % nbuf

                @pl.when(cc + nbuf - 1 < nchunk)
                def _():
                    @pl.when(cc >= 1)
                    def _():
                        write(cc - 1, pb).wait()

                    gather(cc + nbuf - 1, pb).start()

        for b in range(nbuf):
            write(nchunk - nbuf + b, b).wait()

    return gather_kernel(table, idx)


def _moe_group_kernel(te_ref, nv_ref, x_ref, gate_ref, *refs):
    w_refs, (g_ref, b_ref, o_ref) = refs[:3 * TOP_K], refs[3 * TOP_K:]
    j = pl.program_id(0)
    live = j < nv_ref[0]

    @pl.when(live)
    def _():
        x = x_ref[...]
        xb = x.astype(BF16)
        gates = gate_ref[...]
        lane = lax.broadcasted_iota(jnp.int32, gates.shape, 1)
        acc = None
        for k in range(TOP_K):
            wg_ref, wu_ref, wd_ref = w_refs[3 * k:3 * k + 3]
            h = jax.nn.silu(_dot(xb, wg_ref[0, 0].astype(BF16))) * _dot(xb, wu_ref[0, 0].astype(BF16))
            y = _dot(h.astype(BF16), wd_ref[0, 0].astype(BF16))
            gcol = jnp.sum(jnp.where(lane == te_ref[TOP_K * j + k], gates, 0.0), -1, keepdims=True)
            acc = gcol * y if acc is None else acc + gcol * y
        o_ref[...] = _layer_norm(DN_ALPHA * x + acc, g_ref[...], b_ref[...])

    @pl.when(jnp.logical_not(live))
    def _():
        o_ref[...] = jnp.zeros_like(o_ref)


def _moe_group(tile_experts, n_valid, xs, gs, wg, wu, wd, l, g, b, tm):
    R, D = xs.shape
    H = wg.shape[3]
    row = lambda j, te, nv: (j, 0)
    const = lambda j, te, nv: (0, 0)
    w_specs, w_args = [], []
    for k in range(TOP_K):
        wsel = lambda j, te, nv, k=k: (l, te[TOP_K * j + k], 0, 0)
        w_specs += [pl.BlockSpec((1, 1, D, H), wsel), pl.BlockSpec((1, 1, D, H), wsel),
                    pl.BlockSpec((1, 1, H, D), wsel)]
        w_args += [wg, wu, wd]
    grid_spec = pltpu.PrefetchScalarGridSpec(
        num_scalar_prefetch=2,
        grid=(R // tm,),
        in_specs=[pl.BlockSpec((tm, D), row), pl.BlockSpec((tm, LANES), row)] + w_specs + [
            pl.BlockSpec(g.shape, const), pl.BlockSpec(b.shape, const)],
        out_specs=pl.BlockSpec((tm, D), row),
    )
    return pl.pallas_call(
        _moe_group_kernel,
        grid_spec=grid_spec,
        out_shape=jax.ShapeDtypeStruct((R, D), F32),
        compiler_params=_cparams(("arbitrary",)),
        name="moe_group",
    )(tile_experts, n_valid, xs, gs, *w_args, g, b)


def _sc_scatter(x, dest, pad_dest, n_rows):
    T, D = x.shape
    P = pad_dest.shape[0]
    per_w, pad_w = T // SC_WORKERS, P // SC_WORKERS
    assert per_w * SC_WORKERS == T and pad_w * SC_WORKERS == P and T + P == n_rows
    ch, nbuf = _sc_ring(per_w, D * x.dtype.itemsize)
    assert pad_w % ch == 0
    nchunk, npad = per_w // ch, pad_w // ch
    mesh = plsc.VectorSubcoreMesh(core_axis_name="c", subcore_axis_name="s")

    @functools.partial(
        pl.kernel, mesh=mesh,
        out_type=jax.ShapeDtypeStruct((n_rows, D), x.dtype),
        scratch_types=[pltpu.VMEM((nchunk, ch), jnp.int32), pltpu.VMEM((npad, ch), jnp.int32),
                       pltpu.VMEM((nbuf, ch, D), x.dtype), pltpu.VMEM((ch, D), x.dtype),
                       pltpu.SemaphoreType.DMA((nbuf,)), pltpu.SemaphoreType.DMA((nbuf,)),
                       pltpu.SemaphoreType.DMA],
    )
    def scatter_kernel(x_hbm, dest_hbm, pad_hbm, zero_hbm, out_hbm, dest_v, pad_v, rows_v, zero_v,
                       rsem, wsem, zsem):
        wid = lax.axis_index("s") * SC_CORES + lax.axis_index("c")
        base = wid * per_w
        pltpu.sync_copy(dest_hbm.at[wid], dest_v)
        pltpu.sync_copy(pad_hbm.at[wid], pad_v)
        pltpu.sync_copy(zero_hbm, zero_v)

        def read(c, b):
            return pltpu.make_async_copy(x_hbm.at[pl.ds(base + c * ch, ch)], rows_v.at[b], rsem.at[b])

        def write(c, b):
            return pltpu.make_async_copy(rows_v.at[b], out_hbm.at[dest_v.at[c]], wsem.at[b])

        def write_zero(c):
            return pltpu.make_async_copy(zero_v, out_hbm.at[pad_v.at[c]], zsem)

        for c in range(npad):
            write_zero(c).start()
        for b in range(nbuf - 1):
            read(b, b).start()

        @pl.loop(0, nchunk, step=nbuf)
        def _(c):
            for b in range(nbuf):
                cc = c + b
                read(cc, b).wait()
                write(cc, b).start()
                pb = (b - 1) % nbuf

                @pl.when(cc + nbuf - 1 < nchunk)
                def _():
                    @pl.when(cc >= 1)
                    def _():
                        write(cc - 1, pb).wait()

                    read(cc + nbuf - 1, pb).start()

        for b in range(nbuf):
            write(nchunk - nbuf + b, b).wait()
        for c in range(npad):
            write_zero(c).wait()

    return scatter_kernel(x, dest.reshape(SC_WORKERS, nchunk, ch), pad_dest.reshape(SC_WORKERS, npad, ch),
                          jnp.zeros((ch, D), x.dtype))


def _moe_sparse(x, gates, route, counts, wg, wu, wd, l, g, b):
    T = x.shape[0]
    tm = SPARSE_ROW_TILE
    n_tiles = T // tm + N_SEG
    n_rows = n_tiles * tm
    sid, rank = route[0], route[1]
    cnt = counts[:N_SEG, 0].astype(jnp.int32)
    tiles = (cnt + tm - 1) // tm
    ends = jnp.cumsum(tiles)
    starts = (ends - tiles) * tm
    seg_ids = jnp.arange(N_SEG, dtype=jnp.int32)
    pick = lambda which, vals: jnp.sum(jnp.where(which[:, None] == seg_ids[None, :], vals[None, :], 0), 1)
    passed = lambda pos, bounds: jnp.sum((pos[:, None] >= bounds[None, :]).astype(jnp.int32), 1)
    dest = rank + pick(sid, starts)
    pad_cnt = tiles * tm - cnt
    pad_end = jnp.cumsum(pad_cnt)
    p = jnp.arange(n_rows - T, dtype=jnp.int32)
    seg = passed(p, pad_end)
    pad_dest = jnp.where(seg < N_SEG,
                         pick(seg, starts + cnt) + p - pick(seg, pad_end - pad_cnt),
                         ends[N_SEG - 1] * tm + p - pad_end[N_SEG - 1])
    tile_seg = jnp.minimum(passed(jnp.arange(n_tiles, dtype=jnp.int32), ends), N_SEG - 1)
    pair = jnp.asarray(np.asarray(EXPERT_PAIRS, np.int32))
    tile_experts = (tile_seg // N_PAIRS * EXPERTS_PER_GROUP)[:, None] + pair[tile_seg % N_PAIRS]
    xs = _sc_scatter(x, dest, pad_dest, n_rows)
    gs = _sc_scatter(gates, dest, pad_dest, n_rows)
    ys = _moe_group(tile_experts.reshape(-1), ends[N_SEG - 1:], xs, gs, wg, wu, wd, l, g, b, tm)
    return _sc_gather(ys, dest)


def _prep_weights(w_in_even, w_uq, w_ukv, w_out_even, w_in_odd, w_out_odd, w_router,
                  w_expert_gate, w_expert_up, w_expert_down):
    n_main = 4 * RET_W + MLA_Q_RANK + MLA_KV_RANK
    w_in = w_in_even[0]
    kr_cols = jnp.pad(w_in[:, n_main:], ((0, 0), (MLA_NOPE, LANES - MLA_NOPE - MLA_ROPE)))
    w_a = jnp.concatenate([w_in[:, :n_main], kr_cols], 1).astype(BF16)
    qd = MLA_NOPE + MLA_ROPE
    wq = jnp.pad(w_uq[0].reshape(MLA_Q_RANK, MLA_HEADS, qd), ((0, 0), (0, 0), (0, LANES - qd)))
    wq = wq.reshape(MLA_Q_RANK, MLA_PAD).astype(BF16)
    wkv = w_ukv[0].reshape(MLA_KV_RANK, MLA_HEADS, MLA_NOPE + MLA_V)
    wk = jnp.pad(wkv[:, :, :MLA_NOPE], ((0, 0), (0, 0), (0, LANES - MLA_NOPE)))
    wk = wk.reshape(MLA_KV_RANK, MLA_PAD).astype(BF16)
    wvt = wkv[:, :, MLA_NOPE:].reshape(MLA_KV_RANK, MLA_HEADS * MLA_V).T.astype(BF16)
    e_np = np.zeros((MLA_ROPE, MLA_HEADS, LANES), np.float32)
    for j in range(MLA_ROPE):
        e_np[j, :, MLA_NOPE + j] = 1.0
    e_mat = jnp.asarray(e_np.reshape(MLA_ROPE, MLA_PAD)).astype(BF16)
    return dict(
        w_a=w_a, wq=wq, wk=wk, wvt=wvt, e_mat=e_mat,
        w_out_even=w_out_even[0].astype(BF16), w_in_odd=w_in_odd[0].astype(BF16),
        w_out_odd=w_out_odd[0].astype(BF16), wr_t=w_router.T.astype(BF16),
        wg=w_expert_gate, wu=w_expert_up, wd=w_expert_down,
    )


def _moe(routed, wts, l, ln_g, ln_b, tm):
    x, gates, route, counts = routed
    g, b = ln_g[l, 1][None], ln_b[l, 1][None]
    if x.shape[0] >= SPARSE_MIN_TOKENS:
        return _moe_sparse(x, gates, route, counts, wts["wg"], wts["wu"], wts["wd"], l, g, b)
    return _moe_dense(x, gates, wts["wg"], wts["wu"], wts["wd"], l, g, b, tm)


def _trunk(x3, pos0, past, wts, prm):
    B, S, D = x3.shape
    T = B * S
    x = x3.reshape(T, D)
    tm = min(T, TOKEN_TILE)
    rep = max(tm // S, 1)
    pos = np.tile(pos0 + np.arange(S), rep)
    ln_g, ln_b = prm["ln_g"], prm["ln_b"]

    rq, rk, rv, rg, q, lat, kr = _even_in(x, wts["w_a"], wts["wq"], prm["gq"], prm["gkv"], _even_tables(pos), tm)
    causal = past is None
    if causal:
        state0 = jnp.zeros((B, RET_HEADS, RET_DK, RET_DV), F32)
        lat_all, kr_all, sk, skp = lat, kr, S, S
        tq = tk = min(S, ATTN_TILE)
    else:
        state0 = past["state"]
        sk = past["lat"].shape[1] + S
        skp = -(-sk // LANES) * LANES
        padk = lambda parts: jnp.concatenate(
            parts + [jnp.zeros((B, skp - sk, parts[0].shape[2]), parts[0].dtype)], 1).reshape(B * skp, -1)
        lat_all = padk([past["lat"], lat.reshape(B, S, -1)])
        kr_all = padk([past["kr"], kr.reshape(B, S, -1)])
        tq, tk = S, skp
    ret_out, ret_state = _retention(rq, rk, rv, rg, state0, B, S)
    tkv = KV_UP_TILE if (B * skp) % KV_UP_TILE == 0 else skp
    k_mla, vt_mla = _kv_up(lat_all, kr_all, wts["wk"], wts["wvt"], wts["e_mat"], skp, tkv)
    if vt_mla.shape[0] != B:
        vt_mla = vt_mla.reshape(-1, B, skp).transpose(1, 0, 2)
    mla_out = _mla_attn(q, k_mla, vt_mla, B, S, skp, sk, tq, tk, causal)
    routed = _out_proj([ret_out, mla_out], wts["w_out_even"], x, ln_g[0, 0][None], ln_b[0, 0][None],
                       wts["wr_t"], prm["bias"], tm)
    x = _moe(routed, wts, 0, ln_g, ln_b, tm)

    qd, kt, kb, vf, vt = _odd_in(x, wts["w_in_odd"], _odd_tables(pos), S, tm)
    if vt.shape[0] != B:
        vt = vt.reshape(-1, B, S).transpose(1, 0, 2)
        kt = kt.reshape(-1, B, S).transpose(1, 0, 2)
    kf = kt.reshape(1, B, 2 * DIFF_HEADS, DIFF_HD, S).transpose(0, 1, 4, 2, 3)
    if causal:
        k_all, vt_all = kb, vt
    else:
        k_all = padk([past["dk"], kb.reshape(B, S, -1)])
        vt_all = jnp.concatenate([past["dv"].transpose(0, 2, 1), vt,
                                  jnp.zeros((B, vt.shape[1], skp - sk), BF16)], 2)
    lam_init = 0.8 - 0.6 * math.exp(-0.3 * 1)
    d_out = _diff_attn(prm["lam"], prm["gn"], qd, k_all, vt_all, B, S, skp, sk, tq, tk, causal, lam_init)
    routed = _out_proj([d_out], wts["w_out_odd"], x, ln_g[1, 0][None], ln_b[1, 0][None],
                       wts["wr_t"], prm["bias"], tm)
    x = _moe(routed, wts, 1, ln_g, ln_b, tm)

    return (x.reshape(B, S, D), ret_state[None], lat.reshape(1, B, S, -1), kr.reshape(1, B, S, -1),
            kf, vf.reshape(1, B, S, DIFF_HEADS, DIFF_V))


def kernel(x_prompt, x_sample, state_ret, cache_mla_latent, cache_mla_krope, cache_diff_k, cache_diff_v,
           w_in_even, w_uq, w_ukv, g_qnorm, g_kvnorm, w_out_even,
           w_in_odd, lambda_q1, lambda_k1, lambda_q2, lambda_k2, g_diff_norm, w_out_odd,
           ln_g, ln_b, w_router, router_bias, w_expert_gate, w_expert_up, w_expert_down):
    wts = _prep_weights(w_in_even, w_uq, w_ukv, w_out_even, w_in_odd, w_out_odd, w_router,
                        w_expert_gate, w_expert_up, w_expert_down)
    prm = dict(
        gq=g_qnorm[0][None].astype(F32), gkv=g_kvnorm[0][None].astype(F32),
        lam=jnp.stack([lambda_q1[0], lambda_k1[0], lambda_q2[0], lambda_k2[0]]).astype(F32),
        gn=g_diff_norm[0][None].astype(F32), bias=router_bias.reshape(N_EXPERTS, 1).astype(F32),
        ln_g=ln_g.astype(F32), ln_b=ln_b.astype(F32),
    )
    past_len = cache_mla_latent.shape[2]
    db = x_sample.shape[0]
    past = dict(
        state=state_ret[0].astype(F32), lat=cache_mla_latent[0], kr=cache_mla_krope[0],
        dk=cache_diff_k[0].reshape(db, past_len, -1).astype(BF16),
        dv=cache_diff_v[0].reshape(db, past_len, -1).astype(BF16),
    )
    outs_p = _trunk(x_prompt, 0, None, wts, prm)
    outs_s = _trunk(x_sample, past_len, past, wts, prm)
    return (outs_p[0], outs_s[0]) + outs_p[1:] + outs_s[1:]
```

```python
import functools
import math

import numpy as np
import jax
import jax.numpy as jnp
from jax import lax
from jax.experimental import pallas as pl
from jax.experimental.pallas import tpu as pltpu
from jax.experimental.pallas import tpu_sc as plsc

F32 = jnp.float32
BF16 = jnp.bfloat16

CHUNK = 64
ROPE_THETA = 10000.0
NEG_INF = -1e30
LN_EPS = 1e-5
NORM_EPS = 1e-6
DEPTH = 2
DN_ALPHA = (2.0 * DEPTH) ** 0.25
RET_HEADS = 4
RET_DK = 128
RET_DV = 128
RET_LOG_GAMMA = tuple(math.log(1.0 - 2.0 ** (-5 - h)) for h in range(RET_HEADS))
MLA_HEADS = 8
MLA_Q_RANK = 384
MLA_KV_RANK = 256
MLA_NOPE = 64
MLA_ROPE = 32
MLA_V = 64
DIFF_HEADS = 8
DIFF_HD = 64
DIFF_V = 128
N_EXPERTS = 16
N_GROUPS = 4
EXPERTS_PER_GROUP = 4
TOP_K = 2
EXPERT_PAIRS = tuple((a, b) for a in range(EXPERTS_PER_GROUP) for b in range(a + 1, EXPERTS_PER_GROUP))
N_PAIRS = len(EXPERT_PAIRS)
N_SEG = N_GROUPS * N_PAIRS
SEG_ROWS = 32
LOG2E = math.log2(math.e)

LANES = 128
RET_W = RET_HEADS * RET_DK
MLA_PAD = MLA_HEADS * LANES
VMEM_LIMIT = 56 * 1024 * 1024
TOKEN_TILE = 1024
KV_UP_TILE = 512
RET_CHUNK = 256
RET_TILE = 512
ATTN_TILE = 1024


def _cparams(sem):
    return pltpu.CompilerParams(dimension_semantics=sem, vmem_limit_bytes=VMEM_LIMIT)


def _rope_tables(pos, d, group, offset, scale):
    pos = np.asarray(pos, np.float64)
    half = d // 2
    inv = 1.0 / (ROPE_THETA ** (np.arange(0, d, 2, dtype=np.float64) / d))
    ang = pos[:, None] * inv[None, :]
    cos = np.full((pos.shape[0], LANES), scale, np.float64)
    s_lo = np.zeros((pos.shape[0], LANES), np.float64)
    s_hi = np.zeros((pos.shape[0], LANES), np.float64)
    start = offset
    while start + d <= LANES:
        cos[:, start:start + half] = np.cos(ang) * scale
        cos[:, start + half:start + d] = np.cos(ang) * scale
        s_lo[:, start:start + half] = -np.sin(ang) * scale
        s_hi[:, start + half:start + d] = np.sin(ang) * scale
        start += group
    return cos, s_lo, s_hi


def _even_tables(pos):
    rq = _rope_tables(pos, RET_DK, LANES, 0, 1.0)
    rk = _rope_tables(pos, RET_DK, LANES, 0, RET_DK ** -0.5)
    c = (MLA_NOPE + MLA_ROPE) ** -0.5 * LOG2E
    mq = _rope_tables(pos, MLA_ROPE, LANES, MLA_NOPE, c)
    mk = _rope_tables(pos, MLA_ROPE, LANES, MLA_NOPE, 1.0)
    tabs = [rq[0], rq[1] + rq[2], rk[0], rk[1] + rk[2], mq[0], mq[1], mq[2], mk[0], mk[1], mk[2]]
    return jnp.asarray(np.stack(tabs).astype(np.float32))


def _odd_tables(pos):
    c = DIFF_HD ** -0.5 * LOG2E
    dq = _rope_tables(pos, DIFF_HD, DIFF_HD, 0, c)
    dk = _rope_tables(pos, DIFF_HD, DIFF_HD, 0, 1.0)
    return jnp.asarray(np.stack(list(dq) + list(dk)).astype(np.float32))


def _retention_tables(L):
    lg = np.asarray(RET_LOG_GAMMA, np.float64)
    idx = np.arange(L, dtype=np.float64)
    diff = idx[:, None] - idx[None, :]
    dmask = np.where(diff[None] >= 0, np.exp(np.maximum(diff, 0.0)[None] * lg[:, None, None]), 0.0)
    qd = np.exp((idx[None, :] + 1.0) * lg[:, None])
    kd = np.exp((L - 1.0 - idx)[None, :] * lg[:, None])
    gl = np.exp(L * lg)
    qd = np.broadcast_to(qd[:, :, None], (RET_HEADS, L, LANES))
    kd = np.broadcast_to(kd[:, :, None], (RET_HEADS, L, LANES))
    gl = np.broadcast_to(gl[:, None, None], (RET_HEADS, RET_DK, RET_DV))
    f = lambda a: jnp.asarray(np.ascontiguousarray(a).astype(np.float32))
    return f(dmask), f(qd), f(kd), f(gl)


def _dot(a, b):
    return jnp.dot(a, b, preferred_element_type=F32)


def _dot_nt(a, b):
    return lax.dot_general(a, b, (((1,), (1,)), ((), ())), preferred_element_type=F32)


def _dot_tn(a, b):
    return lax.dot_general(a, b, (((0,), (0,)), ((), ())), preferred_element_type=F32)


def _rope(x, cos, s_lo, s_hi, half):
    return x * cos + pltpu.roll(x, LANES - half, 1) * s_lo + pltpu.roll(x, half, 1) * s_hi


def _layer_norm(x, g, b):
    mu = jnp.mean(x, -1, keepdims=True)
    xc = x - mu
    var = jnp.mean(xc * xc, -1, keepdims=True)
    return xc * lax.rsqrt(var + LN_EPS) * g + b


def _rms_norm(x, g):
    return x * lax.rsqrt(jnp.mean(x * x, -1, keepdims=True) + NORM_EPS) * g


def _even_in_kernel(x_ref, w_ref, wq_ref, gq_ref, gkv_ref, tab_ref,
                    rq_ref, rk_ref, rv_ref, rg_ref, q_ref, lat_ref, kr_ref):
    xb = x_ref[...].astype(BF16)
    c_rq, s_rq, c_rk, s_rk = tab_ref[0], tab_ref[1], tab_ref[2], tab_ref[3]
    hq = _dot(xb, w_ref[:, 0:RET_W])
    hk = _dot(xb, w_ref[:, RET_W:2 * RET_W])
    for h in range(RET_HEADS):
        sl = slice(h * LANES, (h + 1) * LANES)
        xq = hq[:, sl]
        rq_ref[:, sl] = (xq * c_rq + pltpu.roll(xq, RET_DK // 2, 1) * s_rq).astype(BF16)
        xk = hk[:, sl]
        rk_ref[:, sl] = (xk * c_rk + pltpu.roll(xk, RET_DK // 2, 1) * s_rk).astype(BF16)
    rv_ref[...] = _dot(xb, w_ref[:, 2 * RET_W:3 * RET_W]).astype(BF16)
    rg_ref[...] = _dot(xb, w_ref[:, 3 * RET_W:4 * RET_W]).astype(BF16)
    o = 4 * RET_W
    cq = _dot(xb, w_ref[:, o:o + MLA_Q_RANK])
    qn = _rms_norm(cq, gq_ref[...]).astype(BF16)
    qf = _dot(qn, wq_ref[...])
    c_q, lo_q, hi_q = tab_ref[4], tab_ref[5], tab_ref[6]
    for h in range(MLA_HEADS):
        sl = slice(h * LANES, (h + 1) * LANES)
        q_ref[:, sl] = _rope(qf[:, sl], c_q, lo_q, hi_q, MLA_ROPE // 2).astype(BF16)
    o += MLA_Q_RANK
    ckv = _dot(xb, w_ref[:, o:o + MLA_KV_RANK])
    lat_ref[...] = _rms_norm(ckv, gkv_ref[...])
    o += MLA_KV_RANK
    krp = _dot(xb, w_ref[:, o:o + LANES])
    krp = _rope(krp, tab_ref[7], tab_ref[8], tab_ref[9], MLA_ROPE // 2)
    kr_ref[...] = krp[:, MLA_NOPE:MLA_NOPE + MLA_ROPE]


def _even_in(x, w_a, wq, gq, gkv, tabs, tm):
    T, D = x.shape
    P = tabs.shape[1]
    nt = P // tm
    row = lambda i: (i, 0)
    const = lambda i: (0, 0)
    outs = [
        jax.ShapeDtypeStruct((T, RET_W), BF16), jax.ShapeDtypeStruct((T, RET_W), BF16),
        jax.ShapeDtypeStruct((T, RET_W), BF16), jax.ShapeDtypeStruct((T, RET_W), BF16),
        jax.ShapeDtypeStruct((T, MLA_PAD), BF16),
        jax.ShapeDtypeStruct((T, MLA_KV_RANK), F32), jax.ShapeDtypeStruct((T, MLA_ROPE), F32),
    ]
    return pl.pallas_call(
        _even_in_kernel,
        grid=(T // tm,),
        in_specs=[
            pl.BlockSpec((tm, D), row),
            pl.BlockSpec(w_a.shape, const),
            pl.BlockSpec(wq.shape, const),
            pl.BlockSpec(gq.shape, const),
            pl.BlockSpec(gkv.shape, const),
            pl.BlockSpec((tabs.shape[0], tm, LANES), lambda i: (0, i % nt, 0)),
        ],
        out_specs=[
            pl.BlockSpec((tm, RET_W), row), pl.BlockSpec((tm, RET_W), row),
            pl.BlockSpec((tm, RET_W), row), pl.BlockSpec((tm, RET_W), row),
            pl.BlockSpec((tm, MLA_PAD), row),
            pl.BlockSpec((tm, MLA_KV_RANK), row), pl.BlockSpec((tm, MLA_ROPE), row),
        ],
        out_shape=outs,
        compiler_params=_cparams(("parallel",)),
        name="even_in",
    )(x, w_a, wq, gq, gkv, tabs)


def _kv_up_kernel(lat_ref, kr_ref, wk_ref, wvt_ref, e_ref, k_ref, vt_ref):
    lb = lat_ref[...].astype(BF16)
    krb = kr_ref[...].astype(BF16)
    k_ref[...] = (_dot(lb, wk_ref[...]) + _dot(krb, e_ref[...])).astype(BF16)
    vt_ref[0] = _dot_nt(wvt_ref[...], lb).astype(BF16)


def _vt_layout(T, S, tm, width, dtype=BF16):
    nb, cols = (T // S, S) if S % tm == 0 else (1, T)
    nt = cols // tm
    spec = pl.BlockSpec((1, width, tm), lambda i: (i // nt, 0, i % nt))
    return spec, jax.ShapeDtypeStruct((nb, width, cols), dtype)


def _kv_up(lat, kr, wk, wvt, e_mat, S, tm):
    T = lat.shape[0]
    row = lambda i: (i, 0)
    const = lambda i: (0, 0)
    vt_spec, vt_shape = _vt_layout(T, S, tm, MLA_HEADS * MLA_V)
    return pl.pallas_call(
        _kv_up_kernel,
        grid=(T // tm,),
        in_specs=[
            pl.BlockSpec((tm, MLA_KV_RANK), row), pl.BlockSpec((tm, MLA_ROPE), row),
            pl.BlockSpec(wk.shape, const), pl.BlockSpec(wvt.shape, const), pl.BlockSpec(e_mat.shape, const),
        ],
        out_specs=[pl.BlockSpec((tm, MLA_PAD), row), vt_spec],
        out_shape=[jax.ShapeDtypeStruct((T, MLA_PAD), BF16), vt_shape],
        compiler_params=_cparams(("parallel",)),
        name="kv_up",
    )(lat, kr, wk, wvt, e_mat)


def _retention_kernel(q_ref, k_ref, v_ref, g_ref, s0_ref, dm_ref, qd_ref, kd_ref, gl_ref,
                      o_ref, st_ref, *, L, nchunk):
    @pl.when(pl.program_id(1) == 0)
    def _():
        st_ref[...] = s0_ref[...]

    for c in range(nchunk):
        rows = slice(c * L, (c + 1) * L)
        for h in range(RET_HEADS):
            sl = slice(h * LANES, (h + 1) * LANES)
            q = q_ref[rows, sl]
            k = k_ref[rows, sl]
            v = v_ref[rows, sl]
            st = st_ref[0, h]
            a = (_dot_nt(q, k) * dm_ref[h]).astype(BF16)
            o = _dot(a, v) + _dot(q, st.astype(BF16)) * qd_ref[h]
            kdec = (k.astype(F32) * kd_ref[h]).astype(BF16)
            st_ref[0, h] = st * gl_ref[h] + _dot_tn(kdec, v)
            mu = jnp.mean(o, -1, keepdims=True)
            oc = o - mu
            var = jnp.mean(oc * oc, -1, keepdims=True)
            on = oc * lax.rsqrt(var + LN_EPS)
            g = g_ref[rows, sl].astype(F32)
            o_ref[rows, sl] = (g * jax.nn.sigmoid(g) * on).astype(BF16)


def _retention(rq, rk, rv, rg, state0, B, S):
    L = min(S, RET_CHUNK)
    lt = min(S, RET_TILE)
    nj = S // lt
    dm, qd, kd, gl = _retention_tables(L)
    row = lambda b, j: (b * nj + j, 0)
    c3 = lambda b, j: (0, 0, 0)
    st_spec = pl.BlockSpec((1, RET_HEADS, RET_DK, RET_DV), lambda b, j: (b, 0, 0, 0))
    return pl.pallas_call(
        functools.partial(_retention_kernel, L=L, nchunk=lt // L),
        grid=(B, nj),
        in_specs=[pl.BlockSpec((lt, RET_W), row)] * 4 + [
            st_spec,
            pl.BlockSpec(dm.shape, c3), pl.BlockSpec(qd.shape, c3),
            pl.BlockSpec(kd.shape, c3), pl.BlockSpec(gl.shape, c3),
        ],
        out_specs=[pl.BlockSpec((lt, RET_W), row), st_spec],
        out_shape=[jax.ShapeDtypeStruct((B * S, RET_W), BF16),
                   jax.ShapeDtypeStruct((B, RET_HEADS, RET_DK, RET_DV), F32)],
        compiler_params=_cparams(("parallel", "arbitrary")),
        name="retention",
    )(rq, rk, rv, rg, state0, dm, qd, kd, gl)


def _query_t(q, tq):
    q = q.astype(F32)
    if tq < LANES:
        q = jnp.concatenate([q, jnp.zeros((LANES - tq, LANES), F32)], 0)
    return q.T


def _flash_t(streams, qi, tq, tk, sk, sk_valid, causal, m_ref, l_ref, acc_ref):
    for s in range(len(streams)):
        m_ref[s] = jnp.full(m_ref.shape[1:], NEG_INF, F32)
        l_ref[s] = jnp.zeros(l_ref.shape[1:], F32)
        acc_ref[s] = jnp.zeros(acc_ref.shape[1:], F32)

    def step(start, size, rel=None, valid=None):
        for s, (q_t, k_at, vt_at) in enumerate(streams):
            st = _dot(k_at(start, size), q_t)
            if rel is not None:
                kc = (lax.broadcasted_iota(jnp.int32, st.shape, 0) + rel) // CHUNK
                qc = lax.broadcasted_iota(jnp.int32, st.shape, 1) // CHUNK
                st = jnp.where(kc <= qc, st, NEG_INF)
            if valid is not None:
                st = jnp.where(lax.broadcasted_iota(jnp.int32, st.shape, 0) < valid, st, NEG_INF)
            m_old = m_ref[s]
            m_new = jnp.maximum(m_old, jnp.max(st, 0, keepdims=True))
            p = jnp.exp2(st - m_new)
            alpha = jnp.exp2(m_old - m_new)
            l_ref[s] = alpha * l_ref[s] + jnp.sum(p, 0, keepdims=True)
            acc_ref[s] = acc_ref[s] * alpha + _dot(vt_at(start, size), p.astype(BF16))
            m_ref[s] = m_new

    if causal:
        n_full = qi * (tq // tk)

        def body(j, c):
            step(pl.multiple_of(2 * j * tk, tk), tk)
            step(pl.multiple_of((2 * j + 1) * tk, tk), tk)
            return c

        lax.fori_loop(0, n_full // 2, body, 0)

        def own_span():
            for d in range(tq // tk):
                step(pl.multiple_of(qi * tq + d * tk, tk), tk, rel=d * tk)

        if (tq // tk) % 2 == 1:
            @pl.when(n_full % 2 == 1)
            def _():
                step(pl.multiple_of((n_full - 1) * tk, tk), tk)
                own_span()

            @pl.when(n_full % 2 == 0)
            def _():
                own_span()
        else:
            own_span()
    else:
        for j in range(sk // tk):
            last_valid = sk_valid - j * tk
            step(j * tk, tk, valid=last_valid if last_valid < tk else None)
    return [acc_ref[s] / l_ref[s] for s in range(len(streams))]


def _attn_scratch(n_streams, dv, tq):
    tqp = max(tq, LANES)
    return [pltpu.VMEM((n_streams, 1, tqp), F32), pltpu.VMEM((n_streams, 1, tqp), F32),
            pltpu.VMEM((n_streams, dv, tqp), F32)]


def _mla_attn_kernel(q_ref, k_ref, vt_ref, o_ref, m_ref, l_ref, acc_ref, *, tq, tk, sk, sk_valid, causal):
    streams = []
    for hh in range(2):
        sl = slice(hh * LANES, (hh + 1) * LANES)
        vrows = slice(hh * MLA_V, (hh + 1) * MLA_V)
        k_at = lambda start, n, sl=sl: k_ref[pl.ds(start, n), sl]
        vt_at = lambda start, n, vrows=vrows: vt_ref[0, vrows, pl.ds(start, n)]
        streams.append((_query_t(q_ref[:, sl], tq).astype(BF16), k_at, vt_at))
    outs = _flash_t(streams, pl.program_id(2), tq, tk, sk, sk_valid, causal, m_ref, l_ref, acc_ref)
    o_ref[...] = jnp.concatenate(outs, 0).T[:tq].astype(BF16)


def _mla_attn(q, k, vt, B, sq, sk, sk_valid, tq, tk, causal):
    nq = sq // tq
    npair = MLA_HEADS // 2
    return pl.pallas_call(
        functools.partial(_mla_attn_kernel, tq=tq, tk=tk, sk=sk, sk_valid=sk_valid, causal=causal),
        grid=(B, npair, nq),
        in_specs=[
            pl.BlockSpec((tq, 2 * LANES), lambda b, p, i: (b * nq + i, p)),
            pl.BlockSpec((sk, 2 * LANES), lambda b, p, i: (b, p)),
            pl.BlockSpec((1, 2 * MLA_V, sk), lambda b, p, i: (b, p, 0)),
        ],
        out_specs=pl.BlockSpec((tq, LANES), lambda b, p, i: (b * nq + i, p)),
        out_shape=jax.ShapeDtypeStruct((B * sq, MLA_HEADS * MLA_V), BF16),
        scratch_shapes=_attn_scratch(2, MLA_V, tq),
        compiler_params=_cparams(("parallel", "parallel", "arbitrary")),
        name="mla_attn",
    )(q, k, vt)


def _diff_attn_kernel(lam_ref, gn_ref, q_ref, k_ref, vt_ref, o_ref, m_ref, l_ref, acc_ref,
                      *, tq, tk, sk, sk_valid, causal, lam_init):
    k_at = lambda start, n: k_ref[pl.ds(start, n), :]
    vt_at = lambda start, n: vt_ref[0, :, pl.ds(start, n)]
    q_t = _query_t(q_ref[...], tq)
    feat = lax.broadcasted_iota(jnp.int32, q_t.shape, 0)
    zero = jnp.zeros_like(q_t)
    q1 = jnp.where(feat < DIFF_HD, q_t, zero).astype(BF16)
    q2 = jnp.where(feat < DIFF_HD, zero, q_t).astype(BF16)
    o1, o2 = _flash_t([(q1, k_at, vt_at), (q2, k_at, vt_at)], pl.program_id(2), tq, tk, sk, sk_valid,
                      causal, m_ref, l_ref, acc_ref)
    lv = lam_ref[...]
    lam = (jnp.exp(jnp.sum(lv[0:1] * lv[1:2], -1, keepdims=True))
           - jnp.exp(jnp.sum(lv[2:3] * lv[3:4], -1, keepdims=True)) + lam_init)
    o = (o1 - lam * o2).T[:tq]
    o_ref[...] = (_rms_norm(o, gn_ref[...]) * (1.0 - lam_init)).astype(BF16)


def _diff_attn(lam_vecs, gn, q, k, vt, B, sq, sk, sk_valid, tq, tk, causal, lam_init):
    nq = sq // tq
    const = lambda b, h, i: (0, 0)
    return pl.pallas_call(
        functools.partial(_diff_attn_kernel, tq=tq, tk=tk, sk=sk, sk_valid=sk_valid, causal=causal,
                          lam_init=lam_init),
        grid=(B, DIFF_HEADS, nq),
        in_specs=[
            pl.BlockSpec(lam_vecs.shape, const), pl.BlockSpec(gn.shape, const),
            pl.BlockSpec((tq, LANES), lambda b, h, i: (b * nq + i, h)),
            pl.BlockSpec((sk, LANES), lambda b, h, i: (b, h)),
            pl.BlockSpec((1, DIFF_V, sk), lambda b, h, i: (b, h, 0)),
        ],
        out_specs=pl.BlockSpec((tq, LANES), lambda b, h, i: (b * nq + i, h)),
        out_shape=jax.ShapeDtypeStruct((B * sq, DIFF_HEADS * DIFF_V), BF16),
        scratch_shapes=_attn_scratch(2, DIFF_V, tq),
        compiler_params=_cparams(("parallel", "parallel", "arbitrary")),
        name="diff_attn",
    )(lam_vecs, gn, q, k, vt)


def _out_proj_kernel(*refs, n_in):
    a_refs = refs[:n_in]
    (w_ref, x_ref, g_ref, b_ref, wr_ref, bias_ref, tri_ref,
     o_ref, gate_ref, route_ref, cnt_out_ref, gt_ref, oh_ref, cnt_ref) = refs[n_in:]
    y = None
    off = 0
    for a_ref in a_refs:
        width = a_ref.shape[1]
        part = _dot(a_ref[...], w_ref[off:off + width, :])
        y = part if y is None else y + part
        off += width
    x1 = _layer_norm(DN_ALPHA * x_ref[...] + y, g_ref[...], b_ref[...])
    o_ref[...] = x1
    _route(x1, wr_ref, bias_ref, tri_ref, gate_ref, route_ref, cnt_out_ref, gt_ref, oh_ref, cnt_ref)


def _out_proj(acts, w, x, g, b, wr_t, bias, tm):
    T, D = x.shape
    tri = jnp.asarray(np.triu(np.ones((tm, tm), np.float32), 1)).astype(BF16)
    row = lambda i: (i, 0)
    const = lambda i: (0, 0)
    return pl.pallas_call(
        functools.partial(_out_proj_kernel, n_in=len(acts)),
        grid=(T // tm,),
        in_specs=[pl.BlockSpec((tm, a.shape[1]), row) for a in acts] + [
            pl.BlockSpec(w.shape, const), pl.BlockSpec((tm, D), row),
            pl.BlockSpec(g.shape, const), pl.BlockSpec(b.shape, const),
            pl.BlockSpec(wr_t.shape, const), pl.BlockSpec(bias.shape, const), pl.BlockSpec(tri.shape, const),
        ],
        out_specs=[pl.BlockSpec((tm, D), row), pl.BlockSpec((tm, LANES), row),
                   pl.BlockSpec((8, tm), lambda i: (0, i)), pl.BlockSpec((SEG_ROWS, LANES), const)],
        out_shape=[jax.ShapeDtypeStruct((T, D), F32), jax.ShapeDtypeStruct((T, LANES), F32),
                   jax.ShapeDtypeStruct((8, T), jnp.int32), jax.ShapeDtypeStruct((SEG_ROWS, LANES), F32)],
        scratch_shapes=[pltpu.VMEM((LANES, tm), F32), pltpu.VMEM((SEG_ROWS, tm), F32),
                        pltpu.VMEM((SEG_ROWS, LANES), F32)],
        compiler_params=_cparams(("arbitrary",)),
        name="out_proj",
    )(*acts, w, x, g, b, wr_t, bias, tri)


def _odd_in_kernel(x_ref, w_ref, tab_ref, q_ref, kt_ref, kb_ref, vf_ref, vt_ref):
    xb = x_ref[...].astype(BF16)
    W = DIFF_HEADS * 2 * DIFF_HD
    hq = _dot(xb, w_ref[:, 0:W])
    hk = _dot(xb, w_ref[:, W:2 * W])
    for h in range(W // LANES):
        sl = slice(h * LANES, (h + 1) * LANES)
        q_ref[:, sl] = _rope(hq[:, sl], tab_ref[0], tab_ref[1], tab_ref[2], DIFF_HD // 2).astype(BF16)
        kk = _rope(hk[:, sl], tab_ref[3], tab_ref[4], tab_ref[5], DIFF_HD // 2)
        kt_ref[0, sl, :] = kk.T
        kb_ref[:, sl] = kk.astype(BF16)
    hv = _dot(xb, w_ref[:, 2 * W:])
    vf_ref[...] = hv
    vt_ref[0] = hv.T.astype(BF16)


def _odd_in(x, w, tabs, S, tm):
    T, D = x.shape
    W = DIFF_HEADS * 2 * DIFF_HD
    nt = tabs.shape[1] // tm
    row = lambda i: (i, 0)
    blk = pl.BlockSpec((tm, W), row)
    vt_spec, vt_shape = _vt_layout(T, S, tm, DIFF_HEADS * DIFF_V)
    kt_spec, kt_shape = _vt_layout(T, S, tm, W, F32)
    return pl.pallas_call(
        _odd_in_kernel,
        grid=(T // tm,),
        in_specs=[pl.BlockSpec((tm, D), row), pl.BlockSpec(w.shape, lambda i: (0, 0)),
                  pl.BlockSpec((tabs.shape[0], tm, LANES), lambda i: (0, i % nt, 0))],
        out_specs=[blk, kt_spec, blk, blk, vt_spec],
        out_shape=[jax.ShapeDtypeStruct((T, W), BF16), kt_shape,
                   jax.ShapeDtypeStruct((T, W), BF16), jax.ShapeDtypeStruct((T, W), F32), vt_shape],
        compiler_params=_cparams(("parallel",)),
        name="odd_in",
    )(x, w, tabs)


def _route(x, wr_ref, bias_ref, tri_ref, g_ref, route_ref, cnt_out_ref, gt_ref, oh_ref, cnt_ref):
    tm = x.shape[0]
    logits = _dot_nt(wr_ref[...], x.astype(BF16))
    sc = jax.nn.sigmoid(logits)
    sel = sc + bias_ref[...]
    r = [sel[e:e + 1, :] for e in range(N_EXPERTS)]
    s = [sc[e:e + 1, :] for e in range(N_EXPERTS)]
    grp = []
    for g in range(N_GROUPS):
        a, b, c, d = r[4 * g:4 * g + 4]
        top2 = jnp.maximum(jnp.maximum(jnp.maximum(a + b, a + c), jnp.maximum(a + d, b + c)),
                           jnp.maximum(b + d, c + d))
        grp.append(top2)
    best = jnp.maximum(jnp.maximum(grp[0], grp[1]), jnp.maximum(grp[2], grp[3]))
    taken = jnp.zeros((1, tm), jnp.bool_)
    chosen = []
    for g in range(N_GROUPS):
        win = jnp.logical_and(grp[g] == best, jnp.logical_not(taken))
        chosen.append(win)
        taken = jnp.logical_or(taken, win)
    picked = []
    for e in range(N_EXPERTS):
        g = e // EXPERTS_PER_GROUP
        rank = jnp.zeros((1, tm), F32)
        for k in range(4 * g, 4 * g + 4):
            if k < e:
                rank = rank + (r[k] >= r[e]).astype(F32)
            elif k > e:
                rank = rank + (r[k] > r[e]).astype(F32)
        picked.append(jnp.logical_and(chosen[g], rank < 2.0))
    w = [jnp.where(picked[e], s[e], 0.0) for e in range(N_EXPERTS)]
    denom = w[0]
    for e in range(1, N_EXPERTS):
        denom = denom + w[e]
    gt_ref[...] = jnp.zeros_like(gt_ref)
    for e in range(N_EXPERTS):
        gt_ref[e:e + 1, :] = w[e] / denom
    g_ref[...] = gt_ref[...].T

    @pl.when(pl.program_id(0) == 0)
    def _():
        cnt_ref[...] = jnp.zeros_like(cnt_ref)

    oh_ref[...] = jnp.zeros_like(oh_ref)
    for g in range(N_GROUPS):
        for p, (a, b) in enumerate(EXPERT_PAIRS):
            both = jnp.logical_and(picked[4 * g + a], picked[4 * g + b])
            oh_ref[g * N_PAIRS + p:g * N_PAIRS + p + 1, :] = both.astype(F32)
    oh = oh_ref[...]
    before = _dot(oh.astype(BF16), tri_ref[...])
    base = cnt_ref[:, 0:1]
    rank = jnp.sum(oh * (base + before), 0, keepdims=True)
    sid = jnp.sum(oh * lax.broadcasted_iota(jnp.int32, oh.shape, 0).astype(F32), 0, keepdims=True)
    row = lax.broadcasted_iota(jnp.int32, route_ref.shape, 0)
    route_ref[...] = jnp.where(row == 0, sid, jnp.where(row == 1, rank, 0.0)).astype(jnp.int32)
    cnt_ref[...] = cnt_ref[...] + jnp.sum(oh, 1, keepdims=True)
    cnt_out_ref[...] = cnt_ref[...]


def _moe_dense_kernel(x_ref, gate_ref, wg_ref, wu_ref, wd_ref, g_ref, b_ref, o_ref, xb_ref, acc_ref):
    e = pl.program_id(1)

    @pl.when(e == 0)
    def _():
        xb_ref[...] = x_ref[...].astype(BF16)
        acc_ref[...] = jnp.zeros_like(acc_ref)

    xb = xb_ref[...]
    h = jax.nn.silu(_dot(xb, wg_ref[0, 0].astype(BF16))) * _dot(xb, wu_ref[0, 0].astype(BF16))
    y = _dot(h.astype(BF16), wd_ref[0, 0].astype(BF16))
    gates = gate_ref[...]
    lane = lax.broadcasted_iota(jnp.int32, gates.shape, 1)
    gcol = jnp.sum(jnp.where(lane == e, gates, 0.0), -1, keepdims=True)
    acc_ref[...] += gcol * y

    @pl.when(e == pl.num_programs(1) - 1)
    def _():
        o_ref[...] = _layer_norm(DN_ALPHA * x_ref[...] + acc_ref[...], g_ref[...], b_ref[...])


def _moe_dense(x, gates, wg, wu, wd, l, g, b, tm):
    T, D = x.shape
    _, E, _, H = wg.shape
    row = lambda i, e: (i, 0)
    const = lambda i, e: (0, 0)
    wsel = lambda i, e: (l, e, 0, 0)
    return pl.pallas_call(
        _moe_dense_kernel,
        grid=(T // tm, E),
        in_specs=[
            pl.BlockSpec((tm, D), row), pl.BlockSpec((tm, LANES), row),
            pl.BlockSpec((1, 1, D, H), wsel), pl.BlockSpec((1, 1, D, H), wsel),
            pl.BlockSpec((1, 1, H, D), wsel),
            pl.BlockSpec(g.shape, const), pl.BlockSpec(b.shape, const),
        ],
        out_specs=pl.BlockSpec((tm, D), row),
        out_shape=jax.ShapeDtypeStruct((T, D), F32),
        scratch_shapes=[pltpu.VMEM((tm, D), BF16), pltpu.VMEM((tm, D), F32)],
        compiler_params=_cparams(("parallel", "arbitrary")),
        name="moe_dense",
    )(x, gates, wg, wu, wd, g, b)


SC_CORES = 2
SC_SUBCORES = 16
SC_WORKERS = SC_CORES * SC_SUBCORES
SC_TILE_BYTES = 384 * 1024
SPARSE_ROW_TILE = 512
SPARSE_MIN_TOKENS = 4096


def _sc_ring(per_w, row_bytes):
    for ch, nbuf in ((16, 4), (16, 2), (8, 2)):
        if per_w % (ch * nbuf) == 0 and ch * nbuf * row_bytes <= SC_TILE_BYTES:
            return ch, nbuf
    raise ValueError(f"no SparseCore gather ring for {per_w} rows of {row_bytes} bytes per subcore")


def _sc_gather(table, idx):
    R = idx.shape[0]
    D = table.shape[1]
    per_w = R // SC_WORKERS
    assert per_w * SC_WORKERS == R and per_w % 8 == 0
    ch, nbuf = _sc_ring(per_w, D * table.dtype.itemsize)
    nchunk = per_w // ch
    mesh = plsc.VectorSubcoreMesh(core_axis_name="c", subcore_axis_name="s")

    @functools.partial(
        pl.kernel, mesh=mesh,
        out_type=jax.ShapeDtypeStruct((R, D), table.dtype),
        scratch_types=[pltpu.VMEM((per_w,), jnp.int32), pltpu.VMEM((nbuf, ch, D), table.dtype),
                       pltpu.SemaphoreType.DMA((nbuf,)), pltpu.SemaphoreType.DMA((nbuf,))],
    )
    def gather_kernel(table_hbm, idx_hbm, out_hbm, idx_v, rows_v, gsem, wsem):
        base = (lax.axis_index("s") * SC_CORES + lax.axis_index("c")) * per_w
        pltpu.sync_copy(idx_hbm.at[pl.ds(base, per_w)], idx_v)

        def gather(c, b):
            return pltpu.make_async_copy(table_hbm.at[idx_v.at[pl.ds(c * ch, ch)]], rows_v.at[b], gsem.at[b])

        def write(c, b):
            return pltpu.make_async_copy(rows_v.at[b], out_hbm.at[pl.ds(base + c * ch, ch)], wsem.at[b])

        for b in range(nbuf - 1):
            gather(b, b).start()

        @pl.loop(0, nchunk, step=nbuf)
        def _(c):
            for b in range(nbuf):
                cc = c + b
                gather(cc, b).wait()
                write(cc, b).start()
                pb = (b - 1) % nbuf

                @pl.when(cc + nbuf - 1 < nchunk)
                def _():
                    @pl.when(cc >= 1)
                    def _():
                        write(cc - 1, pb).wait()

                    gather(cc + nbuf - 1, pb).start()

        for b in range(nbuf):
            write(nchunk - nbuf + b, b).wait()

    return gather_kernel(table, idx)


def _moe_group_kernel(te_ref, nv_ref, x_ref, gate_ref, *refs):
    w_refs, (g_ref, b_ref, o_ref) = refs[:3 * TOP_K], refs[3 * TOP_K:]
    j = pl.program_id(0)
    live = j < nv_ref[0]

    @pl.when(live)
    def _():
        x = x_ref[...]
        xb = x.astype(BF16)
        gates = gate_ref[...]
        lane = lax.broadcasted_iota(jnp.int32, gates.shape, 1)
        acc = None
        for k in range(TOP_K):
            wg_ref, wu_ref, wd_ref = w_refs[3 * k:3 * k + 3]
            h = jax.nn.silu(_dot(xb, wg_ref[0, 0].astype(BF16))) * _dot(xb, wu_ref[0, 0].astype(BF16))
            y = _dot(h.astype(BF16), wd_ref[0, 0].astype(BF16))
            gcol = jnp.sum(jnp.where(lane == te_ref[TOP_K * j + k], gates, 0.0), -1, keepdims=True)
            acc = gcol * y if acc is None else acc + gcol * y
        o_ref[...] = _layer_norm(DN_ALPHA * x + acc, g_ref[...], b_ref[...])

    @pl.when(jnp.logical_not(live))
    def _():
        o_ref[...] = jnp.zeros_like(o_ref)


def _moe_group(tile_experts, n_valid, xs, gs, wg, wu, wd, l, g, b, tm):
    R, D = xs.shape
    H = wg.shape[3]
    row = lambda j, te, nv: (j, 0)
    const = lambda j, te, nv: (0, 0)
    w_specs, w_args = [], []
    for k in range(TOP_K):
        wsel = lambda j, te, nv, k=k: (l, te[TOP_K * j + k], 0, 0)
        w_specs += [pl.BlockSpec((1, 1, D, H), wsel), pl.BlockSpec((1, 1, D, H), wsel),
                    pl.BlockSpec((1, 1, H, D), wsel)]
        w_args += [wg, wu, wd]
    grid_spec = pltpu.PrefetchScalarGridSpec(
        num_scalar_prefetch=2,
        grid=(R // tm,),
        in_specs=[pl.BlockSpec((tm, D), row), pl.BlockSpec((tm, LANES), row)] + w_specs + [
            pl.BlockSpec(g.shape, const), pl.BlockSpec(b.shape, const)],
        out_specs=pl.BlockSpec((tm, D), row),
    )
    return pl.pallas_call(
        _moe_group_kernel,
        grid_spec=grid_spec,
        out_shape=jax.ShapeDtypeStruct((R, D), F32),
        compiler_params=_cparams(("arbitrary",)),
        name="moe_group",
    )(tile_experts, n_valid, xs, gs, *w_args, g, b)


def _sc_scatter(x, dest, pad_dest, n_rows):
    T, D = x.shape
    P = pad_dest.shape[0]
    per_w, pad_w = T // SC_WORKERS, P // SC_WORKERS
    assert per_w * SC_WORKERS == T and pad_w * SC_WORKERS == P and T + P == n_rows
    ch, nbuf = _sc_ring(per_w, D * x.dtype.itemsize)
    assert pad_w % ch == 0
    nchunk, npad = per_w // ch, pad_w // ch
    mesh = plsc.VectorSubcoreMesh(core_axis_name="c", subcore_axis_name="s")

    @functools.partial(
        pl.kernel, mesh=mesh,
        out_type=jax.ShapeDtypeStruct((n_rows, D), x.dtype),
        scratch_types=[pltpu.VMEM((nchunk, ch), jnp.int32), pltpu.VMEM((npad, ch), jnp.int32),
                       pltpu.VMEM((nbuf, ch, D), x.dtype), pltpu.VMEM((ch, D), x.dtype),
                       pltpu.SemaphoreType.DMA((nbuf,)), pltpu.SemaphoreType.DMA((nbuf,)),
                       pltpu.SemaphoreType.DMA],
    )
    def scatter_kernel(x_hbm, dest_hbm, pad_hbm, zero_hbm, out_hbm, dest_v, pad_v, rows_v, zero_v,
                       rsem, wsem, zsem):
        wid = lax.axis_index("s") * SC_CORES + lax.axis_index("c")
        base = wid * per_w
        pltpu.sync_copy(dest_hbm.at[wid], dest_v)
        pltpu.sync_copy(pad_hbm.at[wid], pad_v)
        pltpu.sync_copy(zero_hbm, zero_v)

        def read(c, b):
            return pltpu.make_async_copy(x_hbm.at[pl.ds(base + c * ch, ch)], rows_v.at[b], rsem.at[b])

        def write(c, b):
            return pltpu.make_async_copy(rows_v.at[b], out_hbm.at[dest_v.at[c]], wsem.at[b])

        def write_zero(c):
            return pltpu.make_async_copy(zero_v, out_hbm.at[pad_v.at[c]], zsem)

        for c in range(npad):
            write_zero(c).start()
        for b in range(nbuf - 1):
            read(b, b).start()

        @pl.loop(0, nchunk, step=nbuf)
        def _(c):
            for b in range(nbuf):
                cc = c + b
                read(cc, b).wait()
                write(cc, b).start()
                pb = (b - 1) % nbuf

                @pl.when(cc + nbuf - 1 < nchunk)
                def _():
                    @pl.when(cc >= 1)
                    def _():
                        write(cc - 1, pb).wait()

                    read(cc + nbuf - 1, pb).start()

        for b in range(nbuf):
            write(nchunk - nbuf + b, b).wait()
        for c in range(npad):
            write_zero(c).wait()

    return scatter_kernel(x, dest.reshape(SC_WORKERS, nchunk, ch), pad_dest.reshape(SC_WORKERS, npad, ch),
                          jnp.zeros((ch, D), x.dtype))


def _moe_sparse(x, gates, route, counts, wg, wu, wd, l, g, b):
    T = x.shape[0]
    tm = SPARSE_ROW_TILE
    n_tiles = T // tm + N_SEG
    n_rows = n_tiles * tm
    sid, rank = route[0], route[1]
    cnt = counts[:N_SEG, 0].astype(jnp.int32)
    tiles = (cnt + tm - 1) // tm
    ends = jnp.cumsum(tiles)
    starts = (ends - tiles) * tm
    seg_ids = jnp.arange(N_SEG, dtype=jnp.int32)
    pick = lambda which, vals: jnp.sum(jnp.where(which[:, None] == seg_ids[None, :], vals[None, :], 0), 1)
    passed = lambda pos, bounds: jnp.sum((pos[:, None] >= bounds[None, :]).astype(jnp.int32), 1)
    dest = rank + pick(sid, starts)
    pad_cnt = tiles * tm - cnt
    pad_end = jnp.cumsum(pad_cnt)
    p = jnp.arange(n_rows - T, dtype=jnp.int32)
    seg = passed(p, pad_end)
    pad_dest = jnp.where(seg < N_SEG,
                         pick(seg, starts + cnt) + p - pick(seg, pad_end - pad_cnt),
                         ends[N_SEG - 1] * tm + p - pad_end[N_SEG - 1])
    tile_seg = jnp.minimum(passed(jnp.arange(n_tiles, dtype=jnp.int32), ends), N_SEG - 1)
    pair = jnp.asarray(np.asarray(EXPERT_PAIRS, np.int32))
    tile_experts = (tile_seg // N_PAIRS * EXPERTS_PER_GROUP)[:, None] + pair[tile_seg % N_PAIRS]
    xs = _sc_scatter(x, dest, pad_dest, n_rows)
    gs = _sc_scatter(gates, dest, pad_dest, n_rows)
    ys = _moe_group(tile_experts.reshape(-1), ends[N_SEG - 1:], xs, gs, wg, wu, wd, l, g, b, tm)
    return _sc_gather(ys, dest)


def _prep_weights(w_in_even, w_uq, w_ukv, w_out_even, w_in_odd, w_out_odd, w_router,
                  w_expert_gate, w_expert_up, w_expert_down):
    n_main = 4 * RET_W + MLA_Q_RANK + MLA_KV_RANK
    w_in = w_in_even[0]
    kr_cols = jnp.pad(w_in[:, n_main:], ((0, 0), (MLA_NOPE, LANES - MLA_NOPE - MLA_ROPE)))
    w_a = jnp.concatenate([w_in[:, :n_main], kr_cols], 1).astype(BF16)
    qd = MLA_NOPE + MLA_ROPE
    wq = jnp.pad(w_uq[0].reshape(MLA_Q_RANK, MLA_HEADS, qd), ((0, 0), (0, 0), (0, LANES - qd)))
    wq = wq.reshape(MLA_Q_RANK, MLA_PAD).astype(BF16)
    wkv = w_ukv[0].reshape(MLA_KV_RANK, MLA_HEADS, MLA_NOPE + MLA_V)
    wk = jnp.pad(wkv[:, :, :MLA_NOPE], ((0, 0), (0, 0), (0, LANES - MLA_NOPE)))
    wk = wk.reshape(MLA_KV_RANK, MLA_PAD).astype(BF16)
    wvt = wkv[:, :, MLA_NOPE:].reshape(MLA_KV_RANK, MLA_HEADS * MLA_V).T.astype(BF16)
    e_np = np.zeros((MLA_ROPE, MLA_HEADS, LANES), np.float32)
    for j in range(MLA_ROPE):
        e_np[j, :, MLA_NOPE + j] = 1.0
    e_mat = jnp.asarray(e_np.reshape(MLA_ROPE, MLA_PAD)).astype(BF16)
    return dict(
        w_a=w_a, wq=wq, wk=wk, wvt=wvt, e_mat=e_mat,
        w_out_even=w_out_even[0].astype(BF16), w_in_odd=w_in_odd[0].astype(BF16),
        w_out_odd=w_out_odd[0].astype(BF16), wr_t=w_router.T.astype(BF16),
        wg=w_expert_gate, wu=w_expert_up, wd=w_expert_down,
    )


def _moe(routed, wts, l, ln_g, ln_b, tm):
    x, gates, route, counts = routed
    g, b = ln_g[l, 1][None], ln_b[l, 1][None]
    if x.shape[0] >= SPARSE_MIN_TOKENS:
        return _moe_sparse(x, gates, route, counts, wts["wg"], wts["wu"], wts["wd"], l, g, b)
    return _moe_dense(x, gates, wts["wg"], wts["wu"], wts["wd"], l, g, b, tm)


def _trunk(x3, pos0, past, wts, prm):
    B, S, D = x3.shape
    T = B * S
    x = x3.reshape(T, D)
    tm = min(T, TOKEN_TILE)
    rep = max(tm // S, 1)
    pos = np.tile(pos0 + np.arange(S), rep)
    ln_g, ln_b = prm["ln_g"], prm["ln_b"]

    rq, rk, rv, rg, q, lat, kr = _even_in(x, wts["w_a"], wts["wq"], prm["gq"], prm["gkv"], _even_tables(pos), tm)
    causal = past is None
    if causal:
        state0 = jnp.zeros((B, RET_HEADS, RET_DK, RET_DV), F32)
        lat_all, kr_all, sk, skp = lat, kr, S, S
        tq = tk = min(S, ATTN_TILE)
    else:
        state0 = past["state"]
        sk = past["lat"].shape[1] + S
        skp = -(-sk // LANES) * LANES
        padk = lambda parts: jnp.concatenate(
            parts + [jnp.zeros((B, skp - sk, parts[0].shape[2]), parts[0].dtype)], 1).reshape(B * skp, -1)
        lat_all = padk([past["lat"], lat.reshape(B, S, -1)])
        kr_all = padk([past["kr"], kr.reshape(B, S, -1)])
        tq, tk = S, skp
    ret_out, ret_state = _retention(rq, rk, rv, rg, state0, B, S)
    tkv = KV_UP_TILE if (B * skp) % KV_UP_TILE == 0 else skp
    k_mla, vt_mla = _kv_up(lat_all, kr_all, wts["wk"], wts["wvt"], wts["e_mat"], skp, tkv)
    if vt_mla.shape[0] != B:
        vt_mla = vt_mla.reshape(-1, B, skp).transpose(1, 0, 2)
    mla_out = _mla_attn(q, k_mla, vt_mla, B, S, skp, sk, tq, tk, causal)
    routed = _out_proj([ret_out, mla_out], wts["w_out_even"], x, ln_g[0, 0][None], ln_b[0, 0][None],
                       wts["wr_t"], prm["bias"], tm)
    x = _moe(routed, wts, 0, ln_g, ln_b, tm)

    qd, kt, kb, vf, vt = _odd_in(x, wts["w_in_odd"], _odd_tables(pos), S, tm)
    if vt.shape[0] != B:
        vt = vt.reshape(-1, B, S).transpose(1, 0, 2)
        kt = kt.reshape(-1, B, S).transpose(1, 0, 2)
    kf = kt.reshape(1, B, 2 * DIFF_HEADS, DIFF_HD, S).transpose(0, 1, 4, 2, 3)
    if causal:
        k_all, vt_all = kb, vt
    else:
        k_all = padk([past["dk"], kb.reshape(B, S, -1)])
        vt_all = jnp.concatenate([past["dv"].transpose(0, 2, 1), vt,
                                  jnp.zeros((B, vt.shape[1], skp - sk), BF16)], 2)
    lam_init = 0.8 - 0.6 * math.exp(-0.3 * 1)
    d_out = _diff_attn(prm["lam"], prm["gn"], qd, k_all, vt_all, B, S, skp, sk, tq, tk, causal, lam_init)
    routed = _out_proj([d_out], wts["w_out_odd"], x, ln_g[1, 0][None], ln_b[1, 0][None],
                       wts["wr_t"], prm["bias"], tm)
    x = _moe(routed, wts, 1, ln_g, ln_b, tm)

    return (x.reshape(B, S, D), ret_state[None], lat.reshape(1, B, S, -1), kr.reshape(1, B, S, -1),
            kf, vf.reshape(1, B, S, DIFF_HEADS, DIFF_V))


def kernel(x_prompt, x_sample, state_ret, cache_mla_latent, cache_mla_krope, cache_diff_k, cache_diff_v,
           w_in_even, w_uq, w_ukv, g_qnorm, g_kvnorm, w_out_even,
           w_in_odd, lambda_q1, lambda_k1, lambda_q2, lambda_k2, g_diff_norm, w_out_odd,
           ln_g, ln_b, w_router, router_bias, w_expert_gate, w_expert_up, w_expert_down):
    wts = _prep_weights(w_in_even, w_uq, w_ukv, w_out_even, w_in_odd, w_out_odd, w_router,
                        w_expert_gate, w_expert_up, w_expert_down)
    prm = dict(
        gq=g_qnorm[0][None].astype(F32), gkv=g_kvnorm[0][None].astype(F32),
        lam=jnp.stack([lambda_q1[0], lambda_k1[0], lambda_q2[0], lambda_k2[0]]).astype(F32),
        gn=g_diff_norm[0][None].astype(F32), bias=router_bias.reshape(N_EXPERTS, 1).astype(F32),
        ln_g=ln_g.astype(F32), ln_b=ln_b.astype(F32),
    )
    past_len = cache_mla_latent.shape[2]
    db = x_sample.shape[0]
    past = dict(
        state=state_ret[0].astype(F32), lat=cache_mla_latent[0], kr=cache_mla_krope[0],
        dk=cache_diff_k[0].reshape(db, past_len, -1).astype(BF16),
        dv=cache_diff_v[0].reshape(db, past_len, -1).astype(BF16),
    )
    outs_p = _trunk(x_prompt, 0, None, wts, prm)
    outs_s = _trunk(x_sample, past_len, past, wts, prm)
    return (outs_p[0], outs_s[0]) + outs_p[1:] + outs_s[1:]
```

```python
import functools
import math

import numpy as np
import jax
import jax.numpy as jnp
from jax import lax
from jax.experimental import pallas as pl
from jax.experimental.pallas import tpu as pltpu
from jax.experimental.pallas import tpu_sc as plsc

F32 = jnp.float32
BF16 = jnp.bfloat16

CHUNK = 64
ROPE_THETA = 10000.0
NEG_INF = -1e30
LN_EPS = 1e-5
NORM_EPS = 1e-6
DEPTH = 2
DN_ALPHA = (2.0 * DEPTH) ** 0.25
RET_HEADS = 4
RET_DK = 128
RET_DV = 128
RET_LOG_GAMMA = tuple(math.log(1.0 - 2.0 ** (-5 - h)) for h in range(RET_HEADS))
MLA_HEADS = 8
MLA_Q_RANK = 384
MLA_KV_RANK = 256
MLA_NOPE = 64
MLA_ROPE = 32
MLA_V = 64
DIFF_HEADS = 8
DIFF_HD = 64
DIFF_V = 128
N_EXPERTS = 16
N_GROUPS = 4
EXPERTS_PER_GROUP = 4
TOP_K = 2
EXPERT_PAIRS = tuple((a, b) for a in range(EXPERTS_PER_GROUP) for b in range(a + 1, EXPERTS_PER_GROUP))
N_PAIRS = len(EXPERT_PAIRS)
N_SEG = N_GROUPS * N_PAIRS
SEG_ROWS = 32
LOG2E = math.log2(math.e)

LANES = 128
RET_W = RET_HEADS * RET_DK
MLA_PAD = MLA_HEADS * LANES
VMEM_LIMIT = 56 * 1024 * 1024
TOKEN_TILE = 1024
KV_UP_TILE = 512
RET_CHUNK = 256
RET_TILE = 512
ATTN_TILE = 1024


def _cparams(sem):
    return pltpu.CompilerParams(dimension_semantics=sem, vmem_limit_bytes=VMEM_LIMIT)


def _rope_tables(pos, d, group, offset, scale):
    pos = np.asarray(pos, np.float64)
    half = d // 2
    inv = 1.0 / (ROPE_THETA ** (np.arange(0, d, 2, dtype=np.float64) / d))
    ang = pos[:, None] * inv[None, :]
    cos = np.full((pos.shape[0], LANES), scale, np.float64)
    s_lo = np.zeros((pos.shape[0], LANES), np.float64)
    s_hi = np.zeros((pos.shape[0], LANES), np.float64)
    start = offset
    while start + d <= LANES:
        cos[:, start:start + half] = np.cos(ang) * scale
        cos[:, start + half:start + d] = np.cos(ang) * scale
        s_lo[:, start:start + half] = -np.sin(ang) * scale
        s_hi[:, start + half:start + d] = np.sin(ang) * scale
        start += group
    return cos, s_lo, s_hi


def _even_tables(pos):
    rq = _rope_tables(pos, RET_DK, LANES, 0, 1.0)
    rk = _rope_tables(pos, RET_DK, LANES, 0, RET_DK ** -0.5)
    c = (MLA_NOPE + MLA_ROPE) ** -0.5 * LOG2E
    mq = _rope_tables(pos, MLA_ROPE, LANES, MLA_NOPE, c)
    mk = _rope_tables(pos, MLA_ROPE, LANES, MLA_NOPE, 1.0)
    tabs = [rq[0], rq[1] + rq[2], rk[0], rk[1] + rk[2], mq[0], mq[1], mq[2], mk[0], mk[1], mk[2]]
    return jnp.asarray(np.stack(tabs).astype(np.float32))


def _odd_tables(pos):
    c = DIFF_HD ** -0.5 * LOG2E
    dq = _rope_tables(pos, DIFF_HD, DIFF_HD, 0, c)
    dk = _rope_tables(pos, DIFF_HD, DIFF_HD, 0, 1.0)
    return jnp.asarray(np.stack(list(dq) + list(dk)).astype(np.float32))


def _retention_tables(L):
    lg = np.asarray(RET_LOG_GAMMA, np.float64)
    idx = np.arange(L, dtype=np.float64)
    diff = idx[:, None] - idx[None, :]
    dmask = np.where(diff[None] >= 0, np.exp(np.maximum(diff, 0.0)[None] * lg[:, None, None]), 0.0)
    qd = np.exp((idx[None, :] + 1.0) * lg[:, None])
    kd = np.exp((L - 1.0 - idx)[None, :] * lg[:, None])
    gl = np.exp(L * lg)
    qd = np.broadcast_to(qd[:, :, None], (RET_HEADS, L, LANES))
    kd = np.broadcast_to(kd[:, :, None], (RET_HEADS, L, LANES))
    gl = np.broadcast_to(gl[:, None, None], (RET_HEADS, RET_DK, RET_DV))
    f = lambda a: jnp.asarray(np.ascontiguousarray(a).astype(np.float32))
    return f(dmask), f(qd), f(kd), f(gl)


def _dot(a, b):
    return jnp.dot(a, b, preferred_element_type=F32)


def _dot_nt(a, b):
    return lax.dot_general(a, b, (((1,), (1,)), ((), ())), preferred_element_type=F32)


def _dot_tn(a, b):
    return lax.dot_general(a, b, (((0,), (0,)), ((), ())), preferred_element_type=F32)


def _rope(x, cos, s_lo, s_hi, half):
    return x * cos + pltpu.roll(x, LANES - half, 1) * s_lo + pltpu.roll(x, half, 1) * s_hi


def _layer_norm(x, g, b):
    mu = jnp.mean(x, -1, keepdims=True)
    xc = x - mu
    var = jnp.mean(xc * xc, -1, keepdims=True)
    return xc * lax.rsqrt(var + LN_EPS) * g + b


def _rms_norm(x, g):
    return x * lax.rsqrt(jnp.mean(x * x, -1, keepdims=True) + NORM_EPS) * g


def _even_in_kernel(x_ref, w_ref, wq_ref, gq_ref, gkv_ref, tab_ref, *refs, fuse_kv):
    if fuse_kv:
        wk_ref, wvt_ref, rq_ref, rk_ref, rv_ref, rg_ref, q_ref, lat_ref, kr_ref, k_ref, vt_ref = refs
    else:
        rq_ref, rk_ref, rv_ref, rg_ref, q_ref, lat_ref, kr_ref = refs
    xb = x_ref[...].astype(BF16)
    c_rq, s_rq, c_rk, s_rk = tab_ref[0], tab_ref[1], tab_ref[2], tab_ref[3]
    hq = _dot(xb, w_ref[:, 0:RET_W])
    hk = _dot(xb, w_ref[:, RET_W:2 * RET_W])
    for h in range(RET_HEADS):
        sl = slice(h * LANES, (h + 1) * LANES)
        xq = hq[:, sl]
        rq_ref[:, sl] = (xq * c_rq + pltpu.roll(xq, RET_DK // 2, 1) * s_rq).astype(BF16)
        xk = hk[:, sl]
        rk_ref[:, sl] = (xk * c_rk + pltpu.roll(xk, RET_DK // 2, 1) * s_rk).astype(BF16)
    rv_ref[...] = _dot(xb, w_ref[:, 2 * RET_W:3 * RET_W]).astype(BF16)
    rg_ref[...] = _dot(xb, w_ref[:, 3 * RET_W:4 * RET_W]).astype(BF16)
    o = 4 * RET_W
    cq = _dot(xb, w_ref[:, o:o + MLA_Q_RANK])
    qn = _rms_norm(cq, gq_ref[...]).astype(BF16)
    qf = _dot(qn, wq_ref[...])
    c_q, lo_q, hi_q = tab_ref[4], tab_ref[5], tab_ref[6]
    for h in range(MLA_HEADS):
        sl = slice(h * LANES, (h + 1) * LANES)
        q_ref[:, sl] = _rope(qf[:, sl], c_q, lo_q, hi_q, MLA_ROPE // 2).astype(BF16)
    o += MLA_Q_RANK
    ckv = _dot(xb, w_ref[:, o:o + MLA_KV_RANK])
    lat = _rms_norm(ckv, gkv_ref[...])
    lat_ref[...] = lat
    o += MLA_KV_RANK
    krp = _dot(xb, w_ref[:, o:o + LANES])
    krp = _rope(krp, tab_ref[7], tab_ref[8], tab_ref[9], MLA_ROPE // 2)
    kr_ref[...] = krp[:, MLA_NOPE:MLA_NOPE + MLA_ROPE]
    if fuse_kv:
        lb = lat.astype(BF16)
        kn = _dot(lb, wk_ref[...])
        for h in range(MLA_HEADS):
            sl = slice(h * LANES, (h + 1) * LANES)
            k_ref[:, sl] = (kn[:, sl] + krp).astype(BF16)
        vt_ref[0] = _dot_nt(wvt_ref[...], lb).astype(BF16)


def _even_in(x, w_a, wq, gq, gkv, tabs, tm, kv=None):
    T, D = x.shape
    P = tabs.shape[1]
    nt = P // tm
    row = lambda i: (i, 0)
    const = lambda i: (0, 0)
    outs = [
        jax.ShapeDtypeStruct((T, RET_W), BF16), jax.ShapeDtypeStruct((T, RET_W), BF16),
        jax.ShapeDtypeStruct((T, RET_W), BF16), jax.ShapeDtypeStruct((T, RET_W), BF16),
        jax.ShapeDtypeStruct((T, MLA_PAD), BF16),
        jax.ShapeDtypeStruct((T, MLA_KV_RANK), F32), jax.ShapeDtypeStruct((T, MLA_ROPE), F32),
    ]
    in_specs = [
        pl.BlockSpec((tm, D), row),
        pl.BlockSpec(w_a.shape, const),
        pl.BlockSpec(wq.shape, const),
        pl.BlockSpec(gq.shape, const),
        pl.BlockSpec(gkv.shape, const),
        pl.BlockSpec((tabs.shape[0], tm, LANES), lambda i: (0, i % nt, 0)),
    ]
    out_specs = [
        pl.BlockSpec((tm, RET_W), row), pl.BlockSpec((tm, RET_W), row),
        pl.BlockSpec((tm, RET_W), row), pl.BlockSpec((tm, RET_W), row),
        pl.BlockSpec((tm, MLA_PAD), row),
        pl.BlockSpec((tm, MLA_KV_RANK), row), pl.BlockSpec((tm, MLA_ROPE), row),
    ]
    args = [x, w_a, wq, gq, gkv, tabs]
    if kv is not None:
        wk, wvt, S = kv
        vt_spec, vt_shape = _vt_layout(T, S, tm, MLA_HEADS * MLA_V)
        in_specs += [pl.BlockSpec(wk.shape, const), pl.BlockSpec(wvt.shape, const)]
        out_specs += [pl.BlockSpec((tm, MLA_PAD), row), vt_spec]
        outs += [jax.ShapeDtypeStruct((T, MLA_PAD), BF16), vt_shape]
        args += [wk, wvt]
    return pl.pallas_call(
        functools.partial(_even_in_kernel, fuse_kv=kv is not None),
        grid=(T // tm,),
        in_specs=in_specs,
        out_specs=out_specs,
        out_shape=outs,
        compiler_params=_cparams(("parallel",)),
        name="even_in",
    )(*args)


def _kv_up_kernel(lat_ref, kr_ref, wk_ref, wvt_ref, e_ref, k_ref, vt_ref):
    lb = lat_ref[...].astype(BF16)
    krb = kr_ref[...].astype(BF16)
    k_ref[...] = (_dot(lb, wk_ref[...]) + _dot(krb, e_ref[...])).astype(BF16)
    vt_ref[0] = _dot_nt(wvt_ref[...], lb).astype(BF16)


def _vt_layout(T, S, tm, width, dtype=BF16):
    nb, cols = (T // S, S) if S % tm == 0 else (1, T)
    nt = cols // tm
    spec = pl.BlockSpec((1, width, tm), lambda i: (i // nt, 0, i % nt))
    return spec, jax.ShapeDtypeStruct((nb, width, cols), dtype)


def _kv_up(lat, kr, wk, wvt, e_mat, S, tm):
    T = lat.shape[0]
    row = lambda i: (i, 0)
    const = lambda i: (0, 0)
    vt_spec, vt_shape = _vt_layout(T, S, tm, MLA_HEADS * MLA_V)
    return pl.pallas_call(
        _kv_up_kernel,
        grid=(T // tm,),
        in_specs=[
            pl.BlockSpec((tm, MLA_KV_RANK), row), pl.BlockSpec((tm, MLA_ROPE), row),
            pl.BlockSpec(wk.shape, const), pl.BlockSpec(wvt.shape, const), pl.BlockSpec(e_mat.shape, const),
        ],
        out_specs=[pl.BlockSpec((tm, MLA_PAD), row), vt_spec],
        out_shape=[jax.ShapeDtypeStruct((T, MLA_PAD), BF16), vt_shape],
        compiler_params=_cparams(("parallel",)),
        name="kv_up",
    )(lat, kr, wk, wvt, e_mat)


def _retention_kernel(q_ref, k_ref, v_ref, g_ref, s0_ref, dm_ref, qd_ref, kd_ref, gl_ref,
                      o_ref, st_ref, *, L, nchunk):
    @pl.when(pl.program_id(1) == 0)
    def _():
        st_ref[...] = s0_ref[...]

    for c in range(nchunk):
        rows = slice(c * L, (c + 1) * L)
        for h in range(RET_HEADS):
            sl = slice(h * LANES, (h + 1) * LANES)
            q = q_ref[rows, sl]
            k = k_ref[rows, sl]
            v = v_ref[rows, sl]
            st = st_ref[0, h]
            a = (_dot_nt(q, k) * dm_ref[h]).astype(BF16)
            o = _dot(a, v) + _dot(q, st.astype(BF16)) * qd_ref[h]
            kdec = (k.astype(F32) * kd_ref[h]).astype(BF16)
            st_ref[0, h] = st * gl_ref[h] + _dot_tn(kdec, v)
            mu = jnp.mean(o, -1, keepdims=True)
            oc = o - mu
            var = jnp.mean(oc * oc, -1, keepdims=True)
            on = oc * lax.rsqrt(var + LN_EPS)
            g = g_ref[rows, sl].astype(F32)
            o_ref[rows, sl] = (g * jax.nn.sigmoid(g) * on).astype(BF16)


def _retention(rq, rk, rv, rg, state0, B, S):
    L = min(S, RET_CHUNK)
    lt = min(S, RET_TILE)
    nj = S // lt
    dm, qd, kd, gl = _retention_tables(L)
    row = lambda b, j: (b * nj + j, 0)
    c3 = lambda b, j: (0, 0, 0)
    st_spec = pl.BlockSpec((1, RET_HEADS, RET_DK, RET_DV), lambda b, j: (b, 0, 0, 0))
    return pl.pallas_call(
        functools.partial(_retention_kernel, L=L, nchunk=lt // L),
        grid=(B, nj),
        in_specs=[pl.BlockSpec((lt, RET_W), row)] * 4 + [
            st_spec,
            pl.BlockSpec(dm.shape, c3), pl.BlockSpec(qd.shape, c3),
            pl.BlockSpec(kd.shape, c3), pl.BlockSpec(gl.shape, c3),
        ],
        out_specs=[pl.BlockSpec((lt, RET_W), row), st_spec],
        out_shape=[jax.ShapeDtypeStruct((B * S, RET_W), BF16),
                   jax.ShapeDtypeStruct((B, RET_HEADS, RET_DK, RET_DV), F32)],
        compiler_params=_cparams(("parallel", "arbitrary")),
        name="retention",
    )(rq, rk, rv, rg, state0, dm, qd, kd, gl)


def _query_t(q, tq):
    q = q.astype(F32)
    if tq < LANES:
        q = jnp.concatenate([q, jnp.zeros((LANES - tq, LANES), F32)], 0)
    return q.T


def _flash_t(streams, qi, tq, tk, sk, sk_valid, causal, m_ref, l_ref, acc_ref):
    for s in range(len(streams)):
        m_ref[s] = jnp.full(m_ref.shape[1:], NEG_INF, F32)
        l_ref[s] = jnp.zeros(l_ref.shape[1:], F32)
        acc_ref[s] = jnp.zeros(acc_ref.shape[1:], F32)

    def step(start, size, rel=None, valid=None):
        for s, (q_t, k_at, vt_at) in enumerate(streams):
            st = _dot(k_at(start, size), q_t)
            if rel is not None:
                kc = (lax.broadcasted_iota(jnp.int32, st.shape, 0) + rel) // CHUNK
                qc = lax.broadcasted_iota(jnp.int32, st.shape, 1) // CHUNK
                st = jnp.where(kc <= qc, st, NEG_INF)
            if valid is not None:
                st = jnp.where(lax.broadcasted_iota(jnp.int32, st.shape, 0) < valid, st, NEG_INF)
            m_old = m_ref[s]
            m_new = jnp.maximum(m_old, jnp.max(st, 0, keepdims=True))
            p = jnp.exp2(st - m_new)
            alpha = jnp.exp2(m_old - m_new)
            l_ref[s] = alpha * l_ref[s] + jnp.sum(p, 0, keepdims=True)
            acc_ref[s] = acc_ref[s] * alpha + _dot(vt_at(start, size), p.astype(BF16))
            m_ref[s] = m_new

    if causal:
        n_full = qi * (tq // tk)

        def body(j, c):
            step(pl.multiple_of(2 * j * tk, tk), tk)
            step(pl.multiple_of((2 * j + 1) * tk, tk), tk)
            return c

        lax.fori_loop(0, n_full // 2, body, 0)

        if (tq // tk) % 2 == 1:
            @pl.when(n_full % 2 == 1)
            def _():
                step(pl.multiple_of((n_full - 1) * tk, tk), tk)

        for d in range(tq // tk):
            step(pl.multiple_of(qi * tq + d * tk, tk), tk, rel=d * tk)
    else:
        for j in range(sk // tk):
            last_valid = sk_valid - j * tk
            step(j * tk, tk, valid=last_valid if last_valid < tk else None)
    return [acc_ref[s] / l_ref[s] for s in range(len(streams))]


def _attn_scratch(n_streams, dv, tq):
    tqp = max(tq, LANES)
    return [pltpu.VMEM((n_streams, 1, tqp), F32), pltpu.VMEM((n_streams, 1, tqp), F32),
            pltpu.VMEM((n_streams, dv, tqp), F32)]


def _mla_attn_kernel(q_ref, k_ref, vt_ref, o_ref, m_ref, l_ref, acc_ref, *, tq, tk, sk, sk_valid, causal):
    streams = []
    for hh in range(2):
        sl = slice(hh * LANES, (hh + 1) * LANES)
        vrows = slice(hh * MLA_V, (hh + 1) * MLA_V)
        k_at = lambda start, n, sl=sl: k_ref[pl.ds(start, n), sl]
        vt_at = lambda start, n, vrows=vrows: vt_ref[0, vrows, pl.ds(start, n)]
        streams.append((_query_t(q_ref[:, sl], tq).astype(BF16), k_at, vt_at))
    outs = _flash_t(streams, pl.program_id(2), tq, tk, sk, sk_valid, causal, m_ref, l_ref, acc_ref)
    o_ref[...] = jnp.concatenate(outs, 0).T[:tq].astype(BF16)


def _mla_attn(q, k, vt, B, sq, sk, sk_valid, tq, tk, causal):
    nq = sq // tq
    npair = MLA_HEADS // 2
    return pl.pallas_call(
        functools.partial(_mla_attn_kernel, tq=tq, tk=tk, sk=sk, sk_valid=sk_valid, causal=causal),
        grid=(B, npair, nq),
        in_specs=[
            pl.BlockSpec((tq, 2 * LANES), lambda b, p, i: (b * nq + i, p)),
            pl.BlockSpec((sk, 2 * LANES), lambda b, p, i: (b, p)),
            pl.BlockSpec((1, 2 * MLA_V, sk), lambda b, p, i: (b, p, 0)),
        ],
        out_specs=pl.BlockSpec((tq, LANES), lambda b, p, i: (b * nq + i, p)),
        out_shape=jax.ShapeDtypeStruct((B * sq, MLA_HEADS * MLA_V), BF16),
        scratch_shapes=_attn_scratch(2, MLA_V, tq),
        compiler_params=_cparams(("parallel", "parallel", "arbitrary")),
        name="mla_attn",
    )(q, k, vt)


def _diff_attn_kernel(lam_ref, gn_ref, q_ref, k_ref, vt_ref, o_ref, m_ref, l_ref, acc_ref,
                      *, tq, tk, sk, sk_valid, causal, lam_init):
    k_at = lambda start, n: k_ref[pl.ds(start, n), :]
    vt_at = lambda start, n: vt_ref[0, :, pl.ds(start, n)]
    q_t = _query_t(q_ref[...], tq)
    feat = lax.broadcasted_iota(jnp.int32, q_t.shape, 0)
    zero = jnp.zeros_like(q_t)
    q1 = jnp.where(feat < DIFF_HD, q_t, zero).astype(BF16)
    q2 = jnp.where(feat < DIFF_HD, zero, q_t).astype(BF16)
    o1, o2 = _flash_t([(q1, k_at, vt_at), (q2, k_at, vt_at)], pl.program_id(2), tq, tk, sk, sk_valid,
                      causal, m_ref, l_ref, acc_ref)
    lv = lam_ref[...]
    lam = (jnp.exp(jnp.sum(lv[0:1] * lv[1:2], -1, keepdims=True))
           - jnp.exp(jnp.sum(lv[2:3] * lv[3:4], -1, keepdims=True)) + lam_init)
    o = (o1 - lam * o2).T[:tq]
    o_ref[...] = (_rms_norm(o, gn_ref[...]) * (1.0 - lam_init)).astype(BF16)


def _diff_attn(lam_vecs, gn, q, k, vt, B, sq, sk, sk_valid, tq, tk, causal, lam_init):
    nq = sq // tq
    const = lambda b, h, i: (0, 0)
    return pl.pallas_call(
        functools.partial(_diff_attn_kernel, tq=tq, tk=tk, sk=sk, sk_valid=sk_valid, causal=causal,
                          lam_init=lam_init),
        grid=(B, DIFF_HEADS, nq),
        in_specs=[
            pl.BlockSpec(lam_vecs.shape, const), pl.BlockSpec(gn.shape, const),
            pl.BlockSpec((tq, LANES), lambda b, h, i: (b * nq + i, h)),
            pl.BlockSpec((sk, LANES), lambda b, h, i: (b, h)),
            pl.BlockSpec((1, DIFF_V, sk), lambda b, h, i: (b, h, 0)),
        ],
        out_specs=pl.BlockSpec((tq, LANES), lambda b, h, i: (b * nq + i, h)),
        out_shape=jax.ShapeDtypeStruct((B * sq, DIFF_HEADS * DIFF_V), BF16),
        scratch_shapes=_attn_scratch(2, DIFF_V, tq),
        compiler_params=_cparams(("parallel", "parallel", "arbitrary")),
        name="diff_attn",
    )(lam_vecs, gn, q, k, vt)


def _out_proj_kernel(*refs, n_in):
    a_refs = refs[:n_in]
    (w_ref, x_ref, g_ref, b_ref, wr_ref, bias_ref, tri_ref,
     o_ref, gate_ref, route_ref, cnt_out_ref, gt_ref, oh_ref, cnt_ref) = refs[n_in:]
    y = None
    off = 0
    for a_ref in a_refs:
        width = a_ref.shape[1]
        part = _dot(a_ref[...], w_ref[off:off + width, :])
        y = part if y is None else y + part
        off += width
    x1 = _layer_norm(DN_ALPHA * x_ref[...] + y, g_ref[...], b_ref[...])
    o_ref[...] = x1
    _route(x1, wr_ref, bias_ref, tri_ref, gate_ref, route_ref, cnt_out_ref, gt_ref, oh_ref, cnt_ref)


def _out_proj(acts, w, x, g, b, wr_t, bias, tm):
    T, D = x.shape
    tri = jnp.asarray(np.triu(np.ones((tm, tm), np.float32), 1)).astype(BF16)
    row = lambda i: (i, 0)
    const = lambda i: (0, 0)
    return pl.pallas_call(
        functools.partial(_out_proj_kernel, n_in=len(acts)),
        grid=(T // tm,),
        in_specs=[pl.BlockSpec((tm, a.shape[1]), row) for a in acts] + [
            pl.BlockSpec(w.shape, const), pl.BlockSpec((tm, D), row),
            pl.BlockSpec(g.shape, const), pl.BlockSpec(b.shape, const),
            pl.BlockSpec(wr_t.shape, const), pl.BlockSpec(bias.shape, const), pl.BlockSpec(tri.shape, const),
        ],
        out_specs=[pl.BlockSpec((tm, D), row), pl.BlockSpec((tm, LANES), row),
                   pl.BlockSpec((8, tm), lambda i: (0, i)), pl.BlockSpec((SEG_ROWS, LANES), const)],
        out_shape=[jax.ShapeDtypeStruct((T, D), F32), jax.ShapeDtypeStruct((T, LANES), F32),
                   jax.ShapeDtypeStruct((8, T), jnp.int32), jax.ShapeDtypeStruct((SEG_ROWS, LANES), F32)],
        scratch_shapes=[pltpu.VMEM((LANES, tm), F32), pltpu.VMEM((SEG_ROWS, tm), F32),
                        pltpu.VMEM((SEG_ROWS, LANES), F32)],
        compiler_params=_cparams(("arbitrary",)),
        name="out_proj",
    )(*acts, w, x, g, b, wr_t, bias, tri)


def _odd_in_kernel(x_ref, w_ref, tab_ref, q_ref, kt_ref, kb_ref, vf_ref, vt_ref):
    xb = x_ref[...].astype(BF16)
    W = DIFF_HEADS * 2 * DIFF_HD
    hq = _dot(xb, w_ref[:, 0:W])
    hk = _dot(xb, w_ref[:, W:2 * W])
    for h in range(W // LANES):
        sl = slice(h * LANES, (h + 1) * LANES)
        q_ref[:, sl] = _rope(hq[:, sl], tab_ref[0], tab_ref[1], tab_ref[2], DIFF_HD // 2).astype(BF16)
        kk = _rope(hk[:, sl], tab_ref[3], tab_ref[4], tab_ref[5], DIFF_HD // 2)
        kt_ref[0, sl, :] = kk.T
        kb_ref[:, sl] = kk.astype(BF16)
    hv = _dot(xb, w_ref[:, 2 * W:])
    vf_ref[...] = hv
    vt_ref[0] = hv.T.astype(BF16)


def _odd_in(x, w, tabs, S, tm):
    T, D = x.shape
    W = DIFF_HEADS * 2 * DIFF_HD
    nt = tabs.shape[1] // tm
    row = lambda i: (i, 0)
    blk = pl.BlockSpec((tm, W), row)
    vt_spec, vt_shape = _vt_layout(T, S, tm, DIFF_HEADS * DIFF_V)
    kt_spec, kt_shape = _vt_layout(T, S, tm, W, F32)
    return pl.pallas_call(
        _odd_in_kernel,
        grid=(T // tm,),
        in_specs=[pl.BlockSpec((tm, D), row), pl.BlockSpec(w.shape, lambda i: (0, 0)),
                  pl.BlockSpec((tabs.shape[0], tm, LANES), lambda i: (0, i % nt, 0))],
        out_specs=[blk, kt_spec, blk, blk, vt_spec],
        out_shape=[jax.ShapeDtypeStruct((T, W), BF16), kt_shape,
                   jax.ShapeDtypeStruct((T, W), BF16), jax.ShapeDtypeStruct((T, W), F32), vt_shape],
        compiler_params=_cparams(("parallel",)),
        name="odd_in",
    )(x, w, tabs)


def _route(x, wr_ref, bias_ref, tri_ref, g_ref, route_ref, cnt_out_ref, gt_ref, oh_ref, cnt_ref):
    tm = x.shape[0]
    logits = _dot_nt(wr_ref[...], x.astype(BF16))
    sc = jax.nn.sigmoid(logits)
    sel = sc + bias_ref[...]
    r = [sel[e:e + 1, :] for e in range(N_EXPERTS)]
    s = [sc[e:e + 1, :] for e in range(N_EXPERTS)]
    grp = []
    for g in range(N_GROUPS):
        a, b, c, d = r[4 * g:4 * g + 4]
        top2 = jnp.maximum(jnp.maximum(jnp.maximum(a + b, a + c), jnp.maximum(a + d, b + c)),
                           jnp.maximum(b + d, c + d))
        grp.append(top2)
    best = jnp.maximum(jnp.maximum(grp[0], grp[1]), jnp.maximum(grp[2], grp[3]))
    taken = jnp.zeros((1, tm), jnp.bool_)
    chosen = []
    for g in range(N_GROUPS):
        win = jnp.logical_and(grp[g] == best, jnp.logical_not(taken))
        chosen.append(win)
        taken = jnp.logical_or(taken, win)
    picked = []
    for e in range(N_EXPERTS):
        g = e // EXPERTS_PER_GROUP
        rank = jnp.zeros((1, tm), F32)
        for k in range(4 * g, 4 * g + 4):
            if k < e:
                rank = rank + (r[k] >= r[e]).astype(F32)
            elif k > e:
                rank = rank + (r[k] > r[e]).astype(F32)
        picked.append(jnp.logical_and(chosen[g], rank < 2.0))
    w = [jnp.where(picked[e], s[e], 0.0) for e in range(N_EXPERTS)]
    denom = w[0]
    for e in range(1, N_EXPERTS):
        denom = denom + w[e]
    gt_ref[...] = jnp.zeros_like(gt_ref)
    for e in range(N_EXPERTS):
        gt_ref[e:e + 1, :] = w[e] / denom
    g_ref[...] = gt_ref[...].T

    @pl.when(pl.program_id(0) == 0)
    def _():
        cnt_ref[...] = jnp.zeros_like(cnt_ref)

    oh_ref[...] = jnp.zeros_like(oh_ref)
    for g in range(N_GROUPS):
        for p, (a, b) in enumerate(EXPERT_PAIRS):
            both = jnp.logical_and(picked[4 * g + a], picked[4 * g + b])
            oh_ref[g * N_PAIRS + p:g * N_PAIRS + p + 1, :] = both.astype(F32)
    oh = oh_ref[...]
    before = _dot(oh.astype(BF16), tri_ref[...])
    base = cnt_ref[:, 0:1]
    rank = jnp.sum(oh * (base + before), 0, keepdims=True)
    sid = jnp.sum(oh * lax.broadcasted_iota(jnp.int32, oh.shape, 0).astype(F32), 0, keepdims=True)
    row = lax.broadcasted_iota(jnp.int32, route_ref.shape, 0)
    route_ref[...] = jnp.where(row == 0, sid, jnp.where(row == 1, rank, 0.0)).astype(jnp.int32)
    cnt_ref[...] = cnt_ref[...] + jnp.sum(oh, 1, keepdims=True)
    cnt_out_ref[...] = cnt_ref[...]


def _moe_dense_kernel(x_ref, gate_ref, wg_ref, wu_ref, wd_ref, g_ref, b_ref, o_ref, xb_ref, acc_ref):
    e = pl.program_id(1)

    @pl.when(e == 0)
    def _():
        xb_ref[...] = x_ref[...].astype(BF16)
        acc_ref[...] = jnp.zeros_like(acc_ref)

    xb = xb_ref[...]
    h = jax.nn.silu(_dot(xb, wg_ref[0, 0].astype(BF16))) * _dot(xb, wu_ref[0, 0].astype(BF16))
    y = _dot(h.astype(BF16), wd_ref[0, 0].astype(BF16))
    gates = gate_ref[...]
    lane = lax.broadcasted_iota(jnp.int32, gates.shape, 1)
    gcol = jnp.sum(jnp.where(lane == e, gates, 0.0), -1, keepdims=True)
    acc_ref[...] += gcol * y

    @pl.when(e == pl.num_programs(1) - 1)
    def _():
        o_ref[...] = _layer_norm(DN_ALPHA * x_ref[...] + acc_ref[...], g_ref[...], b_ref[...])


def _moe_dense(x, gates, wg, wu, wd, l, g, b, tm):
    T, D = x.shape
    _, E, _, H = wg.shape
    row = lambda i, e: (i, 0)
    const = lambda i, e: (0, 0)
    wsel = lambda i, e: (l, e, 0, 0)
    return pl.pallas_call(
        _moe_dense_kernel,
        grid=(T // tm, E),
        in_specs=[
            pl.BlockSpec((tm, D), row), pl.BlockSpec((tm, LANES), row),
            pl.BlockSpec((1, 1, D, H), wsel), pl.BlockSpec((1, 1, D, H), wsel),
            pl.BlockSpec((1, 1, H, D), wsel),
            pl.BlockSpec(g.shape, const), pl.BlockSpec(b.shape, const),
        ],
        out_specs=pl.BlockSpec((tm, D), row),
        out_shape=jax.ShapeDtypeStruct((T, D), F32),
        scratch_shapes=[pltpu.VMEM((tm, D), BF16), pltpu.VMEM((tm, D), F32)],
        compiler_params=_cparams(("parallel", "arbitrary")),
        name="moe_dense",
    )(x, gates, wg, wu, wd, g, b)


SC_CORES = 2
SC_SUBCORES = 16
SC_WORKERS = SC_CORES * SC_SUBCORES
SC_TILE_BYTES = 384 * 1024
SPARSE_ROW_TILE = 512
SPARSE_MIN_TOKENS = 4096


def _sc_ring(per_w, row_bytes):
    for ch, nbuf in ((16, 4), (16, 2), (8, 2)):
        if per_w % (ch * nbuf) == 0 and ch * nbuf * row_bytes <= SC_TILE_BYTES:
            return ch, nbuf
    raise ValueError(f"no SparseCore gather ring for {per_w} rows of {row_bytes} bytes per subcore")


def _sc_gather(table, idx):
    R = idx.shape[0]
    D = table.shape[1]
    per_w = R // SC_WORKERS
    assert per_w * SC_WORKERS == R and per_w % 8 == 0
    ch, nbuf = _sc_ring(per_w, D * table.dtype.itemsize)
    nchunk = per_w // ch
    mesh = plsc.VectorSubcoreMesh(core_axis_name="c", subcore_axis_name="s")

    @functools.partial(
        pl.kernel, mesh=mesh,
        out_type=jax.ShapeDtypeStruct((R, D), table.dtype),
        scratch_types=[pltpu.VMEM((per_w,), jnp.int32), pltpu.VMEM((nbuf, ch, D), table.dtype),
                       pltpu.SemaphoreType.DMA((nbuf,)), pltpu.SemaphoreType.DMA((nbuf,))],
    )
    def gather_kernel(table_hbm, idx_hbm, out_hbm, idx_v, rows_v, gsem, wsem):
        base = (lax.axis_index("s") * SC_CORES + lax.axis_index("c")) * per_w
        pltpu.sync_copy(idx_hbm.at[pl.ds(base, per_w)], idx_v)

        def gather(c, b):
            return pltpu.make_async_copy(table_hbm.at[idx_v.at[pl.ds(c * ch, ch)]], rows_v.at[b], gsem.at[b])

        def write(c, b):
            return pltpu.make_async_copy(rows_v.at[b], out_hbm.at[pl.ds(base + c * ch, ch)], wsem.at[b])

        for b in range(nbuf - 1):
            gather(b, b).start()

        @pl.loop(0, nchunk, step=nbuf)
        def _(c):
            for b in range(nbuf):
                cc = c + b
                gather(cc, b).wait()
                write(cc, b).start()
                pb = (b - 1) % nbuf

                @pl.when(cc + nbuf - 1 < nchunk)
                def _():
                    @pl.when(cc >= 1)
                    def _():
                        write(cc - 1, pb).wait()

                    gather(cc + nbuf - 1, pb).start()

        for b in range(nbuf):
            write(nchunk - nbuf + b, b).wait()

    return gather_kernel(table, idx)


def _moe_group_kernel(te_ref, nv_ref, x_ref, gate_ref, *refs):
    w_refs, (g_ref, b_ref, o_ref) = refs[:3 * TOP_K], refs[3 * TOP_K:]
    j = pl.program_id(0)
    live = j < nv_ref[0]

    @pl.when(live)
    def _():
        x = x_ref[...]
        xb = x.astype(BF16)
        gates = gate_ref[...]
        lane = lax.broadcasted_iota(jnp.int32, gates.shape, 1)
        acc = None
        for k in range(TOP_K):
            wg_ref, wu_ref, wd_ref = w_refs[3 * k:3 * k + 3]
            h = jax.nn.silu(_dot(xb, wg_ref[0, 0].astype(BF16))) * _dot(xb, wu_ref[0, 0].astype(BF16))
            y = _dot(h.astype(BF16), wd_ref[0, 0].astype(BF16))
            gcol = jnp.sum(jnp.where(lane == te_ref[TOP_K * j + k], gates, 0.0), -1, keepdims=True)
            acc = gcol * y if acc is None else acc + gcol * y
        o_ref[...] = _layer_norm(DN_ALPHA * x + acc, g_ref[...], b_ref[...])

    @pl.when(jnp.logical_not(live))
    def _():
        o_ref[...] = jnp.zeros_like(o_ref)


def _moe_group(tile_experts, n_valid, xs, gs, wg, wu, wd, l, g, b, tm):
    R, D = xs.shape
    H = wg.shape[3]
    row = lambda j, te, nv: (j, 0)
    const = lambda j, te, nv: (0, 0)
    w_specs, w_args = [], []
    for k in range(TOP_K):
        wsel = lambda j, te, nv, k=k: (l, te[TOP_K * j + k], 0, 0)
        w_specs += [pl.BlockSpec((1, 1, D, H), wsel), pl.BlockSpec((1, 1, D, H), wsel),
                    pl.BlockSpec((1, 1, H, D), wsel)]
        w_args += [wg, wu, wd]
    grid_spec = pltpu.PrefetchScalarGridSpec(
        num_scalar_prefetch=2,
        grid=(R // tm,),
        in_specs=[pl.BlockSpec((tm, D), row), pl.BlockSpec((tm, LANES), row)] + w_specs + [
            pl.BlockSpec(g.shape, const), pl.BlockSpec(b.shape, const)],
        out_specs=pl.BlockSpec((tm, D), row),
    )
    return pl.pallas_call(
        _moe_group_kernel,
        grid_spec=grid_spec,
        out_shape=jax.ShapeDtypeStruct((R, D), F32),
        compiler_params=_cparams(("arbitrary",)),
        name="moe_group",
    )(tile_experts, n_valid, xs, gs, *w_args, g, b)


def _sc_scatter(x, dest, pad_dest, n_rows):
    T, D = x.shape
    P = pad_dest.shape[0]
    per_w, pad_w = T // SC_WORKERS, P // SC_WORKERS
    assert per_w * SC_WORKERS == T and pad_w * SC_WORKERS == P and T + P == n_rows
    ch, nbuf = _sc_ring(per_w, D * x.dtype.itemsize)
    assert pad_w % ch == 0
    nchunk, npad = per_w // ch, pad_w // ch
    mesh = plsc.VectorSubcoreMesh(core_axis_name="c", subcore_axis_name="s")

    @functools.partial(
        pl.kernel, mesh=mesh,
        out_type=jax.ShapeDtypeStruct((n_rows, D), x.dtype),
        scratch_types=[pltpu.VMEM((nchunk, ch), jnp.int32), pltpu.VMEM((npad, ch), jnp.int32),
                       pltpu.VMEM((nbuf, ch, D), x.dtype), pltpu.VMEM((ch, D), x.dtype),
                       pltpu.SemaphoreType.DMA((nbuf,)), pltpu.SemaphoreType.DMA((nbuf,)),
                       pltpu.SemaphoreType.DMA],
    )
    def scatter_kernel(x_hbm, dest_hbm, pad_hbm, zero_hbm, out_hbm, dest_v, pad_v, rows_v, zero_v,
                       rsem, wsem, zsem):
        wid = lax.axis_index("s") * SC_CORES + lax.axis_index("c")
        base = wid * per_w
        pltpu.sync_copy(dest_hbm.at[wid], dest_v)
        pltpu.sync_copy(pad_hbm.at[wid], pad_v)
        pltpu.sync_copy(zero_hbm, zero_v)

        def read(c, b):
            return pltpu.make_async_copy(x_hbm.at[pl.ds(base + c * ch, ch)], rows_v.at[b], rsem.at[b])

        def write(c, b):
            return pltpu.make_async_copy(rows_v.at[b], out_hbm.at[dest_v.at[c]], wsem.at[b])

        def write_zero(c):
            return pltpu.make_async_copy(zero_v, out_hbm.at[pad_v.at[c]], zsem)

        for c in range(npad):
            write_zero(c).start()
        for b in range(nbuf - 1):
            read(b, b).start()

        @pl.loop(0, nchunk, step=nbuf)
        def _(c):
            for b in range(nbuf):
                cc = c + b
                read(cc, b).wait()
                write(cc, b).start()
                pb = (b - 1) % nbuf

                @pl.when(cc + nbuf - 1 < nchunk)
                def _():
                    @pl.when(cc >= 1)
                    def _():
                        write(cc - 1, pb).wait()

                    read(cc + nbuf - 1, pb).start()

        for b in range(nbuf):
            write(nchunk - nbuf + b, b).wait()
        for c in range(npad):
            write_zero(c).wait()

    return scatter_kernel(x, dest.reshape(SC_WORKERS, nchunk, ch), pad_dest.reshape(SC_WORKERS, npad, ch),
                          jnp.zeros((ch, D), x.dtype))


def _moe_sparse(x, gates, route, counts, wg, wu, wd, l, g, b):
    T = x.shape[0]
    tm = SPARSE_ROW_TILE
    n_tiles = T // tm + N_SEG
    n_rows = n_tiles * tm
    sid, rank = route[0], route[1]
    cnt = counts[:N_SEG, 0].astype(jnp.int32)
    tiles = (cnt + tm - 1) // tm
    ends = jnp.cumsum(tiles)
    starts = (ends - tiles) * tm
    seg_ids = jnp.arange(N_SEG, dtype=jnp.int32)
    pick = lambda which, vals: jnp.sum(jnp.where(which[:, None] == seg_ids[None, :], vals[None, :], 0), 1)
    passed = lambda pos, bounds: jnp.sum((pos[:, None] >= bounds[None, :]).astype(jnp.int32), 1)
    dest = rank + pick(sid, starts)
    pad_cnt = tiles * tm - cnt
    pad_end = jnp.cumsum(pad_cnt)
    p = jnp.arange(n_rows - T, dtype=jnp.int32)
    seg = passed(p, pad_end)
    pad_dest = jnp.where(seg < N_SEG,
                         pick(seg, starts + cnt) + p - pick(seg, pad_end - pad_cnt),
                         ends[N_SEG - 1] * tm + p - pad_end[N_SEG - 1])
    tile_seg = jnp.minimum(passed(jnp.arange(n_tiles, dtype=jnp.int32), ends), N_SEG - 1)
    pair = jnp.asarray(np.asarray(EXPERT_PAIRS, np.int32))
    tile_experts = (tile_seg // N_PAIRS * EXPERTS_PER_GROUP)[:, None] + pair[tile_seg % N_PAIRS]
    xs = _sc_scatter(x, dest, pad_dest, n_rows)
    gs = _sc_scatter(gates, dest, pad_dest, n_rows)
    ys = _moe_group(tile_experts.reshape(-1), ends[N_SEG - 1:], xs, gs, wg, wu, wd, l, g, b, tm)
    return _sc_gather(ys, dest)


def _prep_weights(w_in_even, w_uq, w_ukv, w_out_even, w_in_odd, w_out_odd, w_router,
                  w_expert_gate, w_expert_up, w_expert_down):
    n_main = 4 * RET_W + MLA_Q_RANK + MLA_KV_RANK
    w_in = w_in_even[0]
    kr_cols = jnp.pad(w_in[:, n_main:], ((0, 0), (MLA_NOPE, LANES - MLA_NOPE - MLA_ROPE)))
    w_a = jnp.concatenate([w_in[:, :n_main], kr_cols], 1).astype(BF16)
    qd = MLA_NOPE + MLA_ROPE
    wq = jnp.pad(w_uq[0].reshape(MLA_Q_RANK, MLA_HEADS, qd), ((0, 0), (0, 0), (0, LANES - qd)))
    wq = wq.reshape(MLA_Q_RANK, MLA_PAD).astype(BF16)
    wkv = w_ukv[0].reshape(MLA_KV_RANK, MLA_HEADS, MLA_NOPE + MLA_V)
    wk = jnp.pad(wkv[:, :, :MLA_NOPE], ((0, 0), (0, 0), (0, LANES - MLA_NOPE)))
    wk = wk.reshape(MLA_KV_RANK, MLA_PAD).astype(BF16)
    wvt = wkv[:, :, MLA_NOPE:].reshape(MLA_KV_RANK, MLA_HEADS * MLA_V).T.astype(BF16)
    e_np = np.zeros((MLA_ROPE, MLA_HEADS, LANES), np.float32)
    for j in range(MLA_ROPE):
        e_np[j, :, MLA_NOPE + j] = 1.0
    e_mat = jnp.asarray(e_np.reshape(MLA_ROPE, MLA_PAD)).astype(BF16)
    return dict(
        w_a=w_a, wq=wq, wk=wk, wvt=wvt, e_mat=e_mat,
        w_out_even=w_out_even[0].astype(BF16), w_in_odd=w_in_odd[0].astype(BF16),
        w_out_odd=w_out_odd[0].astype(BF16), wr_t=w_router.T.astype(BF16),
        wg=w_expert_gate, wu=w_expert_up, wd=w_expert_down,
    )


def _moe(routed, wts, l, ln_g, ln_b, tm):
    x, gates, route, counts = routed
    g, b = ln_g[l, 1][None], ln_b[l, 1][None]
    if x.shape[0] >= SPARSE_MIN_TOKENS:
        return _moe_sparse(x, gates, route, counts, wts["wg"], wts["wu"], wts["wd"], l, g, b)
    return _moe_dense(x, gates, wts["wg"], wts["wu"], wts["wd"], l, g, b, tm)


def _trunk(x3, pos0, past, wts, prm):
    B, S, D = x3.shape
    T = B * S
    x = x3.reshape(T, D)
    tm = min(T, TOKEN_TILE)
    rep = max(tm // S, 1)
    pos = np.tile(pos0 + np.arange(S), rep)
    ln_g, ln_b = prm["ln_g"], prm["ln_b"]

    causal = past is None
    fuse_kv = causal and S % tm == 0
    outs = _even_in(x, wts["w_a"], wts["wq"], prm["gq"], prm["gkv"], _even_tables(pos), tm,
                    kv=(wts["wk"], wts["wvt"], S) if fuse_kv else None)
    rq, rk, rv, rg, q, lat, kr = outs[:7]
    if causal:
        state0 = jnp.zeros((B, RET_HEADS, RET_DK, RET_DV), F32)
        lat_all, kr_all, sk, skp = lat, kr, S, S
        tq = tk = min(S, ATTN_TILE)
    else:
        state0 = past["state"]
        sk = past["lat"].shape[1] + S
        skp = -(-sk // LANES) * LANES
        padk = lambda parts: jnp.concatenate(
            parts + [jnp.zeros((B, skp - sk, parts[0].shape[2]), parts[0].dtype)], 1).reshape(B * skp, -1)
        lat_all = padk([past["lat"], lat.reshape(B, S, -1)])
        kr_all = padk([past["kr"], kr.reshape(B, S, -1)])
        tq, tk = S, skp
    ret_out, ret_state = _retention(rq, rk, rv, rg, state0, B, S)
    if fuse_kv:
        k_mla, vt_mla = outs[7:]
    else:
        tkv = KV_UP_TILE if (B * skp) % KV_UP_TILE == 0 else skp
        k_mla, vt_mla = _kv_up(lat_all, kr_all, wts["wk"], wts["wvt"], wts["e_mat"], skp, tkv)
        if vt_mla.shape[0] != B:
            vt_mla = vt_mla.reshape(-1, B, skp).transpose(1, 0, 2)
    mla_out = _mla_attn(q, k_mla, vt_mla, B, S, skp, sk, tq, tk, causal)
    routed = _out_proj([ret_out, mla_out], wts["w_out_even"], x, ln_g[0, 0][None], ln_b[0, 0][None],
                       wts["wr_t"], prm["bias"], tm)
    x = _moe(routed, wts, 0, ln_g, ln_b, tm)

    qd, kt, kb, vf, vt = _odd_in(x, wts["w_in_odd"], _odd_tables(pos), S, tm)
    if vt.shape[0] != B:
        vt = vt.reshape(-1, B, S).transpose(1, 0, 2)
        kt = kt.reshape(-1, B, S).transpose(1, 0, 2)
    kf = kt.reshape(1, B, 2 * DIFF_HEADS, DIFF_HD, S).transpose(0, 1, 4, 2, 3)
    if causal:
        k_all, vt_all = kb, vt
    else:
        k_all = padk([past["dk"], kb.reshape(B, S, -1)])
        vt_all = jnp.concatenate([past["dv"].transpose(0, 2, 1), vt,
                                  jnp.zeros((B, vt.shape[1], skp - sk), BF16)], 2)
    lam_init = 0.8 - 0.6 * math.exp(-0.3 * 1)
    d_out = _diff_attn(prm["lam"], prm["gn"], qd, k_all, vt_all, B, S, skp, sk, tq, tk, causal, lam_init)
    routed = _out_proj([d_out], wts["w_out_odd"], x, ln_g[1, 0][None], ln_b[1, 0][None],
                       wts["wr_t"], prm["bias"], tm)
    x = _moe(routed, wts, 1, ln_g, ln_b, tm)

    return (x.reshape(B, S, D), ret_state[None], lat.reshape(1, B, S, -1), kr.reshape(1, B, S, -1),
            kf, vf.reshape(1, B, S, DIFF_HEADS, DIFF_V))


def kernel(x_prompt, x_sample, state_ret, cache_mla_latent, cache_mla_krope, cache_diff_k, cache_diff_v,
           w_in_even, w_uq, w_ukv, g_qnorm, g_kvnorm, w_out_even,
           w_in_odd, lambda_q1, lambda_k1, lambda_q2, lambda_k2, g_diff_norm, w_out_odd,
           ln_g, ln_b, w_router, router_bias, w_expert_gate, w_expert_up, w_expert_down):
    wts = _prep_weights(w_in_even, w_uq, w_ukv, w_out_even, w_in_odd, w_out_odd, w_router,
                        w_expert_gate, w_expert_up, w_expert_down)
    prm = dict(
        gq=g_qnorm[0][None].astype(F32), gkv=g_kvnorm[0][None].astype(F32),
        lam=jnp.stack([lambda_q1[0], lambda_k1[0], lambda_q2[0], lambda_k2[0]]).astype(F32),
        gn=g_diff_norm[0][None].astype(F32), bias=router_bias.reshape(N_EXPERTS, 1).astype(F32),
        ln_g=ln_g.astype(F32), ln_b=ln_b.astype(F32),
    )
    past_len = cache_mla_latent.shape[2]
    db = x_sample.shape[0]
    past = dict(
        state=state_ret[0].astype(F32), lat=cache_mla_latent[0], kr=cache_mla_krope[0],
        dk=cache_diff_k[0].reshape(db, past_len, -1).astype(BF16),
        dv=cache_diff_v[0].reshape(db, past_len, -1).astype(BF16),
    )
    outs_p = _trunk(x_prompt, 0, None, wts, prm)
    outs_s = _trunk(x_sample, past_len, past, wts, prm)
    return (outs_p[0], outs_s[0]) + outs_p[1:] + outs_s[1:]
```

```python
import functools
import math

import numpy as np
import jax
import jax.numpy as jnp
from jax import lax
from jax.experimental import pallas as pl
from jax.experimental.pallas import tpu as pltpu
from jax.experimental.pallas import tpu_sc as plsc

F32 = jnp.float32
BF16 = jnp.bfloat16

CHUNK = 64
ROPE_THETA = 10000.0
NEG_INF = -1e30
LN_EPS = 1e-5
NORM_EPS = 1e-6
DEPTH = 2
DN_ALPHA = (2.0 * DEPTH) ** 0.25
RET_HEADS = 4
RET_DK = 128
RET_DV = 128
RET_LOG_GAMMA = tuple(math.log(1.0 - 2.0 ** (-5 - h)) for h in range(RET_HEADS))
MLA_HEADS = 8
MLA_Q_RANK = 384
MLA_KV_RANK = 256
MLA_NOPE = 64
MLA_ROPE = 32
MLA_V = 64
DIFF_HEADS = 8
DIFF_HD = 64
DIFF_V = 128
N_EXPERTS = 16
N_GROUPS = 4
EXPERTS_PER_GROUP = 4
TOP_K = 2
EXPERT_PAIRS = tuple((a, b) for a in range(EXPERTS_PER_GROUP) for b in range(a + 1, EXPERTS_PER_GROUP))
N_PAIRS = len(EXPERT_PAIRS)
N_SEG = N_GROUPS * N_PAIRS
SEG_ROWS = 32
LOG2E = math.log2(math.e)

LANES = 128
RET_W = RET_HEADS * RET_DK
MLA_PAD = MLA_HEADS * LANES
VMEM_LIMIT = 56 * 1024 * 1024
TOKEN_TILE = 1024
KV_UP_TILE = 512
RET_CHUNK = 256
RET_TILE = 512
ATTN_TILE = 1024


def _cparams(sem):
    return pltpu.CompilerParams(dimension_semantics=sem, vmem_limit_bytes=VMEM_LIMIT)


def _rope_tables(pos, d, group, offset, scale):
    pos = np.asarray(pos, np.float64)
    half = d // 2
    inv = 1.0 / (ROPE_THETA ** (np.arange(0, d, 2, dtype=np.float64) / d))
    ang = pos[:, None] * inv[None, :]
    cos = np.full((pos.shape[0], LANES), scale, np.float64)
    s_lo = np.zeros((pos.shape[0], LANES), np.float64)
    s_hi = np.zeros((pos.shape[0], LANES), np.float64)
    start = offset
    while start + d <= LANES:
        cos[:, start:start + half] = np.cos(ang) * scale
        cos[:, start + half:start + d] = np.cos(ang) * scale
        s_lo[:, start:start + half] = -np.sin(ang) * scale
        s_hi[:, start + half:start + d] = np.sin(ang) * scale
        start += group
    return cos, s_lo, s_hi


def _even_tables(pos):
    rq = _rope_tables(pos, RET_DK, LANES, 0, 1.0)
    rk = _rope_tables(pos, RET_DK, LANES, 0, RET_DK ** -0.5)
    c = (MLA_NOPE + MLA_ROPE) ** -0.5 * LOG2E
    mq = _rope_tables(pos, MLA_ROPE, LANES, MLA_NOPE, c)
    mk = _rope_tables(pos, MLA_ROPE, LANES, MLA_NOPE, 1.0)
    tabs = [rq[0], rq[1] + rq[2], rk[0], rk[1] + rk[2], mq[0], mq[1], mq[2], mk[0], mk[1], mk[2]]
    return jnp.asarray(np.stack(tabs).astype(np.float32))


def _odd_tables(pos):
    c = DIFF_HD ** -0.5 * LOG2E
    dq = _rope_tables(pos, DIFF_HD, DIFF_HD, 0, c)
    dk = _rope_tables(pos, DIFF_HD, DIFF_HD, 0, 1.0)
    return jnp.asarray(np.stack(list(dq) + list(dk)).astype(np.float32))


def _retention_tables(L):
    lg = np.asarray(RET_LOG_GAMMA, np.float64)
    idx = np.arange(L, dtype=np.float64)
    diff = idx[:, None] - idx[None, :]
    dmask = np.where(diff[None] >= 0, np.exp(np.maximum(diff, 0.0)[None] * lg[:, None, None]), 0.0)
    qd = np.exp((idx[None, :] + 1.0) * lg[:, None])
    kd = np.exp((L - 1.0 - idx)[None, :] * lg[:, None])
    gl = np.exp(L * lg)
    qd = np.broadcast_to(qd[:, :, None], (RET_HEADS, L, LANES))
    kd = np.broadcast_to(kd[:, :, None], (RET_HEADS, L, LANES))
    gl = np.broadcast_to(gl[:, None, None], (RET_HEADS, RET_DK, RET_DV))
    f = lambda a: jnp.asarray(np.ascontiguousarray(a).astype(np.float32))
    return f(dmask), f(qd), f(kd), f(gl)


def _dot(a, b):
    return jnp.dot(a, b, preferred_element_type=F32)


def _dot_nt(a, b):
    return lax.dot_general(a, b, (((1,), (1,)), ((), ())), preferred_element_type=F32)


def _dot_tn(a, b):
    return lax.dot_general(a, b, (((0,), (0,)), ((), ())), preferred_element_type=F32)


def _rope(x, cos, s_lo, s_hi, half):
    return x * cos + pltpu.roll(x, LANES - half, 1) * s_lo + pltpu.roll(x, half, 1) * s_hi


def _layer_norm(x, g, b):
    mu = jnp.mean(x, -1, keepdims=True)
    xc = x - mu
    var = jnp.mean(xc * xc, -1, keepdims=True)
    return xc * lax.rsqrt(var + LN_EPS) * g + b


def _rms_norm(x, g):
    return x * lax.rsqrt(jnp.mean(x * x, -1, keepdims=True) + NORM_EPS) * g


def _even_in_kernel(x_ref, w_ref, wq_ref, gq_ref, gkv_ref, tab_ref, *refs, fuse_kv):
    if fuse_kv:
        wk_ref, wvt_ref, rq_ref, rk_ref, rv_ref, rg_ref, q_ref, lat_ref, kr_ref, k_ref, vt_ref = refs
    else:
        rq_ref, rk_ref, rv_ref, rg_ref, q_ref, lat_ref, kr_ref = refs
    xb = x_ref[...].astype(BF16)
    c_rq, s_rq, c_rk, s_rk = tab_ref[0], tab_ref[1], tab_ref[2], tab_ref[3]
    hq = _dot(xb, w_ref[:, 0:RET_W])
    hk = _dot(xb, w_ref[:, RET_W:2 * RET_W])
    for h in range(RET_HEADS):
        sl = slice(h * LANES, (h + 1) * LANES)
        xq = hq[:, sl]
        rq_ref[:, sl] = (xq * c_rq + pltpu.roll(xq, RET_DK // 2, 1) * s_rq).astype(BF16)
        xk = hk[:, sl]
        rk_ref[:, sl] = (xk * c_rk + pltpu.roll(xk, RET_DK // 2, 1) * s_rk).astype(BF16)
    rv_ref[...] = _dot(xb, w_ref[:, 2 * RET_W:3 * RET_W]).astype(BF16)
    rg_ref[...] = _dot(xb, w_ref[:, 3 * RET_W:4 * RET_W]).astype(BF16)
    o = 4 * RET_W
    cq = _dot(xb, w_ref[:, o:o + MLA_Q_RANK])
    qn = _rms_norm(cq, gq_ref[...]).astype(BF16)
    qf = _dot(qn, wq_ref[...])
    c_q, lo_q, hi_q = tab_ref[4], tab_ref[5], tab_ref[6]
    for h in range(MLA_HEADS):
        sl = slice(h * LANES, (h + 1) * LANES)
        q_ref[:, sl] = _rope(qf[:, sl], c_q, lo_q, hi_q, MLA_ROPE // 2).astype(BF16)
    o += MLA_Q_RANK
    ckv = _dot(xb, w_ref[:, o:o + MLA_KV_RANK])
    lat = _rms_norm(ckv, gkv_ref[...])
    lat_ref[...] = lat
    o += MLA_KV_RANK
    krp = _dot(xb, w_ref[:, o:o + LANES])
    krp = _rope(krp, tab_ref[7], tab_ref[8], tab_ref[9], MLA_ROPE // 2)
    kr_ref[...] = krp[:, MLA_NOPE:MLA_NOPE + MLA_ROPE]
    if fuse_kv:
        lb = lat.astype(BF16)
        kn = _dot(lb, wk_ref[...])
        for h in range(MLA_HEADS):
            sl = slice(h * LANES, (h + 1) * LANES)
            k_ref[:, sl] = (kn[:, sl] + krp).astype(BF16)
        vt_ref[0] = _dot_nt(wvt_ref[...], lb).astype(BF16)


def _even_in(x, w_a, wq, gq, gkv, tabs, tm, kv=None):
    T, D = x.shape
    P = tabs.shape[1]
    nt = P // tm
    row = lambda i: (i, 0)
    const = lambda i: (0, 0)
    outs = [
        jax.ShapeDtypeStruct((T, RET_W), BF16), jax.ShapeDtypeStruct((T, RET_W), BF16),
        jax.ShapeDtypeStruct((T, RET_W), BF16), jax.ShapeDtypeStruct((T, RET_W), BF16),
        jax.ShapeDtypeStruct((T, MLA_PAD), BF16),
        jax.ShapeDtypeStruct((T, MLA_KV_RANK), F32), jax.ShapeDtypeStruct((T, MLA_ROPE), F32),
    ]
    in_specs = [
        pl.BlockSpec((tm, D), row),
        pl.BlockSpec(w_a.shape, const),
        pl.BlockSpec(wq.shape, const),
        pl.BlockSpec(gq.shape, const),
        pl.BlockSpec(gkv.shape, const),
        pl.BlockSpec((tabs.shape[0], tm, LANES), lambda i: (0, i % nt, 0)),
    ]
    out_specs = [
        pl.BlockSpec((tm, RET_W), row), pl.BlockSpec((tm, RET_W), row),
        pl.BlockSpec((tm, RET_W), row), pl.BlockSpec((tm, RET_W), row),
        pl.BlockSpec((tm, MLA_PAD), row),
        pl.BlockSpec((tm, MLA_KV_RANK), row), pl.BlockSpec((tm, MLA_ROPE), row),
    ]
    args = [x, w_a, wq, gq, gkv, tabs]
    if kv is not None:
        wk, wvt, S = kv
        vt_spec, vt_shape = _vt_layout(T, S, tm, MLA_HEADS * MLA_V)
        in_specs += [pl.BlockSpec(wk.shape, const), pl.BlockSpec(wvt.shape, const)]
        out_specs += [pl.BlockSpec((tm, MLA_PAD), row), vt_spec]
        outs += [jax.ShapeDtypeStruct((T, MLA_PAD), BF16), vt_shape]
        args += [wk, wvt]
    return pl.pallas_call(
        functools.partial(_even_in_kernel, fuse_kv=kv is not None),
        grid=(T // tm,),
        in_specs=in_specs,
        out_specs=out_specs,
        out_shape=outs,
        compiler_params=_cparams(("parallel",)),
        name="even_in",
    )(*args)


def _kv_up_kernel(lat_ref, kr_ref, wk_ref, wvt_ref, e_ref, k_ref, vt_ref):
    lb = lat_ref[...].astype(BF16)
    krb = kr_ref[...].astype(BF16)
    k_ref[...] = (_dot(lb, wk_ref[...]) + _dot(krb, e_ref[...])).astype(BF16)
    vt_ref[0] = _dot_nt(wvt_ref[...], lb).astype(BF16)


def _vt_layout(T, S, tm, width, dtype=BF16):
    nb, cols = (T // S, S) if S % tm == 0 else (1, T)
    nt = cols // tm
    spec = pl.BlockSpec((1, width, tm), lambda i: (i // nt, 0, i % nt))
    return spec, jax.ShapeDtypeStruct((nb, width, cols), dtype)


def _kv_up(lat, kr, wk, wvt, e_mat, S, tm):
    T = lat.shape[0]
    row = lambda i: (i, 0)
    const = lambda i: (0, 0)
    vt_spec, vt_shape = _vt_layout(T, S, tm, MLA_HEADS * MLA_V)
    return pl.pallas_call(
        _kv_up_kernel,
        grid=(T // tm,),
        in_specs=[
            pl.BlockSpec((tm, MLA_KV_RANK), row), pl.BlockSpec((tm, MLA_ROPE), row),
            pl.BlockSpec(wk.shape, const), pl.BlockSpec(wvt.shape, const), pl.BlockSpec(e_mat.shape, const),
        ],
        out_specs=[pl.BlockSpec((tm, MLA_PAD), row), vt_spec],
        out_shape=[jax.ShapeDtypeStruct((T, MLA_PAD), BF16), vt_shape],
        compiler_params=_cparams(("parallel",)),
        name="kv_up",
    )(lat, kr, wk, wvt, e_mat)


def _retention_kernel(q_ref, k_ref, v_ref, g_ref, s0_ref, dm_ref, qd_ref, kd_ref, gl_ref,
                      o_ref, st_ref, *, L, nchunk):
    @pl.when(pl.program_id(1) == 0)
    def _():
        st_ref[...] = s0_ref[...]

    for c in range(nchunk):
        rows = slice(c * L, (c + 1) * L)
        for h in range(RET_HEADS):
            sl = slice(h * LANES, (h + 1) * LANES)
            q = q_ref[rows, sl]
            k = k_ref[rows, sl]
            v = v_ref[rows, sl]
            st = st_ref[0, h]
            a = (_dot_nt(q, k) * dm_ref[h]).astype(BF16)
            o = _dot(a, v) + _dot(q, st.astype(BF16)) * qd_ref[h]
            kdec = (k.astype(F32) * kd_ref[h]).astype(BF16)
            st_ref[0, h] = st * gl_ref[h] + _dot_tn(kdec, v)
            mu = jnp.mean(o, -1, keepdims=True)
            oc = o - mu
            var = jnp.mean(oc * oc, -1, keepdims=True)
            on = oc * lax.rsqrt(var + LN_EPS)
            g = g_ref[rows, sl].astype(F32)
            o_ref[rows, sl] = (g * jax.nn.sigmoid(g) * on).astype(BF16)


def _retention(rq, rk, rv, rg, state0, B, S):
    L = min(S, RET_CHUNK)
    lt = min(S, RET_TILE)
    nj = S // lt
    dm, qd, kd, gl = _retention_tables(L)
    row = lambda b, j: (b * nj + j, 0)
    c3 = lambda b, j: (0, 0, 0)
    st_spec = pl.BlockSpec((1, RET_HEADS, RET_DK, RET_DV), lambda b, j: (b, 0, 0, 0))
    return pl.pallas_call(
        functools.partial(_retention_kernel, L=L, nchunk=lt // L),
        grid=(B, nj),
        in_specs=[pl.BlockSpec((lt, RET_W), row)] * 4 + [
            st_spec,
            pl.BlockSpec(dm.shape, c3), pl.BlockSpec(qd.shape, c3),
            pl.BlockSpec(kd.shape, c3), pl.BlockSpec(gl.shape, c3),
        ],
        out_specs=[pl.BlockSpec((lt, RET_W), row), st_spec],
        out_shape=[jax.ShapeDtypeStruct((B * S, RET_W), BF16),
                   jax.ShapeDtypeStruct((B, RET_HEADS, RET_DK, RET_DV), F32)],
        compiler_params=_cparams(("parallel", "arbitrary")),
        name="retention",
    )(rq, rk, rv, rg, state0, dm, qd, kd, gl)


def _query_t(q, tq):
    q = q.astype(F32)
    if tq < LANES:
        q = jnp.concatenate([q, jnp.zeros((LANES - tq, LANES), F32)], 0)
    return q.T


def _flash_t(streams, qi, tq, tk, sk, sk_valid, causal, m_ref, l_ref, acc_ref):
    for s in range(len(streams)):
        m_ref[s] = jnp.full(m_ref.shape[1:], NEG_INF, F32)
        l_ref[s] = jnp.zeros(l_ref.shape[1:], F32)
        acc_ref[s] = jnp.zeros(acc_ref.shape[1:], F32)

    def step(start, size, rel=None, valid=None):
        for s, (q_t, k_at, vt_at) in enumerate(streams):
            st = _dot(k_at(start, size), q_t)
            if rel is not None:
                kc = (lax.broadcasted_iota(jnp.int32, st.shape, 0) + rel) // CHUNK
                qc = lax.broadcasted_iota(jnp.int32, st.shape, 1) // CHUNK
                st = jnp.where(kc <= qc, st, NEG_INF)
            if valid is not None:
                st = jnp.where(lax.broadcasted_iota(jnp.int32, st.shape, 0) < valid, st, NEG_INF)
            m_old = m_ref[s]
            m_new = jnp.maximum(m_old, jnp.max(st, 0, keepdims=True))
            p = jnp.exp2(st - m_new)
            alpha = jnp.exp2(m_old - m_new)
            l_ref[s] = alpha * l_ref[s] + jnp.sum(p, 0, keepdims=True)
            acc_ref[s] = acc_ref[s] * alpha + _dot(vt_at(start, size), p.astype(BF16))
            m_ref[s] = m_new

    if causal:
        n_full = qi * (tq // tk)

        def body(j, c):
            step(pl.multiple_of(2 * j * tk, tk), tk)
            step(pl.multiple_of((2 * j + 1) * tk, tk), tk)
            return c

        lax.fori_loop(0, n_full // 2, body, 0)

        if (tq // tk) % 2 == 1:
            @pl.when(n_full % 2 == 1)
            def _():
                step(pl.multiple_of((n_full - 1) * tk, tk), tk)

        for d in range(tq // tk):
            step(pl.multiple_of(qi * tq + d * tk, tk), tk, rel=d * tk)
    else:
        for j in range(sk // tk):
            last_valid = sk_valid - j * tk
            step(j * tk, tk, valid=last_valid if last_valid < tk else None)
    return [acc_ref[s] / l_ref[s] for s in range(len(streams))]


def _attn_scratch(n_streams, dv, tq):
    tqp = max(tq, LANES)
    return [pltpu.VMEM((n_streams, 1, tqp), F32), pltpu.VMEM((n_streams, 1, tqp), F32),
            pltpu.VMEM((n_streams, dv, tqp), F32)]


def _mla_attn_kernel(q_ref, k_ref, vt_ref, o_ref, m_ref, l_ref, acc_ref, *, tq, tk, sk, sk_valid, causal):
    streams = []
    for hh in range(2):
        sl = slice(hh * LANES, (hh + 1) * LANES)
        vrows = slice(hh * MLA_V, (hh + 1) * MLA_V)
        k_at = lambda start, n, sl=sl: k_ref[pl.ds(start, n), sl]
        vt_at = lambda start, n, vrows=vrows: vt_ref[0, vrows, pl.ds(start, n)]
        streams.append((_query_t(q_ref[:, sl], tq).astype(BF16), k_at, vt_at))
    outs = _flash_t(streams, pl.program_id(2), tq, tk, sk, sk_valid, causal, m_ref, l_ref, acc_ref)
    o_ref[...] = jnp.concatenate(outs, 0).T[:tq].astype(BF16)


def _mla_attn(q, k, vt, B, sq, sk, sk_valid, tq, tk, causal):
    nq = sq // tq
    npair = MLA_HEADS // 2
    return pl.pallas_call(
        functools.partial(_mla_attn_kernel, tq=tq, tk=tk, sk=sk, sk_valid=sk_valid, causal=causal),
        grid=(B, npair, nq),
        in_specs=[
            pl.BlockSpec((tq, 2 * LANES), lambda b, p, i: (b * nq + i, p)),
            pl.BlockSpec((sk, 2 * LANES), lambda b, p, i: (b, p)),
            pl.BlockSpec((1, 2 * MLA_V, sk), lambda b, p, i: (b, p, 0)),
        ],
        out_specs=pl.BlockSpec((tq, LANES), lambda b, p, i: (b * nq + i, p)),
        out_shape=jax.ShapeDtypeStruct((B * sq, MLA_HEADS * MLA_V), BF16),
        scratch_shapes=_attn_scratch(2, MLA_V, tq),
        compiler_params=_cparams(("parallel", "parallel", "arbitrary")),
        name="mla_attn",
    )(q, k, vt)


def _diff_attn_kernel(lam_ref, gn_ref, q_ref, k_ref, vt_ref, o_ref, m_ref, l_ref, acc_ref,
                      *, tq, tk, sk, sk_valid, causal, lam_init):
    k_at = lambda start, n: k_ref[pl.ds(start, n), :]
    vt_at = lambda start, n: vt_ref[0, :, pl.ds(start, n)]
    q_t = _query_t(q_ref[...], tq)
    feat = lax.broadcasted_iota(jnp.int32, q_t.shape, 0)
    zero = jnp.zeros_like(q_t)
    q1 = jnp.where(feat < DIFF_HD, q_t, zero).astype(BF16)
    q2 = jnp.where(feat < DIFF_HD, zero, q_t).astype(BF16)
    o1, o2 = _flash_t([(q1, k_at, vt_at), (q2, k_at, vt_at)], pl.program_id(2), tq, tk, sk, sk_valid,
                      causal, m_ref, l_ref, acc_ref)
    lv = lam_ref[...]
    lam = (jnp.exp(jnp.sum(lv[0:1] * lv[1:2], -1, keepdims=True))
           - jnp.exp(jnp.sum(lv[2:3] * lv[3:4], -1, keepdims=True)) + lam_init)
    o = (o1 - lam * o2).T[:tq]
    o_ref[...] = (_rms_norm(o, gn_ref[...]) * (1.0 - lam_init)).astype(BF16)


def _diff_attn(lam_vecs, gn, q, k, vt, B, sq, sk, sk_valid, tq, tk, causal, lam_init):
    nq = sq // tq
    const = lambda b, h, i: (0, 0)
    return pl.pallas_call(
        functools.partial(_diff_attn_kernel, tq=tq, tk=tk, sk=sk, sk_valid=sk_valid, causal=causal,
                          lam_init=lam_init),
        grid=(B, DIFF_HEADS, nq),
        in_specs=[
            pl.BlockSpec(lam_vecs.shape, const), pl.BlockSpec(gn.shape, const),
            pl.BlockSpec((tq, LANES), lambda b, h, i: (b * nq + i, h)),
            pl.BlockSpec((sk, LANES), lambda b, h, i: (b, h)),
            pl.BlockSpec((1, DIFF_V, sk), lambda b, h, i: (b, h, 0)),
        ],
        out_specs=pl.BlockSpec((tq, LANES), lambda b, h, i: (b * nq + i, h)),
        out_shape=jax.ShapeDtypeStruct((B * sq, DIFF_HEADS * DIFF_V), BF16),
        scratch_shapes=_attn_scratch(2, DIFF_V, tq),
        compiler_params=_cparams(("parallel", "parallel", "arbitrary")),
        name="diff_attn",
    )(lam_vecs, gn, q, k, vt)


def _out_proj_kernel(*refs, n_in):
    a_refs = refs[:n_in]
    (w_ref, x_ref, g_ref, b_ref, wr_ref, bias_ref, tri_ref,
     o_ref, gate_ref, route_ref, cnt_out_ref, gt_ref, oh_ref, cnt_ref) = refs[n_in:]
    y = None
    off = 0
    for a_ref in a_refs:
        width = a_ref.shape[1]
        part = _dot(a_ref[...], w_ref[off:off + width, :])
        y = part if y is None else y + part
        off += width
    x1 = _layer_norm(DN_ALPHA * x_ref[...] + y, g_ref[...], b_ref[...])
    o_ref[...] = x1
    _route(x1, wr_ref, bias_ref, tri_ref, gate_ref, route_ref, cnt_out_ref, gt_ref, oh_ref, cnt_ref)


def _out_proj(acts, w, x, g, b, wr_t, bias, tm):
    T, D = x.shape
    tri = jnp.asarray(np.triu(np.ones((tm, tm), np.float32), 1)).astype(BF16)
    row = lambda i: (i, 0)
    const = lambda i: (0, 0)
    return pl.pallas_call(
        functools.partial(_out_proj_kernel, n_in=len(acts)),
        grid=(T // tm,),
        in_specs=[pl.BlockSpec((tm, a.shape[1]), row) for a in acts] + [
            pl.BlockSpec(w.shape, const), pl.BlockSpec((tm, D), row),
            pl.BlockSpec(g.shape, const), pl.BlockSpec(b.shape, const),
            pl.BlockSpec(wr_t.shape, const), pl.BlockSpec(bias.shape, const), pl.BlockSpec(tri.shape, const),
        ],
        out_specs=[pl.BlockSpec((tm, D), row), pl.BlockSpec((tm, LANES), row),
                   pl.BlockSpec((8, tm), lambda i: (0, i)), pl.BlockSpec((SEG_ROWS, LANES), const)],
        out_shape=[jax.ShapeDtypeStruct((T, D), F32), jax.ShapeDtypeStruct((T, LANES), F32),
                   jax.ShapeDtypeStruct((8, T), jnp.int32), jax.ShapeDtypeStruct((SEG_ROWS, LANES), F32)],
        scratch_shapes=[pltpu.VMEM((LANES, tm), F32), pltpu.VMEM((SEG_ROWS, tm), F32),
                        pltpu.VMEM((SEG_ROWS, LANES), F32)],
        compiler_params=_cparams(("arbitrary",)),
        name="out_proj",
    )(*acts, w, x, g, b, wr_t, bias, tri)


def _odd_in_kernel(x_ref, w_ref, tab_ref, q_ref, kt_ref, kb_ref, vf_ref, vt_ref):
    xb = x_ref[...].astype(BF16)
    W = DIFF_HEADS * 2 * DIFF_HD
    hq = _dot(xb, w_ref[:, 0:W])
    hk = _dot(xb, w_ref[:, W:2 * W])
    for h in range(W // LANES):
        sl = slice(h * LANES, (h + 1) * LANES)
        q_ref[:, sl] = _rope(hq[:, sl], tab_ref[0], tab_ref[1], tab_ref[2], DIFF_HD // 2).astype(BF16)
        kk = _rope(hk[:, sl], tab_ref[3], tab_ref[4], tab_ref[5], DIFF_HD // 2)
        kt_ref[0, sl, :] = kk.T
        kb_ref[:, sl] = kk.astype(BF16)
    hv = _dot(xb, w_ref[:, 2 * W:])
    vf_ref[...] = hv
    vt_ref[0] = hv.T.astype(BF16)


def _odd_in(x, w, tabs, S, tm):
    T, D = x.shape
    W = DIFF_HEADS * 2 * DIFF_HD
    nt = tabs.shape[1] // tm
    row = lambda i: (i, 0)
    blk = pl.BlockSpec((tm, W), row)
    vt_spec, vt_shape = _vt_layout(T, S, tm, DIFF_HEADS * DIFF_V)
    kt_spec, kt_shape = _vt_layout(T, S, tm, W, F32)
    return pl.pallas_call(
        _odd_in_kernel,
        grid=(T // tm,),
        in_specs=[pl.BlockSpec((tm, D), row), pl.BlockSpec(w.shape, lambda i: (0, 0)),
                  pl.BlockSpec((tabs.shape[0], tm, LANES), lambda i: (0, i % nt, 0))],
        out_specs=[blk, kt_spec, blk, blk, vt_spec],
        out_shape=[jax.ShapeDtypeStruct((T, W), BF16), kt_shape,
                   jax.ShapeDtypeStruct((T, W), BF16), jax.ShapeDtypeStruct((T, W), F32), vt_shape],
        compiler_params=_cparams(("parallel",)),
        name="odd_in",
    )(x, w, tabs)


def _route(x, wr_ref, bias_ref, tri_ref, g_ref, route_ref, cnt_out_ref, gt_ref, oh_ref, cnt_ref):
    tm = x.shape[0]
    logits = _dot_nt(wr_ref[...], x.astype(BF16))
    sc = jax.nn.sigmoid(logits)
    sel = sc + bias_ref[...]
    r = [sel[e:e + 1, :] for e in range(N_EXPERTS)]
    s = [sc[e:e + 1, :] for e in range(N_EXPERTS)]
    grp = []
    for g in range(N_GROUPS):
        a, b, c, d = r[4 * g:4 * g + 4]
        top2 = jnp.maximum(jnp.maximum(jnp.maximum(a + b, a + c), jnp.maximum(a + d, b + c)),
                           jnp.maximum(b + d, c + d))
        grp.append(top2)
    best = jnp.maximum(jnp.maximum(grp[0], grp[1]), jnp.maximum(grp[2], grp[3]))
    taken = jnp.zeros((1, tm), jnp.bool_)
    chosen = []
    for g in range(N_GROUPS):
        win = jnp.logical_and(grp[g] == best, jnp.logical_not(taken))
        chosen.append(win)
        taken = jnp.logical_or(taken, win)
    picked = []
    for e in range(N_EXPERTS):
        g = e // EXPERTS_PER_GROUP
        rank = jnp.zeros((1, tm), F32)
        for k in range(4 * g, 4 * g + 4):
            if k < e:
                rank = rank + (r[k] >= r[e]).astype(F32)
            elif k > e:
                rank = rank + (r[k] > r[e]).astype(F32)
        picked.append(jnp.logical_and(chosen[g], rank < 2.0))
    w = [jnp.where(picked[e], s[e], 0.0) for e in range(N_EXPERTS)]
    denom = w[0]
    for e in range(1, N_EXPERTS):
        denom = denom + w[e]
    gt_ref[...] = jnp.zeros_like(gt_ref)
    for e in range(N_EXPERTS):
        gt_ref[e:e + 1, :] = w[e] / denom
    g_ref[...] = gt_ref[...].T

    @pl.when(pl.program_id(0) == 0)
    def _():
        cnt_ref[...] = jnp.zeros_like(cnt_ref)

    oh_ref[...] = jnp.zeros_like(oh_ref)
    for g in range(N_GROUPS):
        for p, (a, b) in enumerate(EXPERT_PAIRS):
            both = jnp.logical_and(picked[4 * g + a], picked[4 * g + b])
            oh_ref[g * N_PAIRS + p:g * N_PAIRS + p + 1, :] = both.astype(F32)
    oh = oh_ref[...]
    before = _dot(oh.astype(BF16), tri_ref[...])
    base = cnt_ref[:, 0:1]
    rank = jnp.sum(oh * (base + before), 0, keepdims=True)
    sid = jnp.sum(oh * lax.broadcasted_iota(jnp.int32, oh.shape, 0).astype(F32), 0, keepdims=True)
    row = lax.broadcasted_iota(jnp.int32, route_ref.shape, 0)
    route_ref[...] = jnp.where(row == 0, sid, jnp.where(row == 1, rank, 0.0)).astype(jnp.int32)
    cnt_ref[...] = cnt_ref[...] + jnp.sum(oh, 1, keepdims=True)
    cnt_out_ref[...] = cnt_ref[...]


def _moe_dense_kernel(x_ref, gate_ref, wg_ref, wu_ref, wd_ref, g_ref, b_ref, o_ref, xb_ref, acc_ref):
    e = pl.program_id(1)

    @pl.when(e == 0)
    def _():
        xb_ref[...] = x_ref[...].astype(BF16)
        acc_ref[...] = jnp.zeros_like(acc_ref)

    xb = xb_ref[...]
    h = jax.nn.silu(_dot(xb, wg_ref[0, 0].astype(BF16))) * _dot(xb, wu_ref[0, 0].astype(BF16))
    y = _dot(h.astype(BF16), wd_ref[0, 0].astype(BF16))
    gates = gate_ref[...]
    lane = lax.broadcasted_iota(jnp.int32, gates.shape, 1)
    gcol = jnp.sum(jnp.where(lane == e, gates, 0.0), -1, keepdims=True)
    acc_ref[...] += gcol * y

    @pl.when(e == pl.num_programs(1) - 1)
    def _():
        o_ref[...] = _layer_norm(DN_ALPHA * x_ref[...] + acc_ref[...], g_ref[...], b_ref[...])


def _moe_dense(x, gates, wg, wu, wd, l, g, b, tm):
    T, D = x.shape
    _, E, _, H = wg.shape
    row = lambda i, e: (i, 0)
    const = lambda i, e: (0, 0)
    wsel = lambda i, e: (l, e, 0, 0)
    return pl.pallas_call(
        _moe_dense_kernel,
        grid=(T // tm, E),
        in_specs=[
            pl.BlockSpec((tm, D), row), pl.BlockSpec((tm, LANES), row),
            pl.BlockSpec((1, 1, D, H), wsel), pl.BlockSpec((1, 1, D, H), wsel),
            pl.BlockSpec((1, 1, H, D), wsel),
            pl.BlockSpec(g.shape, const), pl.BlockSpec(b.shape, const),
        ],
        out_specs=pl.BlockSpec((tm, D), row),
        out_shape=jax.ShapeDtypeStruct((T, D), F32),
        scratch_shapes=[pltpu.VMEM((tm, D), BF16), pltpu.VMEM((tm, D), F32)],
        compiler_params=_cparams(("parallel", "arbitrary")),
        name="moe_dense",
    )(x, gates, wg, wu, wd, g, b)


SC_CORES = 2
SC_SUBCORES = 16
SC_WORKERS = SC_CORES * SC_SUBCORES
SC_TILE_BYTES = 384 * 1024
SPARSE_ROW_TILE = 512
SPARSE_MIN_TOKENS = 4096


def _sc_ring(per_w, row_bytes):
    for ch, nbuf in ((16, 4), (16, 2), (8, 2)):
        if per_w % (ch * nbuf) == 0 and ch * nbuf * row_bytes <= SC_TILE_BYTES:
            return ch, nbuf
    raise ValueError(f"no SparseCore gather ring for {per_w} rows of {row_bytes} bytes per subcore")


def _sc_gather(table, idx):
    R = idx.shape[0]
    D = table.shape[1]
    per_w = R // SC_WORKERS
    assert per_w * SC_WORKERS == R and per_w % 8 == 0
    ch, nbuf = _sc_ring(per_w, D * table.dtype.itemsize)
    nchunk = per_w // ch
    mesh = plsc.VectorSubcoreMesh(core_axis_name="c", subcore_axis_name="s")

    @functools.partial(
        pl.kernel, mesh=mesh,
        out_type=jax.ShapeDtypeStruct((R, D), table.dtype),
        scratch_types=[pltpu.VMEM((per_w,), jnp.int32), pltpu.VMEM((nbuf, ch, D), table.dtype),
                       pltpu.SemaphoreType.DMA((nbuf,)), pltpu.SemaphoreType.DMA((nbuf,))],
    )
    def gather_kernel(table_hbm, idx_hbm, out_hbm, idx_v, rows_v, gsem, wsem):
        base = (lax.axis_index("s") * SC_CORES + lax.axis_index("c")) * per_w
        pltpu.sync_copy(idx_hbm.at[pl.ds(base, per_w)], idx_v)

        def gather(c, b):
            return pltpu.make_async_copy(table_hbm.at[idx_v.at[pl.ds(c * ch, ch)]], rows_v.at[b], gsem.at[b])

        def write(c, b):
            return pltpu.make_async_copy(rows_v.at[b], out_hbm.at[pl.ds(base + c * ch, ch)], wsem.at[b])

        for b in range(nbuf - 1):
            gather(b, b).start()

        @pl.loop(0, nchunk, step=nbuf)
        def _(c):
            for b in range(nbuf):
                cc = c + b
                gather(cc, b).wait()
                write(cc, b).start()
                pb = (b - 1) % nbuf

                @pl.when(cc + nbuf - 1 < nchunk)
                def _():
                    @pl.when(cc >= 1)
                    def _():
                        write(cc - 1, pb).wait()

                    gather(cc + nbuf - 1, pb).start()

        for b in range(nbuf):
            write(nchunk - nbuf + b, b).wait()

    return gather_kernel(table, idx)


def _moe_group_kernel(te_ref, nv_ref, x_ref, wr_ref, *refs):
    w_refs, (g_ref, b_ref, o_ref) = refs[:3 * TOP_K], refs[3 * TOP_K:]
    j = pl.program_id(0)
    live = j < nv_ref[0]

    @pl.when(live)
    def _():
        x = x_ref[...]
        xb = x.astype(BF16)
        scores = jax.nn.sigmoid(_dot(xb, wr_ref[...]))
        lane = lax.broadcasted_iota(jnp.int32, scores.shape, 1)
        picked = [jnp.sum(jnp.where(lane == te_ref[TOP_K * j + k], scores, 0.0), -1, keepdims=True)
                  for k in range(TOP_K)]
        denom = picked[0] + picked[1]
        acc = None
        for k in range(TOP_K):
            wg_ref, wu_ref, wd_ref = w_refs[3 * k:3 * k + 3]
            h = jax.nn.silu(_dot(xb, wg_ref[0, 0].astype(BF16))) * _dot(xb, wu_ref[0, 0].astype(BF16))
            y = _dot(h.astype(BF16), wd_ref[0, 0].astype(BF16))
            gcol = picked[k] / denom
            acc = gcol * y if acc is None else acc + gcol * y
        o_ref[...] = _layer_norm(DN_ALPHA * x + acc, g_ref[...], b_ref[...])

    @pl.when(jnp.logical_not(live))
    def _():
        o_ref[...] = jnp.zeros_like(o_ref)


def _moe_group(tile_experts, n_valid, xs, wr_pad, wg, wu, wd, l, g, b, tm):
    R, D = xs.shape
    H = wg.shape[3]
    row = lambda j, te, nv: (j, 0)
    const = lambda j, te, nv: (0, 0)
    w_specs, w_args = [], []
    for k in range(TOP_K):
        wsel = lambda j, te, nv, k=k: (l, te[TOP_K * j + k], 0, 0)
        w_specs += [pl.BlockSpec((1, 1, D, H), wsel), pl.BlockSpec((1, 1, D, H), wsel),
                    pl.BlockSpec((1, 1, H, D), wsel)]
        w_args += [wg, wu, wd]
    grid_spec = pltpu.PrefetchScalarGridSpec(
        num_scalar_prefetch=2,
        grid=(R // tm,),
        in_specs=[pl.BlockSpec((tm, D), row), pl.BlockSpec(wr_pad.shape, const)] + w_specs + [
            pl.BlockSpec(g.shape, const), pl.BlockSpec(b.shape, const)],
        out_specs=pl.BlockSpec((tm, D), row),
    )
    return pl.pallas_call(
        _moe_group_kernel,
        grid_spec=grid_spec,
        out_shape=jax.ShapeDtypeStruct((R, D), F32),
        compiler_params=_cparams(("arbitrary",)),
        name="moe_group",
    )(tile_experts, n_valid, xs, wr_pad, *w_args, g, b)


def _sc_scatter(x, dest, pad_dest, n_rows):
    T, D = x.shape
    P = pad_dest.shape[0]
    per_w, pad_w = T // SC_WORKERS, P // SC_WORKERS
    assert per_w * SC_WORKERS == T and pad_w * SC_WORKERS == P and T + P == n_rows
    ch, nbuf = _sc_ring(per_w, D * x.dtype.itemsize)
    assert pad_w % ch == 0
    nchunk, npad = per_w // ch, pad_w // ch
    mesh = plsc.VectorSubcoreMesh(core_axis_name="c", subcore_axis_name="s")

    @functools.partial(
        pl.kernel, mesh=mesh,
        out_type=jax.ShapeDtypeStruct((n_rows, D), x.dtype),
        scratch_types=[pltpu.VMEM((nchunk, ch), jnp.int32), pltpu.VMEM((npad, ch), jnp.int32),
                       pltpu.VMEM((nbuf, ch, D), x.dtype), pltpu.VMEM((ch, D), x.dtype),
                       pltpu.SemaphoreType.DMA((nbuf,)), pltpu.SemaphoreType.DMA((nbuf,)),
                       pltpu.SemaphoreType.DMA],
    )
    def scatter_kernel(x_hbm, dest_hbm, pad_hbm, zero_hbm, out_hbm, dest_v, pad_v, rows_v, zero_v,
                       rsem, wsem, zsem):
        wid = lax.axis_index("s") * SC_CORES + lax.axis_index("c")
        base = wid * per_w
        pltpu.sync_copy(dest_hbm.at[wid], dest_v)
        pltpu.sync_copy(pad_hbm.at[wid], pad_v)
        pltpu.sync_copy(zero_hbm, zero_v)

        def read(c, b):
            return pltpu.make_async_copy(x_hbm.at[pl.ds(base + c * ch, ch)], rows_v.at[b], rsem.at[b])

        def write(c, b):
            return pltpu.make_async_copy(rows_v.at[b], out_hbm.at[dest_v.at[c]], wsem.at[b])

        def write_zero(c):
            return pltpu.make_async_copy(zero_v, out_hbm.at[pad_v.at[c]], zsem)

        for c in range(npad):
            write_zero(c).start()
        for b in range(nbuf - 1):
            read(b, b).start()

        @pl.loop(0, nchunk, step=nbuf)
        def _(c):
            for b in range(nbuf):
                cc = c + b
                read(cc, b).wait()
                write(cc, b).start()
                pb = (b - 1) % nbuf

                @pl.when(cc + nbuf - 1 < nchunk)
                def _():
                    @pl.when(cc >= 1)
                    def _():
                        write(cc - 1, pb).wait()

                    read(cc + nbuf - 1, pb).start()

        for b in range(nbuf):
            write(nchunk - nbuf + b, b).wait()
        for c in range(npad):
            write_zero(c).wait()

    return scatter_kernel(x, dest.reshape(SC_WORKERS, nchunk, ch), pad_dest.reshape(SC_WORKERS, npad, ch),
                          jnp.zeros((ch, D), x.dtype))


def _moe_sparse(x, wr_pad, route, counts, wg, wu, wd, l, g, b):
    T = x.shape[0]
    tm = SPARSE_ROW_TILE
    n_tiles = T // tm + N_SEG
    n_rows = n_tiles * tm
    sid, rank = route[0], route[1]
    cnt = counts[:N_SEG, 0].astype(jnp.int32)
    tiles = (cnt + tm - 1) // tm
    ends = jnp.cumsum(tiles)
    starts = (ends - tiles) * tm
    seg_ids = jnp.arange(N_SEG, dtype=jnp.int32)
    pick = lambda which, vals: jnp.sum(jnp.where(which[:, None] == seg_ids[None, :], vals[None, :], 0), 1)
    passed = lambda pos, bounds: jnp.sum((pos[:, None] >= bounds[None, :]).astype(jnp.int32), 1)
    dest = rank + pick(sid, starts)
    pad_cnt = tiles * tm - cnt
    pad_end = jnp.cumsum(pad_cnt)
    p = jnp.arange(n_rows - T, dtype=jnp.int32)
    seg = passed(p, pad_end)
    pad_dest = jnp.where(seg < N_SEG,
                         pick(seg, starts + cnt) + p - pick(seg, pad_end - pad_cnt),
                         ends[N_SEG - 1] * tm + p - pad_end[N_SEG - 1])
    tile_seg = jnp.minimum(passed(jnp.arange(n_tiles, dtype=jnp.int32), ends), N_SEG - 1)
    pair = jnp.asarray(np.asarray(EXPERT_PAIRS, np.int32))
    tile_experts = (tile_seg // N_PAIRS * EXPERTS_PER_GROUP)[:, None] + pair[tile_seg % N_PAIRS]
    xs = _sc_scatter(x, dest, pad_dest, n_rows)
    ys = _moe_group(tile_experts.reshape(-1), ends[N_SEG - 1:], xs, wr_pad, wg, wu, wd, l, g, b, tm)
    return _sc_gather(ys, dest)


def _prep_weights(w_in_even, w_uq, w_ukv, w_out_even, w_in_odd, w_out_odd, w_router,
                  w_expert_gate, w_expert_up, w_expert_down):
    n_main = 4 * RET_W + MLA_Q_RANK + MLA_KV_RANK
    w_in = w_in_even[0]
    kr_cols = jnp.pad(w_in[:, n_main:], ((0, 0), (MLA_NOPE, LANES - MLA_NOPE - MLA_ROPE)))
    w_a = jnp.concatenate([w_in[:, :n_main], kr_cols], 1).astype(BF16)
    qd = MLA_NOPE + MLA_ROPE
    wq = jnp.pad(w_uq[0].reshape(MLA_Q_RANK, MLA_HEADS, qd), ((0, 0), (0, 0), (0, LANES - qd)))
    wq = wq.reshape(MLA_Q_RANK, MLA_PAD).astype(BF16)
    wkv = w_ukv[0].reshape(MLA_KV_RANK, MLA_HEADS, MLA_NOPE + MLA_V)
    wk = jnp.pad(wkv[:, :, :MLA_NOPE], ((0, 0), (0, 0), (0, LANES - MLA_NOPE)))
    wk = wk.reshape(MLA_KV_RANK, MLA_PAD).astype(BF16)
    wvt = wkv[:, :, MLA_NOPE:].reshape(MLA_KV_RANK, MLA_HEADS * MLA_V).T.astype(BF16)
    e_np = np.zeros((MLA_ROPE, MLA_HEADS, LANES), np.float32)
    for j in range(MLA_ROPE):
        e_np[j, :, MLA_NOPE + j] = 1.0
    e_mat = jnp.asarray(e_np.reshape(MLA_ROPE, MLA_PAD)).astype(BF16)
    return dict(
        w_a=w_a, wq=wq, wk=wk, wvt=wvt, e_mat=e_mat,
        w_out_even=w_out_even[0].astype(BF16), w_in_odd=w_in_odd[0].astype(BF16),
        w_out_odd=w_out_odd[0].astype(BF16), wr_t=w_router.T.astype(BF16),
        wr_pad=jnp.pad(w_router, ((0, 0), (0, LANES - N_EXPERTS))).astype(BF16),
        wg=w_expert_gate, wu=w_expert_up, wd=w_expert_down,
    )


def _moe(routed, wts, l, ln_g, ln_b, tm):
    x, gates, route, counts = routed
    g, b = ln_g[l, 1][None], ln_b[l, 1][None]
    if x.shape[0] >= SPARSE_MIN_TOKENS:
        return _moe_sparse(x, wts["wr_pad"], route, counts, wts["wg"], wts["wu"], wts["wd"], l, g, b)
    return _moe_dense(x, gates, wts["wg"], wts["wu"], wts["wd"], l, g, b, tm)


def _trunk(x3, pos0, past, wts, prm):
    B, S, D = x3.shape
    T = B * S
    x = x3.reshape(T, D)
    tm = min(T, TOKEN_TILE)
    rep = max(tm // S, 1)
    pos = np.tile(pos0 + np.arange(S), rep)
    ln_g, ln_b = prm["ln_g"], prm["ln_b"]

    causal = past is None
    fuse_kv = causal and S % tm == 0
    outs = _even_in(x, wts["w_a"], wts["wq"], prm["gq"], prm["gkv"], _even_tables(pos), tm,
                    kv=(wts["wk"], wts["wvt"], S) if fuse_kv else None)
    rq, rk, rv, rg, q, lat, kr = outs[:7]
    if causal:
        state0 = jnp.zeros((B, RET_HEADS, RET_DK, RET_DV), F32)
        lat_all, kr_all, sk, skp = lat, kr, S, S
        tq = tk = min(S, ATTN_TILE)
    else:
        state0 = past["state"]
        sk = past["lat"].shape[1] + S
        skp = -(-sk // LANES) * LANES
        padk = lambda parts: jnp.concatenate(
            parts + [jnp.zeros((B, skp - sk, parts[0].shape[2]), parts[0].dtype)], 1).reshape(B * skp, -1)
        lat_all = padk([past["lat"], lat.reshape(B, S, -1)])
        kr_all = padk([past["kr"], kr.reshape(B, S, -1)])
        tq, tk = S, skp
    ret_out, ret_state = _retention(rq, rk, rv, rg, state0, B, S)
    if fuse_kv:
        k_mla, vt_mla = outs[7:]
    else:
        tkv = KV_UP_TILE if (B * skp) % KV_UP_TILE == 0 else skp
        k_mla, vt_mla = _kv_up(lat_all, kr_all, wts["wk"], wts["wvt"], wts["e_mat"], skp, tkv)
        if vt_mla.shape[0] != B:
            vt_mla = vt_mla.reshape(-1, B, skp).transpose(1, 0, 2)
    mla_out = _mla_attn(q, k_mla, vt_mla, B, S, skp, sk, tq, tk, causal)
    routed = _out_proj([ret_out, mla_out], wts["w_out_even"], x, ln_g[0, 0][None], ln_b[0, 0][None],
                       wts["wr_t"], prm["bias"], tm)
    x = _moe(routed, wts, 0, ln_g, ln_b, tm)

    qd, kt, kb, vf, vt = _odd_in(x, wts["w_in_odd"], _odd_tables(pos), S, tm)
    if vt.shape[0] != B:
        vt = vt.reshape(-1, B, S).transpose(1, 0, 2)
        kt = kt.reshape(-1, B, S).transpose(1, 0, 2)
    kf = kt.reshape(1, B, 2 * DIFF_HEADS, DIFF_HD, S).transpose(0, 1, 4, 2, 3)
    if causal:
        k_all, vt_all = kb, vt
    else:
        k_all = padk([past["dk"], kb.reshape(B, S, -1)])
        vt_all = jnp.concatenate([past["dv"].transpose(0, 2, 1), vt,
                                  jnp.zeros((B, vt.shape[1], skp - sk), BF16)], 2)
    lam_init = 0.8 - 0.6 * math.exp(-0.3 * 1)
    d_out = _diff_attn(prm["lam"], prm["gn"], qd, k_all, vt_all, B, S, skp, sk, tq, tk, causal, lam_init)
    routed = _out_proj([d_out], wts["w_out_odd"], x, ln_g[1, 0][None], ln_b[1, 0][None],
                       wts["wr_t"], prm["bias"], tm)
    x = _moe(routed, wts, 1, ln_g, ln_b, tm)

    return (x.reshape(B, S, D), ret_state[None], lat.reshape(1, B, S, -1), kr.reshape(1, B, S, -1),
            kf, vf.reshape(1, B, S, DIFF_HEADS, DIFF_V))


def kernel(x_prompt, x_sample, state_ret, cache_mla_latent, cache_mla_krope, cache_diff_k, cache_diff_v,
           w_in_even, w_uq, w_ukv, g_qnorm, g_kvnorm, w_out_even,
           w_in_odd, lambda_q1, lambda_k1, lambda_q2, lambda_k2, g_diff_norm, w_out_odd,
           ln_g, ln_b, w_router, router_bias, w_expert_gate, w_expert_up, w_expert_down):
    wts = _prep_weights(w_in_even, w_uq, w_ukv, w_out_even, w_in_odd, w_out_odd, w_router,
                        w_expert_gate, w_expert_up, w_expert_down)
    prm = dict(
        gq=g_qnorm[0][None].astype(F32), gkv=g_kvnorm[0][None].astype(F32),
        lam=jnp.stack([lambda_q1[0], lambda_k1[0], lambda_q2[0], lambda_k2[0]]).astype(F32),
        gn=g_diff_norm[0][None].astype(F32), bias=router_bias.reshape(N_EXPERTS, 1).astype(F32),
        ln_g=ln_g.astype(F32), ln_b=ln_b.astype(F32),
    )
    past_len = cache_mla_latent.shape[2]
    db = x_sample.shape[0]
    past = dict(
        state=state_ret[0].astype(F32), lat=cache_mla_latent[0], kr=cache_mla_krope[0],
        dk=cache_diff_k[0].reshape(db, past_len, -1).astype(BF16),
        dv=cache_diff_v[0].reshape(db, past_len, -1).astype(BF16),
    )
    outs_p = _trunk(x_prompt, 0, None, wts, prm)
    outs_s = _trunk(x_sample, past_len, past, wts, prm)
    return (outs_p[0], outs_s[0]) + outs_p[1:] + outs_s[1:]
```
